```python
import math
import jax
import jax.numpy as jnp
from jax import lax
import numpy as np

D_MODEL = 1024
BATCH = 32
SEQ = 256
DEPTH = 2
DEC_BATCH = 4
DEC_SEQ = 4096
PAST_LEN = 256

GRID_W = 64
D_MIX = D_MODEL
D_RNN = D_MIX // 2
D_HY = D_MIX - D_RNN
N_RNN_HEADS = 8
RNN_HEAD_DIM = D_RNN // N_RNN_HEADS
RG_CONV_W = 4
RG_C = 8.0
HY_CONV_W = 3
HY_ORDER = 2
HY_BANDS = 16
HY_EMB = 1 + 2 * HY_BANDS
HY_FH = 64
HY_DECAY_TARGET = 1e-2
HY_DECAY_PCT_SHORT = 0.3
HY_DECAY_PCT_LONG = 1.5
D_FF = ((8 * D_MODEL // 3 + 127) // 128) * 128
FFN_CONV_W = 3
D_IN = 2 * D_RNN + 3 * D_HY
N_MOD = 6
EPS = 1e-6

kernel_name = 'hymba_rglru_hyena_prefix_dit_step'


def rms_norm(x, g):
    x32 = x.astype(jnp.float32)
    y = x32 * lax.rsqrt(jnp.mean(x32 * x32, axis=-1, keepdims=True) + EPS)
    return y.astype(x.dtype) * g


def dwconv1d(x, w, b):
    K = w.shape[0]
    L = x.shape[1]
    left = K // 2
    xp = jnp.pad(x, ((0, 0), (left, K - 1 - left), (0, 0)))
    out = xp[:, 0:L] * w[0]
    for k in range(1, K):
        out = out + xp[:, k:k + L] * w[k]
    return out + b


def dwconv2d_grid(x, w, b):
    B, L, C = x.shape
    rows = L // GRID_W
    xg = x.reshape(B, rows, GRID_W, C)
    y = lax.conv_general_dilated(xg, w[:, :, None, :], window_strides=(1, 1), padding='SAME',
                                 dimension_numbers=('NHWC', 'HWIO', 'NHWC'), feature_group_count=C)
    return y.reshape(B, L, C) + b


def _lin_combine(e1, e2):
    a1, b1 = e1
    a2, b2 = e2
    return a1 * a2, a2 * b1 + b2


def rglru(x, h0, gate_w, gate_b, a_param, reverse):
    B, L, _ = x.shape
    x32 = x.astype(jnp.float32)
    xh = x32.reshape(B, L, N_RNN_HEADS, RNN_HEAD_DIM)
    g = jnp.einsum('blhi,ghij->gblhj', xh, gate_w.astype(jnp.float32)).reshape(2, B, L, D_RNN)
    g = g + gate_b.astype(jnp.float32)[:, None, None, :]
    r = jax.nn.sigmoid(g[0])
    i = jax.nn.sigmoid(g[1])
    log_a = -RG_C * r * jax.nn.softplus(-a_param.astype(jnp.float32))
    a = jnp.exp(log_a)
    b = x32 * i * jnp.sqrt(-jnp.expm1(2.0 * log_a))
    A, Bc = lax.associative_scan(_lin_combine, (a, b), reverse=reverse, axis=1)
    return A * h0.astype(jnp.float32)[:, None, :] + Bc


def hyena_filter_fft(L, w1, b1, w2, b2, w3, freq):
    f32 = jnp.float32
    pos = jnp.arange(L, dtype=f32)
    t = pos / max(L - 1, 1)
    omega = 2.0 * math.pi * pos / L
    bands = jnp.linspace(1e-4, HY_BANDS - 1, HY_BANDS, dtype=f32)
    ang = omega[:, None] * bands[None, :]
    feats = jnp.concatenate([t[:, None], jnp.cos(ang), jnp.sin(ang)], axis=-1)
    fr = freq.astype(f32)
    h = jnp.sin(fr[0] * (feats @ w1.astype(f32) + b1.astype(f32)))
    h = jnp.sin(fr[1] * (h @ w2.astype(f32) + b2.astype(f32)))
    h = (h @ w3.astype(f32)).reshape(L, HY_ORDER, 2, D_HY)
    min_decay = math.log(HY_DECAY_TARGET) / HY_DECAY_PCT_LONG
    max_decay = math.log(HY_DECAY_TARGET) / HY_DECAY_PCT_SHORT
    deltas = jnp.abs(jnp.linspace(min_decay, max_decay, D_HY, dtype=f32))
    h = h * jnp.exp(-t[:, None] * deltas[None, :])[:, None, None, :]
    h_fwd = h[:, :, 0]
    h_bwd = h[:, :, 1]
    circ = jnp.concatenate([h_fwd, jnp.zeros((1, HY_ORDER, D_HY), f32), h_bwd[:0:-1]], axis=0)
    return jnp.fft.rfft(circ, axis=0)


def fft_long_conv(u, k_f, bias):
    L = u.shape[1]
    U = jnp.fft.rfft(u, n=2 * L, axis=1)
    y = jnp.fft.irfft(U * k_f[None], n=2 * L, axis=1)[:, :L]
    return y + u * bias.astype(jnp.float32)


def trunk_layer(x, mod, h0_f, h0_b, p, on_grid):
    shift1, scale1, gate1, shift2, scale2, gate2 = jnp.split(mod[:, None, :], N_MOD, axis=-1)
    xn = rms_norm(x, p['g_norm1']) * (1 + scale1) + shift1
    proj = xn @ p['w_in']
    x_r = proj[..., :D_RNN]
    y_r = proj[..., D_RNN:2 * D_RNN]
    hy_in = proj[..., 2 * D_RNN:]
    xc = dwconv1d(x_r, p['rg_conv_w'], p['rg_conv_b'])
    h_f = rglru(xc, h0_f, p['rg_gate_w'][0], p['rg_gate_b'][0], p['rg_a'][0], False)
    h_b = rglru(xc, h0_b, p['rg_gate_w'][1], p['rg_gate_b'][1], p['rg_a'][1], True)
    o_r = ((h_f + h_b) * jax.nn.gelu(y_r.astype(jnp.float32))).astype(x.dtype)
    hy = dwconv1d(hy_in, p['hy_conv_w'], p['hy_conv_b']).astype(jnp.float32)
    v, x1, x2 = jnp.split(hy, 3, axis=-1)
    k_f = hyena_filter_fft(x.shape[1], p['hf_w1'], p['hf_b1'], p['hf_w2'], p['hf_b2'], p['hf_w3'], p['hf_freq'])
    z = x1 * fft_long_conv(v, k_f[:, 0], p['hy_bias'][0])
    z = x2 * fft_long_conv(z, k_f[:, 1], p['hy_bias'][1])
    o_h = z.astype(x.dtype)
    o = jnp.concatenate([rms_norm(o_r, p['g_rnn_out']), rms_norm(o_h, p['g_hy_out'])], axis=-1) @ p['w_out']
    x = x + gate1 * o
    xn2 = rms_norm(x, p['g_norm2']) * (1 + scale2) + shift2
    a_ff, g_ff = jnp.split(xn2 @ p['w_up'], 2, axis=-1)
    if on_grid:
        g_ff = dwconv2d_grid(g_ff, p['ffn_conv_w'], p['ffn_conv_b'])
    else:
        g_ff = dwconv1d(g_ff, p['ffn_conv_w'][1], p['ffn_conv_b'])
    x = x + gate2 * ((jax.nn.gelu(g_ff) * a_ff) @ p['w_down'])
    return x, h_f[:, -1], h_b[:, 0]


def setup_inputs(seed: int = 0) -> dict:
    key = jax.random.key(seed)
    ks = iter(jax.random.split(key, 48))

    def nrm(shape, scale):
        return scale * jax.random.normal(next(ks), shape, jnp.float32)

    x_prompt = nrm((BATCH, SEQ, D_MODEL), 1.0)
    x_sample = nrm((DEC_BATCH, DEC_SEQ, D_MODEL), 1.0)
    state_rglru = nrm((DEC_BATCH, DEPTH, 2, D_RNN), 0.5)
    c = nrm((DEC_BATCH, D_MODEL), 1.0)
    c_ctx = nrm((D_MODEL,), 1.0)
    w_ada = nrm((DEPTH, D_MODEL, N_MOD * D_MODEL), 0.5 * D_MODEL ** -0.5)
    b_ada = nrm((DEPTH, N_MOD * D_MODEL), 0.02)
    g_norm1 = 1.0 + nrm((DEPTH, D_MODEL), 0.02)
    g_norm2 = 1.0 + nrm((DEPTH, D_MODEL), 0.02)
    w_in = nrm((DEPTH, D_MODEL, D_IN), D_MODEL ** -0.5)
    rg_conv_w = nrm((DEPTH, RG_CONV_W, D_RNN), RG_CONV_W ** -0.5)
    rg_conv_b = nrm((DEPTH, D_RNN), 0.02)
    rg_gate_w = nrm((DEPTH, 2, 2, N_RNN_HEADS, RNN_HEAD_DIM, RNN_HEAD_DIM), RNN_HEAD_DIM ** -0.5)
    rg_gate_b = nrm((DEPTH, 2, 2, D_RNN), 0.02)
    u = jax.random.uniform(next(ks), (DEPTH, 2, D_RNN), jnp.float32, 0.9, 0.999)
    a_base = u ** (1.0 / RG_C)
    rg_a = jnp.log(a_base) - jnp.log1p(-a_base)
    hy_conv_w = nrm((DEPTH, HY_CONV_W, 3 * D_HY), HY_CONV_W ** -0.5)
    hy_conv_b = nrm((DEPTH, 3 * D_HY), 0.02)
    hf_w1 = nrm((DEPTH, HY_EMB, HY_FH), HY_EMB ** -0.5)
    hf_b1 = nrm((DEPTH, HY_FH), 0.1)
    hf_w2 = nrm((DEPTH, HY_FH, HY_FH), HY_FH ** -0.5)
    hf_b2 = nrm((DEPTH, HY_FH), 0.1)
    hf_w3 = nrm((DEPTH, HY_FH, HY_ORDER * 2 * D_HY), HY_FH ** -0.5)
    hf_freq = 1.0 + nrm((DEPTH, 2, HY_FH), 0.1)
    hy_bias = nrm((DEPTH, HY_ORDER, D_HY), 0.5)
    g_rnn_out = 1.0 + nrm((DEPTH, D_RNN), 0.02)
    g_hy_out = 1.0 + nrm((DEPTH, D_HY), 0.02)
    w_out = nrm((DEPTH, D_MIX, D_MODEL), D_MIX ** -0.5)
    w_up = nrm((DEPTH, D_MODEL, 2 * D_FF), D_MODEL ** -0.5)
    ffn_conv_w = nrm((DEPTH, FFN_CONV_W, FFN_CONV_W, D_FF), 1.0 / FFN_CONV_W)
    ffn_conv_b = nrm((DEPTH, D_FF), 0.02)
    w_down = nrm((DEPTH, D_FF, D_MODEL), D_FF ** -0.5)
    g_final = 1.0 + nrm((D_MODEL,), 0.02)
    return {'x_prompt': x_prompt, 'x_sample': x_sample, 'state_rglru': state_rglru, 'c': c, 'c_ctx': c_ctx,
            'w_ada': w_ada, 'b_ada': b_ada, 'g_norm1': g_norm1, 'g_norm2': g_norm2, 'w_in': w_in,
            'rg_conv_w': rg_conv_w, 'rg_conv_b': rg_conv_b, 'rg_gate_w': rg_gate_w, 'rg_gate_b': rg_gate_b,
            'rg_a': rg_a, 'hy_conv_w': hy_conv_w, 'hy_conv_b': hy_conv_b, 'hf_w1': hf_w1, 'hf_b1': hf_b1,
            'hf_w2': hf_w2, 'hf_b2': hf_b2, 'hf_w3': hf_w3, 'hf_freq': hf_freq, 'hy_bias': hy_bias,
            'g_rnn_out': g_rnn_out, 'g_hy_out': g_hy_out, 'w_out': w_out, 'w_up': w_up,
            'ffn_conv_w': ffn_conv_w, 'ffn_conv_b': ffn_conv_b, 'w_down': w_down, 'g_final': g_final}


def reference(x_prompt, x_sample, state_rglru, c, c_ctx, w_ada, b_ada, g_norm1, g_norm2, w_in,
              rg_conv_w, rg_conv_b, rg_gate_w, rg_gate_b, rg_a, hy_conv_w, hy_conv_b, hf_w1, hf_b1,
              hf_w2, hf_b2, hf_w3, hf_freq, hy_bias, g_rnn_out, g_hy_out, w_out, w_up,
              ffn_conv_w, ffn_conv_b, w_down, g_final):
    xp = x_prompt
    xs = x_sample
    zero_h = jnp.zeros((x_prompt.shape[0], D_RNN), jnp.float32)
    new_states = []
    for l in range(DEPTH):
        p = {'g_norm1': g_norm1[l], 'g_norm2': g_norm2[l], 'w_in': w_in[l],
             'rg_conv_w': rg_conv_w[l], 'rg_conv_b': rg_conv_b[l], 'rg_gate_w': rg_gate_w[l],
             'rg_gate_b': rg_gate_b[l], 'rg_a': rg_a[l], 'hy_conv_w': hy_conv_w[l], 'hy_conv_b': hy_conv_b[l],
             'hf_w1': hf_w1[l], 'hf_b1': hf_b1[l], 'hf_w2': hf_w2[l], 'hf_b2': hf_b2[l], 'hf_w3': hf_w3[l],
             'hf_freq': hf_freq[l], 'hy_bias': hy_bias[l], 'g_rnn_out': g_rnn_out[l], 'g_hy_out': g_hy_out[l],
             'w_out': w_out[l], 'w_up': w_up[l], 'ffn_conv_w': ffn_conv_w[l], 'ffn_conv_b': ffn_conv_b[l],
             'w_down': w_down[l]}
        mod_ctx = jax.nn.silu(c_ctx)[None, :] @ w_ada[l] + b_ada[l]
        mod_lat = jax.nn.silu(c) @ w_ada[l] + b_ada[l]
        xp, hf_last, hb_first = trunk_layer(xp, mod_ctx, zero_h, zero_h, p, False)
        new_states.append(jnp.stack([hf_last, hb_first], axis=1).astype(x_prompt.dtype))
        xs, _, _ = trunk_layer(xs, mod_lat, state_rglru[:, l, 0], state_rglru[:, l, 1], p, True)
    y_prompt = rms_norm(xp, g_final)
    y_sample = rms_norm(xs, g_final)
    new_state_rglru = jnp.stack(new_states, axis=1)
    return (y_prompt, y_sample, new_state_rglru)
```

```python
import functools
import math

import jax
import jax.numpy as jnp
import ml_dtypes
import numpy as np
from jax import lax
from jax.experimental import pallas as pl
from jax.experimental.pallas import tpu as pltpu

F32 = jnp.float32
BF16 = jnp.bfloat16

D_MODEL = 1024
D_RNN = 512
D_HY = 512
N_HEADS = 8
HEAD_DIM = D_RNN // N_HEADS
RG_C = 8.0
GRID_W = 64
HY_BANDS = 16
HY_FH = 64
D_FF = 2816
N_MOD = 6
EPS = 1e-6

SUBLANES = 8
LANES = 128
VMEM_LIMIT = 56 * 1024 * 1024

DFT2 = LANES
FFT_PASSES = 3


def _cparams(sem):
    return pltpu.CompilerParams(dimension_semantics=sem, vmem_limit_bytes=VMEM_LIMIT)


def _rms(x):
    return x * lax.rsqrt(jnp.mean(x * x, axis=-1, keepdims=True) + EPS)


def _mod_kernel(c_ref, w_ref, b_ref, o_ref):
    c = c_ref[...]
    s = c * jax.nn.sigmoid(c)
    o_ref[0] = jnp.dot(s, w_ref[0], precision=lax.Precision.HIGHEST, preferred_element_type=F32) + b_ref[0]


def _modulation(cc, w_ada, b_ada):
    depth, _, n = w_ada.shape
    tn = 1536
    return pl.pallas_call(
        _mod_kernel,
        grid=(depth, n // tn),
        in_specs=[pl.BlockSpec((SUBLANES, D_MODEL), lambda l, j: (0, 0)),
                  pl.BlockSpec((1, D_MODEL, tn), lambda l, j: (l, 0, j)),
                  pl.BlockSpec((1, 1, tn), lambda l, j: (l, 0, j))],
        out_specs=pl.BlockSpec((1, SUBLANES, tn), lambda l, j: (l, 0, j)),
        out_shape=jax.ShapeDtypeStruct((depth, SUBLANES, n), F32),
        compiler_params=_cparams(("parallel", "parallel")),
        name="adaln_mod",
    )(cc, w_ada, b_ada.reshape(depth, 1, n))


def _in_proj_kernel(x_ref, mod_ref, g_ref, wxy_ref, whyt_ref, xy_ref, hyt_ref):
    x = x_ref[0]
    xn = _rms(x) * g_ref[...]
    xn = (xn * (1.0 + mod_ref[0, 1:2, :]) + mod_ref[0, 0:1, :]).astype(BF16)
    xy_ref[0] = jnp.dot(xn, wxy_ref[...], preferred_element_type=F32)
    hyt_ref[0] = lax.dot_general(whyt_ref[...], xn, (((1,), (1,)), ((), ())), preferred_element_type=F32)


def _in_proj(x, mod, g, wxy, whyt, tm):
    b, l, _ = x.shape
    per_batch_mod = mod.shape[0] > 1
    mod_map = (lambda i, j: (i, 0, 0)) if per_batch_mod else (lambda i, j: (0, 0, 0))
    return pl.pallas_call(
        _in_proj_kernel,
        grid=(b, l // tm),
        in_specs=[pl.BlockSpec((1, tm, D_MODEL), lambda i, j: (i, j, 0)),
                  pl.BlockSpec((1, N_MOD, D_MODEL), mod_map),
                  pl.BlockSpec((1, D_MODEL), lambda i, j: (0, 0)),
                  pl.BlockSpec((D_MODEL, 2 * D_RNN), lambda i, j: (0, 0)),
                  pl.BlockSpec((3 * D_HY, D_MODEL), lambda i, j: (0, 0))],
        out_specs=[pl.BlockSpec((1, tm, 2 * D_RNN), lambda i, j: (i, j, 0)),
                   pl.BlockSpec((1, 3 * D_HY, tm), lambda i, j: (i, 0, j))],
        out_shape=[jax.ShapeDtypeStruct((b, l, 2 * D_RNN), F32),
                   jax.ShapeDtypeStruct((b, 3 * D_HY, l), F32)],
        compiler_params=_cparams(("parallel", "parallel")),
        name="in_proj",
    )(x, mod, g, wxy, whyt)


RG_HALF = D_RNN // 2
RG_TILES = RG_HALF // LANES


def _rglru_kernel(xr_ref, yr_ref, cw_ref, cb_ref, wg_ref, bg_ref, ap_ref, h0_ref, o_ref, st_ref,
                  ext, hf, hb, a_s, b_s, hl_s, p_s, *, seq, tc):
    sub = tc // SUBLANES
    nchunks = seq // tc
    ext[0:SUBLANES, :] = jnp.zeros((SUBLANES, RG_HALF), F32)
    ext[SUBLANES:SUBLANES + seq, :] = xr_ref[0]
    ext[SUBLANES + seq:2 * SUBLANES + seq, :] = jnp.zeros((SUBLANES, RG_HALF), F32)
    sub_iota = lax.broadcasted_iota(jnp.int32, (SUBLANES, LANES), 0)

    def gates(c0, d):
        xc = cb_ref[...] + cw_ref[0:1, :] * ext[pl.ds(c0 + SUBLANES - 2, tc), :]
        for k in range(1, 4):
            xc = xc + cw_ref[k:k + 1, :] * ext[pl.ds(c0 + SUBLANES - 2 + k, tc), :]
        g = jnp.dot(xc.astype(BF16), wg_ref[d, 0], preferred_element_type=F32) + bg_ref[d, 0]
        r = jax.nn.sigmoid(g[:, :RG_HALF])
        i = jax.nn.sigmoid(g[:, RG_HALF:])
        sp = jax.nn.softplus(-ap_ref[d, 0])
        log_a = -RG_C * r * sp
        a = jnp.exp(log_a)
        mult = jnp.sqrt(jnp.tanh(-log_a) * (a * a + 1.0))
        bb = xc * i * mult
        for t in range(RG_TILES):
            a_s[t] = a[:, t * LANES:(t + 1) * LANES]
            b_s[t] = bb[:, t * LANES:(t + 1) * LANES]

    def scan_chunk(c0, carry, dst, reverse):
        def body(jj, hp):
            j = (sub - 1 - jj) if reverse else jj
            out = []
            for t in range(RG_TILES):
                h8, p8 = hp[t]
                a8 = a_s[t, pl.ds(j, SUBLANES, stride=sub), :]
                b8 = b_s[t, pl.ds(j, SUBLANES, stride=sub), :]
                h8 = a8 * h8 + b8
                p8 = p8 * a8
                hl_s[t, pl.ds(j, SUBLANES, stride=sub), :] = h8
                p_s[t, pl.ds(j, SUBLANES, stride=sub), :] = p8
                out.append((h8, p8))
            return tuple(out)

        init = tuple((jnp.zeros((SUBLANES, LANES), F32), jnp.ones((SUBLANES, LANES), F32)) for _ in range(RG_TILES))
        fin = lax.fori_loop(0, sub, body, init)
        new_carry = []
        for t in range(RG_TILES):
            h8, p8 = fin[t]
            cur = carry[t]
            hs = jnp.zeros((SUBLANES, LANES), F32)
            order = range(SUBLANES - 1, -1, -1) if reverse else range(SUBLANES)
            for k in order:
                hs = jnp.where(sub_iota == k, cur, hs)
                cur = p8[k:k + 1, :] * cur + h8[k:k + 1, :]
            new_carry.append(cur)
            for k in range(SUBLANES):
                rows = slice(k * sub, (k + 1) * sub)
                dst[pl.ds(c0 + k * sub, sub), t * LANES:(t + 1) * LANES] = (
                    hl_s[t, rows, :] + p_s[t, rows, :] * hs[k:k + 1, :])
        return new_carry

    cf = [h0_ref[0, 0:1, t * LANES:(t + 1) * LANES] for t in range(RG_TILES)]
    for c in range(nchunks):
        gates(c * tc, 0)
        cf = scan_chunk(c * tc, cf, hf, False)

    cb = [h0_ref[0, 1:2, t * LANES:(t + 1) * LANES] for t in range(RG_TILES)]
    for c in range(nchunks - 1, -1, -1):
        c0 = c * tc
        gates(c0, 1)
        cb = scan_chunk(c0, cb, hb, True)
        y = yr_ref[0, pl.ds(c0, tc), :]
        o_ref[0, pl.ds(c0, tc), :] = (hf[pl.ds(c0, tc), :] + hb[pl.ds(c0, tc), :]) * jax.nn.gelu(y)
    for t in range(RG_TILES):
        st_ref[0, 0:1, t * LANES:(t + 1) * LANES] = cf[t]
        st_ref[0, 1:2, t * LANES:(t + 1) * LANES] = cb[t]


def _rglru(xy, cw, cb, wg, bg, ap, h0):
    b, seq, _ = xy.shape
    tc = min(512, seq)
    nh = D_RNN // RG_HALF
    kern = functools.partial(_rglru_kernel, seq=seq, tc=tc)
    return pl.pallas_call(
        kern,
        grid=(b, nh),
        in_specs=[pl.BlockSpec((1, seq, RG_HALF), lambda i, h: (i, 0, h)),
                  pl.BlockSpec((1, seq, RG_HALF), lambda i, h: (i, 0, nh + h)),
                  pl.BlockSpec((4, RG_HALF), lambda i, h: (0, h)),
                  pl.BlockSpec((1, RG_HALF), lambda i, h: (0, h)),
                  pl.BlockSpec((2, 1, RG_HALF, 2 * RG_HALF), lambda i, h: (0, h, 0, 0)),
                  pl.BlockSpec((2, 1, 1, 2 * RG_HALF), lambda i, h: (0, h, 0, 0)),
                  pl.BlockSpec((2, 1, 1, RG_HALF), lambda i, h: (0, h, 0, 0)),
                  pl.BlockSpec((1, 2, RG_HALF), lambda i, h: (i, 0, h))],
        out_specs=[pl.BlockSpec((1, seq, RG_HALF), lambda i, h: (i, 0, h)),
                   pl.BlockSpec((1, 2, RG_HALF), lambda i, h: (i, 0, h))],
        out_shape=[jax.ShapeDtypeStruct((b, seq, D_RNN), F32),
                   jax.ShapeDtypeStruct((b, 2, D_RNN), F32)],
        scratch_shapes=[pltpu.VMEM((seq + 2 * SUBLANES, RG_HALF), F32),
                        pltpu.VMEM((seq, RG_HALF), F32),
                        pltpu.VMEM((seq, RG_HALF), F32),
                        pltpu.VMEM((RG_TILES, tc, LANES), F32),
                        pltpu.VMEM((RG_TILES, tc, LANES), F32),
                        pltpu.VMEM((RG_TILES, tc, LANES), F32),
                        pltpu.VMEM((RG_TILES, tc, LANES), F32)],
        compiler_params=_cparams(("parallel", "parallel")),
        name="rglru",
    )(xy, xy, cw, cb, wg, bg, ap, h0)


def _gate_weights(gate_w, gate_b, a_param):
    nh = D_RNN // RG_HALF
    hp = N_HEADS // nh
    w = gate_w.reshape(2, 2, nh, hp, HEAD_DIM, HEAD_DIM)
    eye = jnp.eye(hp, dtype=F32)
    dense = jnp.einsum('dghpio,pq->dhpigqo', w, eye)
    dense = dense.reshape(2, nh, RG_HALF, 2 * RG_HALF).astype(BF16)
    bias = gate_b.reshape(2, 2, nh, RG_HALF).transpose(0, 2, 1, 3).reshape(2, nh, 1, 2 * RG_HALF)
    ap = a_param.reshape(2, nh, 1, RG_HALF)
    return dense, bias, ap


def _add(a, b):
    if a is None:
        return b
    if b is None:
        return a
    return a + b


def _sub(a, b):
    if b is None:
        return a
    if a is None:
        return -b
    return a - b


def _scale(a, s):
    if a is None or s == 0.0:
        return None
    if s == 1.0:
        return a
    if s == -1.0:
        return -a
    return a * s


def _cmul_const(z, wr, wi):
    if z is None:
        return None
    re, im = z
    if abs(wr) < 1e-15:
        wr = 0.0
    if abs(wi) < 1e-15:
        wi = 0.0
    return (_sub(_scale(re, wr), _scale(im, wi)), _add(_scale(re, wi), _scale(im, wr)))


def _cadd(a, b):
    if a is None:
        return b
    if b is None:
        return a
    return (_add(a[0], b[0]), _add(a[1], b[1]))


def _csub(a, b):
    if b is None:
        return a
    if a is None:
        return (_sub(None, b[0]), _sub(None, b[1]))
    return (_sub(a[0], b[0]), _sub(a[1], b[1]))


def _cfft(xs, sign):
    n = len(xs)
    if n == 1:
        return list(xs)
    even = _cfft(xs[0::2], sign)
    odd = _cfft(xs[1::2], sign)
    out = [None] * n
    for k in range(n // 2):
        ang = sign * 2.0 * math.pi * k / n
        t = _cmul_const(odd[k], math.cos(ang), math.sin(ang))
        out[k] = _cadd(even[k], t)
        out[k + n // 2] = _csub(even[k], t)
    return out


def _dft_tables(n1):
    n = n1 * DFT2
    k = np.arange(DFT2, dtype=np.float64)
    ang = 2.0 * np.pi * np.outer(k, k) / DFT2
    c, s = np.cos(ang), np.sin(ang)
    fwd = np.block([[c, -s], [s, c]])
    inv = np.block([[c, s], [-s, c]]) / n

    def split(m):
        hi = m.astype(np.float32).astype(ml_dtypes.bfloat16)
        lo = (m - hi.astype(np.float64)).astype(ml_dtypes.bfloat16)
        return jnp.asarray(hi), jnp.asarray(lo)

    tw_ang = 2.0 * np.pi * np.outer(np.arange(n1, dtype=np.float64), k) / n
    tw = np.concatenate([np.cos(tw_ang), -np.sin(tw_ang)], axis=1)
    tw = np.repeat(tw, SUBLANES, axis=0).astype(np.float32)
    return split(fwd) + split(inv) + (jnp.asarray(tw),)


def _dft_mm(x, hi_ref, lo_ref):
    xh = x.astype(BF16)
    acc = jnp.dot(xh, hi_ref[...], preferred_element_type=F32)
    if FFT_PASSES >= 2:
        acc = acc + jnp.dot(xh, lo_ref[...], preferred_element_type=F32)
    if FFT_PASSES >= 3:
        xl = (x - xh.astype(F32)).astype(BF16)
        acc = acc + jnp.dot(xl, hi_ref[...], preferred_element_type=F32)
    return acc


def _fwd_lane_fft(blocks, tw_ref, n1):
    xs = _cfft(blocks, -1)
    out = []
    for k1 in range(n1):
        z = xs[k1]
        if k1 > 0 and z is not None:
            twr = tw_ref[k1 * SUBLANES:(k1 + 1) * SUBLANES, 0:LANES]
            twi = tw_ref[k1 * SUBLANES:(k1 + 1) * SUBLANES, LANES:2 * LANES]
            re, im = z
            if im is None:
                z = (re * twr, re * twi)
            else:
                z = (re * twr - im * twi, re * twi + im * twr)
        out.append(z)
    return out


def _inv_lane_fft(blocks, tw_ref, n1):
    gs = []
    for k1 in range(n1):
        re, im = blocks[k1]
        if k1 > 0:
            twr = tw_ref[k1 * SUBLANES:(k1 + 1) * SUBLANES, 0:LANES]
            twi = tw_ref[k1 * SUBLANES:(k1 + 1) * SUBLANES, LANES:2 * LANES]
            re, im = re * twr + im * twi, im * twr - re * twi
        gs.append((re, im))
    return _cfft(gs, +1)


def _filter_kernel(feat_ref, w1_ref, b1_ref, w2_ref, b2_ref, fr_ref, w3_ref, fh_ref, fl_ref, tw_ref, kf_ref,
                   h2_s, k_s, s_s, *, seq, n1, cb):
    n = 2 * seq
    hi = lax.Precision.HIGHEST

    @pl.when(pl.program_id(0) == 0)
    def _():
        h = jnp.dot(w1_ref[...], feat_ref[...], precision=hi, preferred_element_type=F32) + b1_ref[...]
        h = jnp.sin(fr_ref[:, 0:1] * h)
        h = jnp.dot(w2_ref[...], h, precision=hi, preferred_element_type=F32) + b2_ref[...]
        h2_s[...] = jnp.sin(fr_ref[:, 1:2] * h)

    lane = lax.broadcasted_iota(jnp.int32, (cb, n), 1)
    pos = jnp.where(lane < seq, lane, n - lane).astype(F32)
    t = pos / float(max(seq - 1, 1))
    d_idx = (lax.broadcasted_iota(jnp.int32, (cb, n), 0) + pl.program_id(0) * cb).astype(F32)
    min_decay = math.log(1e-2) / 1.5
    max_decay = math.log(1e-2) / 0.3
    delta = jnp.abs(min_decay + d_idx * ((max_decay - min_decay) / (D_HY - 1)))
    decay = jnp.exp(-t * delta)
    for o in range(2):
        hf = jnp.dot(w3_ref[2 * o], h2_s[...], precision=hi, preferred_element_type=F32)
        hb = jnp.dot(w3_ref[2 * o + 1], h2_s[...], precision=hi, preferred_element_type=F32)
        kk = jnp.where(lane < seq, hf, jnp.where(lane == seq, 0.0, hb)) * decay
        k_s[o] = kk

    nrc = cb // SUBLANES
    for o in range(2):
        def rows(rc, _, o=o):
            r0 = pl.multiple_of(rc * SUBLANES, SUBLANES)
            blocks = [(k_s[o, pl.ds(r0, SUBLANES), j * LANES:(j + 1) * LANES], None) for j in range(n1)]
            zs = _fwd_lane_fft(blocks, tw_ref, n1)
            for k1 in range(n1):
                re, im = zs[k1]
                s_s[pl.ds(k1 * cb + r0, SUBLANES), 0:LANES] = re
                s_s[pl.ds(k1 * cb + r0, SUBLANES), LANES:2 * LANES] = im if im is not None else jnp.zeros_like(re)
            return 0
        lax.fori_loop(0, nrc, rows, 0)
        x = s_s[...]
        xh = x.astype(BF16)
        xl = (x - xh.astype(F32)).astype(BF16)
        z = (jnp.dot(xh, fh_ref[...], preferred_element_type=F32) + jnp.dot(xh, fl_ref[...], preferred_element_type=F32)
             + jnp.dot(xl, fh_ref[...], preferred_element_type=F32))
        for k1 in range(n1):
            kf_ref[o, k1] = z[k1 * cb:(k1 + 1) * cb, :]


def _hyena_filters(seq, w1, b1, w2, b2, w3, freq, tables):
    n1 = 2 * seq // DFT2
    n = 2 * seq
    cb = 32
    fh, fl, _, _, tw = tables
    pos = np.arange(n, dtype=np.float64)
    pos = np.where(pos < seq, pos, n - pos)
    tt = pos / max(seq - 1, 1)
    omega = 2.0 * math.pi * pos / seq
    bands = np.linspace(1e-4, HY_BANDS - 1, HY_BANDS)
    ang = omega[None, :] * bands[:, None]
    feats = np.concatenate([tt[None, :], np.cos(ang), np.sin(ang)], axis=0)
    nfeat = LANES
    feats = np.pad(feats, ((0, nfeat - feats.shape[0]), (0, 0))).astype(np.float32)
    w1 = jnp.pad(w1, ((0, nfeat - w1.shape[0]), (0, 0)))
    w3t = w3.T.reshape(4, D_HY, HY_FH)
    kern = functools.partial(_filter_kernel, seq=seq, n1=n1, cb=cb)
    full = lambda *shape: pl.BlockSpec(shape, lambda i: (0,) * len(shape))
    return pl.pallas_call(
        kern,
        grid=(D_HY // cb,),
        in_specs=[full(nfeat, n), full(HY_FH, nfeat), full(HY_FH, 1), full(HY_FH, HY_FH), full(HY_FH, 1),
                  full(HY_FH, 2),
                  pl.BlockSpec((4, cb, HY_FH), lambda i: (0, i, 0)),
                  full(2 * DFT2, 2 * DFT2), full(2 * DFT2, 2 * DFT2), full(n1 * SUBLANES, 2 * DFT2)],
        out_specs=pl.BlockSpec((2, n1, cb, 2 * DFT2), lambda i: (0, 0, i, 0)),
        out_shape=jax.ShapeDtypeStruct((2, n1, D_HY, 2 * DFT2), F32),
        scratch_shapes=[pltpu.VMEM((HY_FH, n), F32), pltpu.VMEM((2, cb, n), F32),
                        pltpu.VMEM((n1 * cb, 2 * DFT2), F32)],
        compiler_params=_cparams(("arbitrary",)),
        name="hyena_filters",
    )(jnp.asarray(feats), w1.T, b1.reshape(HY_FH, 1), w2.T, b2.reshape(HY_FH, 1), freq.T, w3t, fh, fl, tw)


def _hyena_kernel(v_ref, x1_ref, x2_ref, cw_ref, cb_ref, bias_ref, kf_ref, fh_ref, fl_ref, ih_ref, il_ref, tw_ref,
                  o_ref, v_s, z_s, s_s, *, seq, n1, cb, nb):
    npairs = nb // 2
    nin = n1 // 2
    nrc = cb // SUBLANES
    lane = lax.broadcasted_iota(jnp.int32, (SUBLANES, seq), 1)

    def conv3(h, part, r0):
        w0 = cw_ref[0, part, pl.ds(r0, SUBLANES), :]
        w1 = cw_ref[1, part, pl.ds(r0, SUBLANES), :]
        w2 = cw_ref[2, part, pl.ds(r0, SUBLANES), :]
        bb = cb_ref[part, pl.ds(r0, SUBLANES), :]
        hm = jnp.where(lane == 0, 0.0, pltpu.roll(h, 1, axis=1))
        hp = jnp.where(lane == seq - 1, 0.0, pltpu.roll(h, seq - 1, axis=1))
        return w0 * hm + w1 * h + w2 * hp + bb

    def prep(i, _):
        b = i // nrc
        r0 = pl.multiple_of((i % nrc) * SUBLANES, SUBLANES)
        v_s[b, pl.ds(r0, SUBLANES), :] = conv3(v_ref[b, pl.ds(r0, SUBLANES), :], 0, r0)
        return 0
    lax.fori_loop(0, nb * nrc, prep, 0)

    rows_per_k1 = npairs * cb
    total_rows = n1 * rows_per_k1
    chunk = 512 if total_rows % 512 == 0 and (512 % rows_per_k1 == 0 or rows_per_k1 % 512 == 0) else rows_per_k1

    for o in range(2):
        src = v_s if o == 0 else z_s
        dst = z_s if o == 0 else o_ref
        xg_ref = x1_ref if o == 0 else x2_ref

        def fwd(i, _, src=src):
            p = i // nrc
            r0 = pl.multiple_of((i % nrc) * SUBLANES, SUBLANES)
            blocks = [(src[2 * p, pl.ds(r0, SUBLANES), j * LANES:(j + 1) * LANES],
                       src[2 * p + 1, pl.ds(r0, SUBLANES), j * LANES:(j + 1) * LANES]) if j < nin else None
                      for j in range(n1)]
            zs = _fwd_lane_fft(blocks, tw_ref, n1)
            base = pl.multiple_of(p * cb + r0, SUBLANES)
            for k1 in range(n1):
                s_s[pl.ds(k1 * rows_per_k1 + base, SUBLANES), 0:LANES] = zs[k1][0]
                s_s[pl.ds(k1 * rows_per_k1 + base, SUBLANES), LANES:2 * LANES] = zs[k1][1]
            return 0
        lax.fori_loop(0, npairs * nrc, fwd, 0)

        for c in range(total_rows // chunk):
            rows = slice(c * chunk, (c + 1) * chunk)
            z = _dft_mm(s_s[rows, :], fh_ref, fl_ref)
            if chunk >= rows_per_k1:
                kparts = []
                for k1 in range(c * chunk // rows_per_k1, (c + 1) * chunk // rows_per_k1):
                    kparts += [kf_ref[o, k1]] * npairs
                kk = jnp.concatenate(kparts, axis=0) if len(kparts) > 1 else kparts[0]
            else:
                k1 = c * chunk // rows_per_k1
                kk = jnp.concatenate([kf_ref[o, k1]] * (chunk // cb), axis=0)
            zr, zi = z[:, :LANES], z[:, LANES:]
            kr, ki = kk[:, :LANES], kk[:, LANES:]
            w = jnp.concatenate([zr * kr - zi * ki, zr * ki + zi * kr], axis=1)
            s_s[rows, :] = _dft_mm(w, ih_ref, il_ref)

        def inv(i, _, src=src, dst=dst, xg_ref=xg_ref, o=o):
            p = i // nrc
            r0 = pl.multiple_of((i % nrc) * SUBLANES, SUBLANES)
            base = pl.multiple_of(p * cb + r0, SUBLANES)
            blocks = [(s_s[pl.ds(k1 * rows_per_k1 + base, SUBLANES), 0:LANES],
                       s_s[pl.ds(k1 * rows_per_k1 + base, SUBLANES), LANES:2 * LANES]) for k1 in range(n1)]
            ys = _inv_lane_fft(blocks, tw_ref, n1)
            bias = bias_ref[o, pl.ds(r0, SUBLANES), :]
            for q in range(2):
                xg = conv3(xg_ref[2 * p + q, pl.ds(r0, SUBLANES), :], 1 + o, r0)
                for j in range(nin):
                    u = src[2 * p + q, pl.ds(r0, SUBLANES), j * LANES:(j + 1) * LANES]
                    y = ys[j][q]
                    dst[2 * p + q, pl.ds(r0, SUBLANES), j * LANES:(j + 1) * LANES] = (
                        xg[:, j * LANES:(j + 1) * LANES] * (y + u * bias))
            return 0
        lax.fori_loop(0, npairs * nrc, inv, 0)


def _hyena(hyt, cw, cbias, bias, kf, tables):
    nb, _, seq = hyt.shape
    n1 = 2 * seq // DFT2
    cb = 32
    nblk = D_HY // cb
    fh, fl, ih, il, tw = tables
    kern = functools.partial(_hyena_kernel, seq=seq, n1=n1, cb=cb, nb=nb)
    full = lambda *shape: pl.BlockSpec(shape, lambda i: (0,) * len(shape))
    rows = n1 * (nb // 2) * cb
    return pl.pallas_call(
        kern,
        grid=(nblk,),
        in_specs=[pl.BlockSpec((nb, cb, seq), lambda i: (0, i, 0)),
                  pl.BlockSpec((nb, cb, seq), lambda i: (0, nblk + i, 0)),
                  pl.BlockSpec((nb, cb, seq), lambda i: (0, 2 * nblk + i, 0)),
                  pl.BlockSpec((3, 3, cb, 1), lambda i: (0, 0, i, 0)),
                  pl.BlockSpec((3, cb, 1), lambda i: (0, i, 0)),
                  pl.BlockSpec((2, cb, 1), lambda i: (0, i, 0)),
                  pl.BlockSpec((2, n1, cb, 2 * DFT2), lambda i: (0, 0, i, 0)),
                  full(2 * DFT2, 2 * DFT2), full(2 * DFT2, 2 * DFT2),
                  full(2 * DFT2, 2 * DFT2), full(2 * DFT2, 2 * DFT2),
                  full(n1 * SUBLANES, 2 * DFT2)],
        out_specs=pl.BlockSpec((nb, cb, seq), lambda i: (0, i, 0)),
        out_shape=jax.ShapeDtypeStruct((nb, D_HY, seq), F32),
        scratch_shapes=[pltpu.VMEM((nb, cb, seq), F32), pltpu.VMEM((nb, cb, seq), F32),
                        pltpu.VMEM((rows, 2 * DFT2), F32)],
        compiler_params=_cparams(("parallel",)),
        name="hyena",
    )(hyt, hyt, hyt, cw, cbias, bias, kf, fh, fl, ih, il, tw)


def _out_proj_kernel(x_ref, or_ref, oht_ref, mod_ref, gr_ref, gh_ref, wr_ref, wh_ref, g2_ref, x1_ref, xn_ref):
    orn = (_rms(or_ref[0]) * gr_ref[...]).astype(BF16)
    oh = oht_ref[0]
    ohn = oh * lax.rsqrt(jnp.mean(oh * oh, axis=0, keepdims=True) + EPS) * gh_ref[...]
    o = jnp.dot(orn, wr_ref[...], preferred_element_type=F32)
    o = o + lax.dot_general(ohn.astype(BF16), wh_ref[...], (((0,), (0,)), ((), ())), preferred_element_type=F32)
    x1 = x_ref[0] + mod_ref[0, 2:3, :] * o
    x1_ref[0] = x1
    xn = _rms(x1) * g2_ref[...]
    xn_ref[0] = (xn * (1.0 + mod_ref[0, 4:5, :]) + mod_ref[0, 3:4, :]).astype(BF16)


def _out_proj(x, o_r, oht, mod, gr, gh, wr, wh, g2, tm):
    b, l, _ = x.shape
    mod_map = (lambda i, j: (i, 0, 0)) if mod.shape[0] > 1 else (lambda i, j: (0, 0, 0))
    return pl.pallas_call(
        _out_proj_kernel,
        grid=(b, l // tm),
        in_specs=[pl.BlockSpec((1, tm, D_MODEL), lambda i, j: (i, j, 0)),
                  pl.BlockSpec((1, tm, D_RNN), lambda i, j: (i, j, 0)),
                  pl.BlockSpec((1, D_HY, tm), lambda i, j: (i, 0, j)),
                  pl.BlockSpec((1, N_MOD, D_MODEL), mod_map),
                  pl.BlockSpec((1, D_RNN), lambda i, j: (0, 0)),
                  pl.BlockSpec((D_HY, 1), lambda i, j: (0, 0)),
                  pl.BlockSpec((D_RNN, D_MODEL), lambda i, j: (0, 0)),
                  pl.BlockSpec((D_HY, D_MODEL), lambda i, j: (0, 0)),
                  pl.BlockSpec((1, D_MODEL), lambda i, j: (0, 0))],
        out_specs=[pl.BlockSpec((1, tm, D_MODEL), lambda i, j: (i, j, 0)),
                   pl.BlockSpec((1, tm, D_MODEL), lambda i, j: (i, j, 0))],
        out_shape=[jax.ShapeDtypeStruct((b, l, D_MODEL), F32),
                   jax.ShapeDtypeStruct((b, l, D_MODEL), BF16)],
        compiler_params=_cparams(("parallel", "parallel")),
        name="out_proj",
    )(x, o_r, oht, mod, gr, gh, wr, wh, g2)


def _mm_kernel(a_ref, w_ref, o_ref):
    o_ref[...] = jnp.dot(a_ref[...], w_ref[...], preferred_element_type=F32)


def _matmul(a, w, tm, tn):
    m, k = a.shape
    n = w.shape[1]
    return pl.pallas_call(
        _mm_kernel,
        grid=(n // tn, m // tm),
        in_specs=[pl.BlockSpec((tm, k), lambda j, i: (i, 0)),
                  pl.BlockSpec((k, tn), lambda j, i: (0, j))],
        out_specs=pl.BlockSpec((tm, tn), lambda j, i: (i, j)),
        out_shape=jax.ShapeDtypeStruct((m, n), F32),
        compiler_params=_cparams(("parallel", "parallel")),
        name="ffn_up",
    )(a, w)


FF_PAD = SUBLANES


def _ffn_down_kernel(x_ref, a_ref, g_ref, gp_ref, gn_ref, mod_ref, cw_ref, cb_ref, wd_ref, gf_ref, o_ref, ext,
                     *, tm, halo, on_grid, tiles_per_seq, final_norm):
    j = pl.program_id(1)
    ext[0:FF_PAD, :] = jnp.zeros((FF_PAD, D_FF), F32)
    ext[FF_PAD + 2 * halo + tm:2 * FF_PAD + 2 * halo + tm, :] = jnp.zeros((FF_PAD, D_FF), F32)
    if halo:
        ext[FF_PAD:FF_PAD + halo, :] = jnp.where(j > 0, gp_ref[0], 0.0)
        ext[FF_PAD + halo + tm:FF_PAD + 2 * halo + tm, :] = jnp.where(j < tiles_per_seq - 1, gn_ref[0], 0.0)
    ext[FF_PAD + halo:FF_PAD + halo + tm, :] = g_ref[0]
    base = FF_PAD + halo
    if on_grid:
        col = lax.broadcasted_iota(jnp.int32, (tm, 1), 0) % GRID_W
        acc = cb_ref[...]
        for dc in (-1, 0, 1):
            part = None
            for dr in (-1, 0, 1):
                term = cw_ref[dr + 1, dc + 1:dc + 2, :] * ext[pl.ds(base + dr * GRID_W + dc, tm), :]
                part = term if part is None else part + term
            if dc == -1:
                part = jnp.where(col == 0, 0.0, part)
            elif dc == 1:
                part = jnp.where(col == GRID_W - 1, 0.0, part)
            acc = acc + part
    else:
        acc = cb_ref[...]
        for dc in (-1, 0, 1):
            acc = acc + cw_ref[1, dc + 1:dc + 2, :] * ext[pl.ds(base + dc, tm), :]
    hcur = (jax.nn.gelu(acc) * a_ref[0]).astype(BF16)
    y = jnp.dot(hcur, wd_ref[...], preferred_element_type=F32)
    x2 = x_ref[0] + mod_ref[0, 5:6, :] * y
    if final_norm:
        x2 = _rms(x2) * gf_ref[...]
    o_ref[0] = x2


def _ffn_down(x1, ag, mod, cw, cb, wd, gf, on_grid, final_norm):
    b, l, _ = x1.shape
    if on_grid:
        tm, halo = 256, GRID_W
    else:
        tm, halo = l, 0
    tps = l // tm
    hb = halo if halo else SUBLANES
    nhb = tm // hb
    mod_map = (lambda i, j: (i, 0, 0)) if mod.shape[0] > 1 else (lambda i, j: (0, 0, 0))
    nff = D_FF // D_FF
    kern = functools.partial(_ffn_down_kernel, tm=tm, halo=halo, on_grid=on_grid, tiles_per_seq=tps,
                             final_norm=final_norm)
    last_hb = l // hb - 1
    return pl.pallas_call(
        kern,
        grid=(b, tps),
        in_specs=[pl.BlockSpec((1, tm, D_MODEL), lambda i, j: (i, j, 0)),
                  pl.BlockSpec((1, tm, D_FF), lambda i, j: (i, j, 0)),
                  pl.BlockSpec((1, tm, D_FF), lambda i, j: (i, j, nff)),
                  pl.BlockSpec((1, hb, D_FF), lambda i, j: (i, jnp.maximum(j * nhb - 1, 0), nff)),
                  pl.BlockSpec((1, hb, D_FF), lambda i, j: (i, jnp.minimum((j + 1) * nhb, last_hb), nff)),
                  pl.BlockSpec((1, N_MOD, D_MODEL), mod_map),
                  pl.BlockSpec((3, 3, D_FF), lambda i, j: (0, 0, 0)),
                  pl.BlockSpec((1, D_FF), lambda i, j: (0, 0)),
                  pl.BlockSpec((D_FF, D_MODEL), lambda i, j: (0, 0)),
                  pl.BlockSpec((1, D_MODEL), lambda i, j: (0, 0))],
        out_specs=pl.BlockSpec((1, tm, D_MODEL), lambda i, j: (i, j, 0)),
        out_shape=jax.ShapeDtypeStruct((b, l, D_MODEL), F32),
        scratch_shapes=[pltpu.VMEM((tm + 2 * halo + 2 * FF_PAD, D_FF), F32)],
        compiler_params=_cparams(("parallel", "parallel")),
        name="ffn_down",
    )(x1, ag, ag, ag, ag, mod, cw, cb, wd, gf)


def _trunk_layer(x, mod, h0, p, kf, tables, on_grid, final_norm, g_final):
    b, l, _ = x.shape
    tm = min(512, l)
    xy, hyt = _in_proj(x, mod, p['g_norm1'], p['wxy'], p['whyt'], tm)
    o_r, states = _rglru(xy, p['rg_conv_w'], p['rg_conv_b'], p['wg'], p['bg'], p['ap'], h0)
    oht = _hyena(hyt, p['hy_cw'], p['hy_cb'], p['hy_bias'], kf, tables)
    x1, xn2 = _out_proj(x, o_r, oht, mod, p['g_rnn_out'], p['g_hy_out'], p['w_out_r'], p['w_out_h'], p['g_norm2'], tm)
    ag = _matmul(xn2.reshape(b * l, D_MODEL), p['w_up'], 512, 2 * D_FF // 4).reshape(b, l, 2 * D_FF)
    x2 = _ffn_down(x1, ag, mod, p['ffn_conv_w'], p['ffn_conv_b'], p['w_down'], g_final, on_grid, final_norm)
    return x2, states


def kernel(x_prompt, x_sample, state_rglru, c, c_ctx, w_ada, b_ada, g_norm1, g_norm2, w_in, rg_conv_w, rg_conv_b, rg_gate_w, rg_gate_b, rg_a, hy_conv_w, hy_conv_b, hf_w1, hf_b1, hf_w2, hf_b2, hf_w3, hf_freq, hy_bias, g_rnn_out, g_hy_out, w_out, w_up, ffn_conv_w, ffn_conv_b, w_down, g_final):
    depth = w_in.shape[0]
    nb_ctx, l_ctx, _ = x_prompt.shape
    nb_lat, l_lat, _ = x_sample.shape

    cc = jnp.zeros((SUBLANES, D_MODEL), F32).at[0].set(c_ctx).at[1:1 + nb_lat].set(c)
    mods = _modulation(cc, w_ada, b_ada).reshape(depth, SUBLANES, N_MOD, D_MODEL)

    tab_ctx = _dft_tables(2 * l_ctx // DFT2)
    tab_lat = _dft_tables(2 * l_lat // DFT2)
    gf = g_final.reshape(1, D_MODEL)
    zero_h = jnp.zeros((nb_ctx, 2, D_RNN), F32)

    xp, xs = x_prompt, x_sample
    new_states = []
    for l in range(depth):
        wg, bg, ap = _gate_weights(rg_gate_w[l], rg_gate_b[l], rg_a[l])
        p = {
            'g_norm1': g_norm1[l].reshape(1, D_MODEL), 'g_norm2': g_norm2[l].reshape(1, D_MODEL),
            'wxy': w_in[l, :, :2 * D_RNN].astype(BF16), 'whyt': w_in[l, :, 2 * D_RNN:].T.astype(BF16),
            'rg_conv_w': rg_conv_w[l], 'rg_conv_b': rg_conv_b[l].reshape(1, D_RNN),
            'wg': wg, 'bg': bg, 'ap': ap,
            'hy_cw': hy_conv_w[l].reshape(3, 3, D_HY, 1), 'hy_cb': hy_conv_b[l].reshape(3, D_HY, 1),
            'hy_bias': hy_bias[l].reshape(2, D_HY, 1),
            'g_rnn_out': g_rnn_out[l].reshape(1, D_RNN), 'g_hy_out': g_hy_out[l].reshape(D_HY, 1),
            'w_out_r': w_out[l, :D_RNN].astype(BF16), 'w_out_h': w_out[l, D_RNN:].astype(BF16),
            'w_up': w_up[l].astype(BF16), 'ffn_conv_w': ffn_conv_w[l], 'ffn_conv_b': ffn_conv_b[l].reshape(1, D_FF),
            'w_down': w_down[l].astype(BF16),
        }
        fargs = (hf_w1[l], hf_b1[l], hf_w2[l], hf_b2[l], hf_w3[l], hf_freq[l])
        kf_ctx = _hyena_filters(l_ctx, *fargs, tab_ctx)
        kf_lat = _hyena_filters(l_lat, *fargs, tab_lat)
        final = l == depth - 1
        xp, st = _trunk_layer(xp, mods[l, 0:1], zero_h, p, kf_ctx, tab_ctx, False, final, gf)
        new_states.append(st)
        xs, _ = _trunk_layer(xs, mods[l, 1:1 + nb_lat], state_rglru[:, l], p, kf_lat, tab_lat, True, final, gf)
    return (xp, xs, jnp.stack(new_states, axis=1))
```

```python
import functools
import math

import jax
import jax.numpy as jnp
import ml_dtypes
import numpy as np
from jax import lax
from jax.experimental import pallas as pl
from jax.experimental.pallas import tpu as pltpu

F32 = jnp.float32
BF16 = jnp.bfloat16

D_MODEL = 1024
D_RNN = 512
D_HY = 512
N_HEADS = 8
HEAD_DIM = D_RNN // N_HEADS
RG_C = 8.0
GRID_W = 64
HY_BANDS = 16
HY_FH = 64
D_FF = 2816
N_MOD = 6
EPS = 1e-6

SUBLANES = 8
LANES = 128
VMEM_LIMIT = 56 * 1024 * 1024

DFT2 = LANES
FFT_PASSES = 2


def _cparams(sem):
    return pltpu.CompilerParams(dimension_semantics=sem, vmem_limit_bytes=VMEM_LIMIT)


def _rms(x):
    return x * lax.rsqrt(jnp.mean(x * x, axis=-1, keepdims=True) + EPS)


def _mod_kernel(c_ref, w_ref, b_ref, o_ref):
    c = c_ref[...]
    s = c * jax.nn.sigmoid(c)
    o_ref[0] = jnp.dot(s, w_ref[0], precision=lax.Precision.HIGHEST, preferred_element_type=F32) + b_ref[0]


def _modulation(cc, w_ada, b_ada):
    depth, _, n = w_ada.shape
    tn = 1536
    return pl.pallas_call(
        _mod_kernel,
        grid=(depth, n // tn),
        in_specs=[pl.BlockSpec((SUBLANES, D_MODEL), lambda l, j: (0, 0)),
                  pl.BlockSpec((1, D_MODEL, tn), lambda l, j: (l, 0, j)),
                  pl.BlockSpec((1, 1, tn), lambda l, j: (l, 0, j))],
        out_specs=pl.BlockSpec((1, SUBLANES, tn), lambda l, j: (l, 0, j)),
        out_shape=jax.ShapeDtypeStruct((depth, SUBLANES, n), F32),
        compiler_params=_cparams(("parallel", "parallel")),
        name="adaln_mod",
    )(cc, w_ada, b_ada.reshape(depth, 1, n))


def _in_proj_kernel(x_ref, mod_ref, g_ref, wxy_ref, whyt_ref, xy_ref, hyt_ref):
    x = x_ref[0]
    xn = _rms(x) * g_ref[...]
    xn = (xn * (1.0 + mod_ref[0, 1:2, :]) + mod_ref[0, 0:1, :]).astype(BF16)
    xy_ref[0] = jnp.dot(xn, wxy_ref[...], preferred_element_type=F32)
    hyt_ref[0] = lax.dot_general(whyt_ref[...], xn, (((1,), (1,)), ((), ())), preferred_element_type=F32)


def _in_proj(x, mod, g, wxy, whyt, tm):
    b, l, _ = x.shape
    per_batch_mod = mod.shape[0] > 1
    mod_map = (lambda i, j: (i, 0, 0)) if per_batch_mod else (lambda i, j: (0, 0, 0))
    return pl.pallas_call(
        _in_proj_kernel,
        grid=(b, l // tm),
        in_specs=[pl.BlockSpec((1, tm, D_MODEL), lambda i, j: (i, j, 0)),
                  pl.BlockSpec((1, N_MOD, D_MODEL), mod_map),
                  pl.BlockSpec((1, D_MODEL), lambda i, j: (0, 0)),
                  pl.BlockSpec((D_MODEL, 2 * D_RNN), lambda i, j: (0, 0)),
                  pl.BlockSpec((3 * D_HY, D_MODEL), lambda i, j: (0, 0))],
        out_specs=[pl.BlockSpec((1, tm, 2 * D_RNN), lambda i, j: (i, j, 0)),
                   pl.BlockSpec((1, 3 * D_HY, tm), lambda i, j: (i, 0, j))],
        out_shape=[jax.ShapeDtypeStruct((b, l, 2 * D_RNN), F32),
                   jax.ShapeDtypeStruct((b, 3 * D_HY, l), F32)],
        compiler_params=_cparams(("parallel", "parallel")),
        name="in_proj",
    )(x, mod, g, wxy, whyt)


RG_HALF = D_RNN // 2
RG_TILES = RG_HALF // LANES


def _rglru_kernel(xr_ref, yr_ref, cw_ref, cb_ref, wg_ref, bg_ref, ap_ref, h0_ref, o_ref, st_ref,
                  ext, xc_s, hf, hb, a_s, b_s, *, seq, tc):
    nchunks = seq // tc
    nblk = tc // SUBLANES
    ext[0:SUBLANES, :] = jnp.zeros((SUBLANES, RG_HALF), F32)
    ext[SUBLANES:SUBLANES + seq, :] = xr_ref[0]
    ext[SUBLANES + seq:2 * SUBLANES + seq, :] = jnp.zeros((SUBLANES, RG_HALF), F32)
    for c in range(nchunks):
        xc = cb_ref[...] + cw_ref[0:1, :] * ext[pl.ds(c * tc + SUBLANES - 2, tc), :]
        for k in range(1, 4):
            xc = xc + cw_ref[k:k + 1, :] * ext[pl.ds(c * tc + SUBLANES - 2 + k, tc), :]
        xc_s[pl.ds(c * tc, tc), :] = xc

    row = lax.broadcasted_iota(jnp.int32, (SUBLANES, LANES), 0)
    neg_c_sp = [-RG_C * jax.nn.softplus(-ap_ref[d, 0]) for d in range(2)]

    def gates(c0, d):
        xc = xc_s[pl.ds(c0, tc), :]
        g = jnp.dot(xc.astype(BF16), wg_ref[d, 0], preferred_element_type=F32) + bg_ref[d, 0]
        r = 0.5 * jnp.tanh(0.5 * g[:, :RG_HALF]) + 0.5
        i = 0.5 * jnp.tanh(0.5 * g[:, RG_HALF:]) + 0.5
        log_a = neg_c_sp[d] * r
        a = jnp.exp(log_a)
        mult = jnp.sqrt(jnp.tanh(-log_a) * (a * a + 1.0))
        a_s[d] = a
        b_s[d] = xc * i * mult

    def local_scan(a, b, reverse):
        for s in (1, 2, 4):
            if reverse:
                keep = row < SUBLANES - s
                shift = SUBLANES - s
            else:
                keep = row >= s
                shift = s
            a_sh = jnp.where(keep, pltpu.roll(a, shift, axis=0), 1.0)
            b_sh = jnp.where(keep, pltpu.roll(b, shift, axis=0), 0.0)
            b = a * b_sh + b
            a = a * a_sh
        return a, b

    def block_body(cf0, cb0):
        def body(j, carry):
            rf = pl.multiple_of(j * SUBLANES, SUBLANES)
            rb = pl.multiple_of(tc - SUBLANES - j * SUBLANES, SUBLANES)
            out = []
            for t in range(RG_TILES):
                lanes = slice(t * LANES, (t + 1) * LANES)
                pa, pb = local_scan(a_s[0, pl.ds(rf, SUBLANES), lanes], b_s[0, pl.ds(rf, SUBLANES), lanes], False)
                h = pa * carry[2 * t] + pb
                hf[pl.ds(cf0 + rf, SUBLANES), lanes] = h
                out.append(jnp.broadcast_to(h[SUBLANES - 1:SUBLANES, :], (SUBLANES, LANES)))
                pa, pb = local_scan(a_s[1, pl.ds(rb, SUBLANES), lanes], b_s[1, pl.ds(rb, SUBLANES), lanes], True)
                h = pa * carry[2 * t + 1] + pb
                hb[pl.ds(cb0 + rb, SUBLANES), lanes] = h
                out.append(jnp.broadcast_to(h[0:1, :], (SUBLANES, LANES)))
            return tuple(out)
        return body

    carry = []
    for t in range(RG_TILES):
        carry.append(jnp.broadcast_to(h0_ref[0, 0:1, t * LANES:(t + 1) * LANES], (SUBLANES, LANES)))
        carry.append(jnp.broadcast_to(h0_ref[0, 1:2, t * LANES:(t + 1) * LANES], (SUBLANES, LANES)))
    carry = tuple(carry)
    for c in range(nchunks):
        cf0 = c * tc
        cb0 = (nchunks - 1 - c) * tc
        gates(cf0, 0)
        gates(cb0, 1)
        carry = lax.fori_loop(0, nblk, block_body(cf0, cb0), carry, unroll=2)
    for t in range(RG_TILES):
        st_ref[0, 0:1, t * LANES:(t + 1) * LANES] = carry[2 * t][0:1, :]
        st_ref[0, 1:2, t * LANES:(t + 1) * LANES] = carry[2 * t + 1][0:1, :]
    for c in range(nchunks):
        rows = pl.ds(c * tc, tc)
        o_ref[0, rows, :] = (hf[rows, :] + hb[rows, :]) * jax.nn.gelu(yr_ref[0, rows, :])


def _rglru(xy, cw, cb, wg, bg, ap, h0):
    b, seq, _ = xy.shape
    tc = min(512, seq)
    nh = D_RNN // RG_HALF
    kern = functools.partial(_rglru_kernel, seq=seq, tc=tc)
    return pl.pallas_call(
        kern,
        grid=(b, nh),
        in_specs=[pl.BlockSpec((1, seq, RG_HALF), lambda i, h: (i, 0, h)),
                  pl.BlockSpec((1, seq, RG_HALF), lambda i, h: (i, 0, nh + h)),
                  pl.BlockSpec((4, RG_HALF), lambda i, h: (0, h)),
                  pl.BlockSpec((1, RG_HALF), lambda i, h: (0, h)),
                  pl.BlockSpec((2, 1, RG_HALF, 2 * RG_HALF), lambda i, h: (0, h, 0, 0)),
                  pl.BlockSpec((2, 1, 1, 2 * RG_HALF), lambda i, h: (0, h, 0, 0)),
                  pl.BlockSpec((2, 1, 1, RG_HALF), lambda i, h: (0, h, 0, 0)),
                  pl.BlockSpec((1, 2, RG_HALF), lambda i, h: (i, 0, h))],
        out_specs=[pl.BlockSpec((1, seq, RG_HALF), lambda i, h: (i, 0, h)),
                   pl.BlockSpec((1, 2, RG_HALF), lambda i, h: (i, 0, h))],
        out_shape=[jax.ShapeDtypeStruct((b, seq, D_RNN), F32),
                   jax.ShapeDtypeStruct((b, 2, D_RNN), F32)],
        scratch_shapes=[pltpu.VMEM((seq + 2 * SUBLANES, RG_HALF), F32),
                        pltpu.VMEM((seq, RG_HALF), F32),
                        pltpu.VMEM((seq, RG_HALF), F32),
                        pltpu.VMEM((seq, RG_HALF), F32),
                        pltpu.VMEM((2, tc, RG_HALF), F32),
                        pltpu.VMEM((2, tc, RG_HALF), F32)],
        compiler_params=_cparams(("parallel", "parallel")),
        name="rglru",
    )(xy, xy, cw, cb, wg, bg, ap, h0)


def _gate_weights(gate_w, gate_b, a_param):
    nh = D_RNN // RG_HALF
    hp = N_HEADS // nh
    w = gate_w.reshape(2, 2, nh, hp, HEAD_DIM, HEAD_DIM)
    eye = jnp.eye(hp, dtype=F32)
    dense = jnp.einsum('dghpio,pq->dhpigqo', w, eye)
    dense = dense.reshape(2, nh, RG_HALF, 2 * RG_HALF).astype(BF16)
    bias = gate_b.reshape(2, 2, nh, RG_HALF).transpose(0, 2, 1, 3).reshape(2, nh, 1, 2 * RG_HALF)
    ap = a_param.reshape(2, nh, 1, RG_HALF)
    return dense, bias, ap


def _add(a, b):
    if a is None:
        return b
    if b is None:
        return a
    return a + b


def _sub(a, b):
    if b is None:
        return a
    if a is None:
        return -b
    return a - b


def _scale(a, s):
    if a is None or s == 0.0:
        return None
    if s == 1.0:
        return a
    if s == -1.0:
        return -a
    return a * s


def _cmul_const(z, wr, wi):
    if z is None:
        return None
    re, im = z
    if abs(wr) < 1e-15:
        wr = 0.0
    if abs(wi) < 1e-15:
        wi = 0.0
    return (_sub(_scale(re, wr), _scale(im, wi)), _add(_scale(re, wi), _scale(im, wr)))


def _cadd(a, b):
    if a is None:
        return b
    if b is None:
        return a
    return (_add(a[0], b[0]), _add(a[1], b[1]))


def _csub(a, b):
    if b is None:
        return a
    if a is None:
        return (_sub(None, b[0]), _sub(None, b[1]))
    return (_sub(a[0], b[0]), _sub(a[1], b[1]))


def _cfft(xs, sign):
    n = len(xs)
    if n == 1:
        return list(xs)
    even = _cfft(xs[0::2], sign)
    odd = _cfft(xs[1::2], sign)
    out = [None] * n
    for k in range(n // 2):
        ang = sign * 2.0 * math.pi * k / n
        t = _cmul_const(odd[k], math.cos(ang), math.sin(ang))
        out[k] = _cadd(even[k], t)
        out[k + n // 2] = _csub(even[k], t)
    return out


def _dft_tables(n1):
    n = n1 * DFT2
    k = np.arange(DFT2, dtype=np.float64)
    ang = 2.0 * np.pi * np.outer(k, k) / DFT2
    c, s = np.cos(ang), np.sin(ang)
    fwd = np.block([[c, -s], [s, c]])
    inv = np.block([[c, s], [-s, c]]) / n

    def split(m):
        hi = m.astype(np.float32).astype(ml_dtypes.bfloat16)
        lo = (m - hi.astype(np.float64)).astype(ml_dtypes.bfloat16)
        return jnp.asarray(hi), jnp.asarray(lo)

    tw_ang = 2.0 * np.pi * np.outer(np.arange(n1, dtype=np.float64), k) / n
    tw = np.concatenate([np.cos(tw_ang), -np.sin(tw_ang)], axis=1)
    tw = np.repeat(tw, SUBLANES, axis=0).astype(np.float32)
    return split(fwd) + split(inv) + (jnp.asarray(tw),)


def _dft_mm(x, hi_ref, lo_ref):
    xh = x.astype(BF16)
    acc = jnp.dot(xh, hi_ref[...], preferred_element_type=F32)
    if FFT_PASSES >= 2:
        acc = acc + jnp.dot(xh, lo_ref[...], preferred_element_type=F32)
    if FFT_PASSES >= 3:
        xl = (x - xh.astype(F32)).astype(BF16)
        acc = acc + jnp.dot(xl, hi_ref[...], preferred_element_type=F32)
    return acc


def _fwd_lane_fft(blocks, tw_ref, n1):
    xs = _cfft(blocks, -1)
    out = []
    for k1 in range(n1):
        z = xs[k1]
        if k1 > 0 and z is not None:
            twr = tw_ref[k1 * SUBLANES:(k1 + 1) * SUBLANES, 0:LANES]
            twi = tw_ref[k1 * SUBLANES:(k1 + 1) * SUBLANES, LANES:2 * LANES]
            re, im = z
            if im is None:
                z = (re * twr, re * twi)
            else:
                z = (re * twr - im * twi, re * twi + im * twr)
        out.append(z)
    return out


def _inv_lane_fft(blocks, tw_ref, n1):
    gs = []
    for k1 in range(n1):
        re, im = blocks[k1]
        if k1 > 0:
            twr = tw_ref[k1 * SUBLANES:(k1 + 1) * SUBLANES, 0:LANES]
            twi = tw_ref[k1 * SUBLANES:(k1 + 1) * SUBLANES, LANES:2 * LANES]
            re, im = re * twr + im * twi, im * twr - re * twi
        gs.append((re, im))
    return _cfft(gs, +1)


def _filter_kernel(feat_ref, w1_ref, b1_ref, w2_ref, b2_ref, fr_ref, w3_ref, fh_ref, fl_ref, tw_ref, kf_ref,
                   h2_s, k_s, s_s, *, seq, n1, cb):
    n = 2 * seq
    hi = lax.Precision.HIGHEST

    @pl.when(pl.program_id(0) == 0)
    def _():
        h = jnp.dot(w1_ref[...], feat_ref[...], precision=hi, preferred_element_type=F32) + b1_ref[...]
        h = jnp.sin(fr_ref[:, 0:1] * h)
        h = jnp.dot(w2_ref[...], h, precision=hi, preferred_element_type=F32) + b2_ref[...]
        h2_s[...] = jnp.sin(fr_ref[:, 1:2] * h)

    lane = lax.broadcasted_iota(jnp.int32, (cb, n), 1)
    pos = jnp.where(lane < seq, lane, n - lane).astype(F32)
    t = pos / float(max(seq - 1, 1))
    d_idx = (lax.broadcasted_iota(jnp.int32, (cb, n), 0) + pl.program_id(0) * cb).astype(F32)
    min_decay = math.log(1e-2) / 1.5
    max_decay = math.log(1e-2) / 0.3
    delta = jnp.abs(min_decay + d_idx * ((max_decay - min_decay) / (D_HY - 1)))
    decay = jnp.exp(-t * delta)
    for o in range(2):
        hf = jnp.dot(w3_ref[2 * o], h2_s[...], precision=hi, preferred_element_type=F32)
        hb = jnp.dot(w3_ref[2 * o + 1], h2_s[...], precision=hi, preferred_element_type=F32)
        kk = jnp.where(lane < seq, hf, jnp.where(lane == seq, 0.0, hb)) * decay
        k_s[o] = kk

    nrc = cb // SUBLANES
    for o in range(2):
        def rows(rc, _, o=o):
            r0 = pl.multiple_of(rc * SUBLANES, SUBLANES)
            blocks = [(k_s[o, pl.ds(r0, SUBLANES), j * LANES:(j + 1) * LANES], None) for j in range(n1)]
            zs = _fwd_lane_fft(blocks, tw_ref, n1)
            for k1 in range(n1):
                re, im = zs[k1]
                s_s[pl.ds(k1 * cb + r0, SUBLANES), 0:LANES] = re
                s_s[pl.ds(k1 * cb + r0, SUBLANES), LANES:2 * LANES] = im if im is not None else jnp.zeros_like(re)
            return 0
        lax.fori_loop(0, nrc, rows, 0)
        x = s_s[...]
        xh = x.astype(BF16)
        xl = (x - xh.astype(F32)).astype(BF16)
        z = (jnp.dot(xh, fh_ref[...], preferred_element_type=F32) + jnp.dot(xh, fl_ref[...], preferred_element_type=F32)
             + jnp.dot(xl, fh_ref[...], preferred_element_type=F32))
        for k1 in range(n1):
            kf_ref[o, k1] = z[k1 * cb:(k1 + 1) * cb, :]


def _hyena_filters(seq, w1, b1, w2, b2, w3, freq, tables):
    n1 = 2 * seq // DFT2
    n = 2 * seq
    cb = 32
    fh, fl, _, _, tw = tables
    pos = np.arange(n, dtype=np.float64)
    pos = np.where(pos < seq, pos, n - pos)
    tt = pos / max(seq - 1, 1)
    omega = 2.0 * math.pi * pos / seq
    bands = np.linspace(1e-4, HY_BANDS - 1, HY_BANDS)
    ang = omega[None, :] * bands[:, None]
    feats = np.concatenate([tt[None, :], np.cos(ang), np.sin(ang)], axis=0)
    nfeat = LANES
    feats = np.pad(feats, ((0, nfeat - feats.shape[0]), (0, 0))).astype(np.float32)
    w1 = jnp.pad(w1, ((0, nfeat - w1.shape[0]), (0, 0)))
    w3t = w3.T.reshape(4, D_HY, HY_FH)
    kern = functools.partial(_filter_kernel, seq=seq, n1=n1, cb=cb)
    full = lambda *shape: pl.BlockSpec(shape, lambda i: (0,) * len(shape))
    return pl.pallas_call(
        kern,
        grid=(D_HY // cb,),
        in_specs=[full(nfeat, n), full(HY_FH, nfeat), full(HY_FH, 1), full(HY_FH, HY_FH), full(HY_FH, 1),
                  full(HY_FH, 2),
                  pl.BlockSpec((4, cb, HY_FH), lambda i: (0, i, 0)),
                  full(2 * DFT2, 2 * DFT2), full(2 * DFT2, 2 * DFT2), full(n1 * SUBLANES, 2 * DFT2)],
        out_specs=pl.BlockSpec((2, n1, cb, 2 * DFT2), lambda i: (0, 0, i, 0)),
        out_shape=jax.ShapeDtypeStruct((2, n1, D_HY, 2 * DFT2), F32),
        scratch_shapes=[pltpu.VMEM((HY_FH, n), F32), pltpu.VMEM((2, cb, n), F32),
                        pltpu.VMEM((n1 * cb, 2 * DFT2), F32)],
        compiler_params=_cparams(("arbitrary",)),
        name="hyena_filters",
    )(jnp.asarray(feats), w1.T, b1.reshape(HY_FH, 1), w2.T, b2.reshape(HY_FH, 1), freq.T, w3t, fh, fl, tw)


def _hyena_kernel(v_ref, x1_ref, x2_ref, cw_ref, cb_ref, bias_ref, kf_ref, fh_ref, fl_ref, ih_ref, il_ref, tw_ref,
                  o_ref, v_s, z_s, s_s, *, seq, n1, cb, nb):
    npairs = nb // 2
    nin = n1 // 2
    nrc = cb // SUBLANES
    lane = lax.broadcasted_iota(jnp.int32, (SUBLANES, seq), 1)

    def conv3(h, part, r0):
        w0 = cw_ref[0, part, pl.ds(r0, SUBLANES), :]
        w1 = cw_ref[1, part, pl.ds(r0, SUBLANES), :]
        w2 = cw_ref[2, part, pl.ds(r0, SUBLANES), :]
        bb = cb_ref[part, pl.ds(r0, SUBLANES), :]
        hm = jnp.where(lane == 0, 0.0, pltpu.roll(h, 1, axis=1))
        hp = jnp.where(lane == seq - 1, 0.0, pltpu.roll(h, seq - 1, axis=1))
        return w0 * hm + w1 * h + w2 * hp + bb

    def prep(i, _):
        b = i // nrc
        r0 = pl.multiple_of((i % nrc) * SUBLANES, SUBLANES)
        v_s[b, pl.ds(r0, SUBLANES), :] = conv3(v_ref[b, pl.ds(r0, SUBLANES), :], 0, r0)
        return 0
    lax.fori_loop(0, nb * nrc, prep, 0)

    rows_per_k1 = npairs * cb
    total_rows = n1 * rows_per_k1
    chunk = 512 if total_rows % 512 == 0 and (512 % rows_per_k1 == 0 or rows_per_k1 % 512 == 0) else rows_per_k1

    for o in range(2):
        src = v_s if o == 0 else z_s
        dst = z_s if o == 0 else o_ref
        xg_ref = x1_ref if o == 0 else x2_ref

        def fwd(i, _, src=src):
            p = i // nrc
            r0 = pl.multiple_of((i % nrc) * SUBLANES, SUBLANES)
            blocks = [(src[2 * p, pl.ds(r0, SUBLANES), j * LANES:(j + 1) * LANES],
                       src[2 * p + 1, pl.ds(r0, SUBLANES), j * LANES:(j + 1) * LANES]) if j < nin else None
                      for j in range(n1)]
            zs = _fwd_lane_fft(blocks, tw_ref, n1)
            base = pl.multiple_of(p * cb + r0, SUBLANES)
            for k1 in range(n1):
                s_s[pl.ds(k1 * rows_per_k1 + base, SUBLANES), 0:LANES] = zs[k1][0]
                s_s[pl.ds(k1 * rows_per_k1 + base, SUBLANES), LANES:2 * LANES] = zs[k1][1]
            return 0
        lax.fori_loop(0, npairs * nrc, fwd, 0)

        for c in range(total_rows // chunk):
            rows = slice(c * chunk, (c + 1) * chunk)
            z = _dft_mm(s_s[rows, :], fh_ref, fl_ref)
            if chunk >= rows_per_k1:
                kparts = []
                for k1 in range(c * chunk // rows_per_k1, (c + 1) * chunk // rows_per_k1):
                    kparts += [kf_ref[o, k1]] * npairs
                kk = jnp.concatenate(kparts, axis=0) if len(kparts) > 1 else kparts[0]
            else:
                k1 = c * chunk // rows_per_k1
                kk = jnp.concatenate([kf_ref[o, k1]] * (chunk // cb), axis=0)
            zr, zi = z[:, :LANES], z[:, LANES:]
            kr, ki = kk[:, :LANES], kk[:, LANES:]
            w = jnp.concatenate([zr * kr - zi * ki, zr * ki + zi * kr], axis=1)
            s_s[rows, :] = _dft_mm(w, ih_ref, il_ref)

        def inv(i, _, src=src, dst=dst, xg_ref=xg_ref, o=o):
            p = i // nrc
            r0 = pl.multiple_of((i % nrc) * SUBLANES, SUBLANES)
            base = pl.multiple_of(p * cb + r0, SUBLANES)
            blocks = [(s_s[pl.ds(k1 * rows_per_k1 + base, SUBLANES), 0:LANES],
                       s_s[pl.ds(k1 * rows_per_k1 + base, SUBLANES), LANES:2 * LANES]) for k1 in range(n1)]
            ys = _inv_lane_fft(blocks, tw_ref, n1)
            bias = bias_ref[o, pl.ds(r0, SUBLANES), :]
            for q in range(2):
                xg = conv3(xg_ref[2 * p + q, pl.ds(r0, SUBLANES), :], 1 + o, r0)
                for j in range(nin):
                    u = src[2 * p + q, pl.ds(r0, SUBLANES), j * LANES:(j + 1) * LANES]
                    y = ys[j][q]
                    dst[2 * p + q, pl.ds(r0, SUBLANES), j * LANES:(j + 1) * LANES] = (
                        xg[:, j * LANES:(j + 1) * LANES] * (y + u * bias))
            return 0
        lax.fori_loop(0, npairs * nrc, inv, 0)


def _hyena(hyt, cw, cbias, bias, kf, tables):
    nb, _, seq = hyt.shape
    n1 = 2 * seq // DFT2
    cb = 32
    nblk = D_HY // cb
    fh, fl, ih, il, tw = tables
    kern = functools.partial(_hyena_kernel, seq=seq, n1=n1, cb=cb, nb=nb)
    full = lambda *shape: pl.BlockSpec(shape, lambda i: (0,) * len(shape))
    rows = n1 * (nb // 2) * cb
    return pl.pallas_call(
        kern,
        grid=(nblk,),
        in_specs=[pl.BlockSpec((nb, cb, seq), lambda i: (0, i, 0)),
                  pl.BlockSpec((nb, cb, seq), lambda i: (0, nblk + i, 0)),
                  pl.BlockSpec((nb, cb, seq), lambda i: (0, 2 * nblk + i, 0)),
                  pl.BlockSpec((3, 3, cb, 1), lambda i: (0, 0, i, 0)),
                  pl.BlockSpec((3, cb, 1), lambda i: (0, i, 0)),
                  pl.BlockSpec((2, cb, 1), lambda i: (0, i, 0)),
                  pl.BlockSpec((2, n1, cb, 2 * DFT2), lambda i: (0, 0, i, 0)),
                  full(2 * DFT2, 2 * DFT2), full(2 * DFT2, 2 * DFT2),
                  full(2 * DFT2, 2 * DFT2), full(2 * DFT2, 2 * DFT2),
                  full(n1 * SUBLANES, 2 * DFT2)],
        out_specs=pl.BlockSpec((nb, cb, seq), lambda i: (0, i, 0)),
        out_shape=jax.ShapeDtypeStruct((nb, D_HY, seq), F32),
        scratch_shapes=[pltpu.VMEM((nb, cb, seq), F32), pltpu.VMEM((nb, cb, seq), F32),
                        pltpu.VMEM((rows, 2 * DFT2), F32)],
        compiler_params=_cparams(("parallel",)),
        name="hyena",
    )(hyt, hyt, hyt, cw, cbias, bias, kf, fh, fl, ih, il, tw)


def _out_proj_kernel(x_ref, or_ref, oht_ref, mod_ref, gr_ref, gh_ref, wr_ref, wh_ref, g2_ref, x1_ref, xn_ref):
    orn = (_rms(or_ref[0]) * gr_ref[...]).astype(BF16)
    oh = oht_ref[0]
    ohn = oh * lax.rsqrt(jnp.mean(oh * oh, axis=0, keepdims=True) + EPS) * gh_ref[...]
    o = jnp.dot(orn, wr_ref[...], preferred_element_type=F32)
    o = o + lax.dot_general(ohn.astype(BF16), wh_ref[...], (((0,), (0,)), ((), ())), preferred_element_type=F32)
    x1 = x_ref[0] + mod_ref[0, 2:3, :] * o
    x1_ref[0] = x1
    xn = _rms(x1) * g2_ref[...]
    xn_ref[0] = (xn * (1.0 + mod_ref[0, 4:5, :]) + mod_ref[0, 3:4, :]).astype(BF16)


def _out_proj(x, o_r, oht, mod, gr, gh, wr, wh, g2, tm):
    b, l, _ = x.shape
    mod_map = (lambda i, j: (i, 0, 0)) if mod.shape[0] > 1 else (lambda i, j: (0, 0, 0))
    return pl.pallas_call(
        _out_proj_kernel,
        grid=(b, l // tm),
        in_specs=[pl.BlockSpec((1, tm, D_MODEL), lambda i, j: (i, j, 0)),
                  pl.BlockSpec((1, tm, D_RNN), lambda i, j: (i, j, 0)),
                  pl.BlockSpec((1, D_HY, tm), lambda i, j: (i, 0, j)),
                  pl.BlockSpec((1, N_MOD, D_MODEL), mod_map),
                  pl.BlockSpec((1, D_RNN), lambda i, j: (0, 0)),
                  pl.BlockSpec((D_HY, 1), lambda i, j: (0, 0)),
                  pl.BlockSpec((D_RNN, D_MODEL), lambda i, j: (0, 0)),
                  pl.BlockSpec((D_HY, D_MODEL), lambda i, j: (0, 0)),
                  pl.BlockSpec((1, D_MODEL), lambda i, j: (0, 0))],
        out_specs=[pl.BlockSpec((1, tm, D_MODEL), lambda i, j: (i, j, 0)),
                   pl.BlockSpec((1, tm, D_MODEL), lambda i, j: (i, j, 0))],
        out_shape=[jax.ShapeDtypeStruct((b, l, D_MODEL), F32),
                   jax.ShapeDtypeStruct((b, l, D_MODEL), BF16)],
        compiler_params=_cparams(("parallel", "parallel")),
        name="out_proj",
    )(x, o_r, oht, mod, gr, gh, wr, wh, g2)


FF_TM = 512
FF_SUB = 2 * LANES


def _ffn_kernel(x1_ref, xn_ref, xp_ref, xq_ref, mod_ref, wu_ref, cw_ref, cb_ref, wd_ref, gf_ref, o_ref, xe_s, acc_s,
                *, seg, halo, on_grid, tiles_per_seq, final_norm):
    t = pl.program_id(0)
    tm = x1_ref.shape[1]
    rows = tm + 2 * halo
    if halo:
        first = (t % tiles_per_seq) == 0
        last = (t % tiles_per_seq) == tiles_per_seq - 1
        xe_s[0:halo, :] = jnp.where(first, jnp.zeros_like(xp_ref[0]), xp_ref[0])
        xe_s[halo + tm:rows, :] = jnp.where(last, jnp.zeros_like(xq_ref[0]), xq_ref[0])
    xe_s[halo:halo + tm, :] = xn_ref[0]
    acc_s[...] = jnp.zeros_like(acc_s)

    pos = lax.broadcasted_iota(jnp.int32, (rows, 1), 0) % seg
    for c0 in range(0, D_FF, FF_SUB):
        w = min(FF_SUB, D_FF - c0)
        g = jnp.dot(xe_s[...], wu_ref[:, D_FF + c0:D_FF + c0 + w], preferred_element_type=F32)
        a = jnp.dot(xe_s[halo:halo + tm, :], wu_ref[:, c0:c0 + w], preferred_element_type=F32)
        gls = pltpu.roll(jnp.where(pos == seg - 1, 0.0, g), 1, axis=0)
        grs = pltpu.roll(jnp.where(pos == 0, 0.0, g), rows - 1, axis=0)
        acc = cb_ref[:, c0:c0 + w]
        for dr in ((-1, 0, 1) if on_grid else (0,)):
            lo = halo + dr * seg
            acc = (acc + cw_ref[dr + 1, 0:1, c0:c0 + w] * gls[lo:lo + tm]
                   + cw_ref[dr + 1, 1:2, c0:c0 + w] * g[lo:lo + tm]
                   + cw_ref[dr + 1, 2:3, c0:c0 + w] * grs[lo:lo + tm])
        h = (jax.nn.gelu(acc) * a).astype(BF16)
        acc_s[...] += jnp.dot(h, wd_ref[c0:c0 + w, :], preferred_element_type=F32)
    x2 = x1_ref[0] + mod_ref[0, 5:6, :] * acc_s[...]
    if final_norm:
        x2 = _rms(x2) * gf_ref[...]
    o_ref[0] = x2


def _ffn(x1, xn2, mod, wu, cw, cb, wd, gf, on_grid, final_norm):
    b, l, _ = x1.shape
    tm = FF_TM
    if on_grid:
        seg, halo = GRID_W, GRID_W
        tps = l // tm
    else:
        seg, halo = l, 0
        tps = 1
    nt = b * l // tm
    hb = GRID_W
    nhb = tm // hb
    last_hb = b * l // hb - 1
    mod_map = (lambda t: (t // tps, 0, 0)) if mod.shape[0] > 1 else (lambda t: (0, 0, 0))
    kern = functools.partial(_ffn_kernel, seg=seg, halo=halo, on_grid=on_grid, tiles_per_seq=tps, final_norm=final_norm)
    const = lambda *shape: pl.BlockSpec(shape, lambda t: (0,) * len(shape), pipeline_mode=pl.Buffered(1))
    xh = xn2.reshape(b * l // hb, hb, D_MODEL)
    out = pl.pallas_call(
        kern,
        grid=(nt,),
        in_specs=[pl.BlockSpec((1, tm, D_MODEL), lambda t: (t, 0, 0)),
                  pl.BlockSpec((1, tm, D_MODEL), lambda t: (t, 0, 0)),
                  pl.BlockSpec((1, hb, D_MODEL), lambda t: (jnp.maximum(t * nhb - 1, 0), 0, 0)),
                  pl.BlockSpec((1, hb, D_MODEL), lambda t: (jnp.minimum((t + 1) * nhb, last_hb), 0, 0)),
                  pl.BlockSpec((1, N_MOD, D_MODEL), mod_map),
                  const(D_MODEL, 2 * D_FF), const(3, 3, D_FF), const(1, D_FF), const(D_FF, D_MODEL), const(1, D_MODEL)],
        out_specs=pl.BlockSpec((1, tm, D_MODEL), lambda t: (t, 0, 0)),
        out_shape=jax.ShapeDtypeStruct((nt, tm, D_MODEL), F32),
        scratch_shapes=[pltpu.VMEM((tm + 2 * halo, D_MODEL), BF16), pltpu.VMEM((tm, D_MODEL), F32)],
        compiler_params=_cparams(("parallel",)),
        name="ffn",
    )(x1.reshape(nt, tm, D_MODEL), xn2.reshape(nt, tm, D_MODEL), xh, xh, mod, wu, cw, cb, wd, gf)
    return out.reshape(b, l, D_MODEL)


def _trunk_layer(x, mod, h0, p, kf, tables, on_grid, final_norm, g_final):
    b, l, _ = x.shape
    tm = min(512, l)
    xy, hyt = _in_proj(x, mod, p['g_norm1'], p['wxy'], p['whyt'], tm)
    o_r, states = _rglru(xy, p['rg_conv_w'], p['rg_conv_b'], p['wg'], p['bg'], p['ap'], h0)
    oht = _hyena(hyt, p['hy_cw'], p['hy_cb'], p['hy_bias'], kf, tables)
    x1, xn2 = _out_proj(x, o_r, oht, mod, p['g_rnn_out'], p['g_hy_out'], p['w_out_r'], p['w_out_h'], p['g_norm2'], tm)
    x2 = _ffn(x1, xn2, mod, p['w_up'], p['ffn_conv_w'], p['ffn_conv_b'], p['w_down'], g_final, on_grid, final_norm)
    return x2, states


def kernel(x_prompt, x_sample, state_rglru, c, c_ctx, w_ada, b_ada, g_norm1, g_norm2, w_in, rg_conv_w, rg_conv_b, rg_gate_w, rg_gate_b, rg_a, hy_conv_w, hy_conv_b, hf_w1, hf_b1, hf_w2, hf_b2, hf_w3, hf_freq, hy_bias, g_rnn_out, g_hy_out, w_out, w_up, ffn_conv_w, ffn_conv_b, w_down, g_final):
    depth = w_in.shape[0]
    nb_ctx, l_ctx, _ = x_prompt.shape
    nb_lat, l_lat, _ = x_sample.shape

    cc = jnp.zeros((SUBLANES, D_MODEL), F32).at[0].set(c_ctx).at[1:1 + nb_lat].set(c)
    mods = _modulation(cc, w_ada, b_ada).reshape(depth, SUBLANES, N_MOD, D_MODEL)

    tab_ctx = _dft_tables(2 * l_ctx // DFT2)
    tab_lat = _dft_tables(2 * l_lat // DFT2)
    gf = g_final.reshape(1, D_MODEL)
    zero_h = jnp.zeros((nb_ctx, 2, D_RNN), F32)

    xp, xs = x_prompt, x_sample
    new_states = []
    for l in range(depth):
        wg, bg, ap = _gate_weights(rg_gate_w[l], rg_gate_b[l], rg_a[l])
        p = {
            'g_norm1': g_norm1[l].reshape(1, D_MODEL), 'g_norm2': g_norm2[l].reshape(1, D_MODEL),
            'wxy': w_in[l, :, :2 * D_RNN].astype(BF16), 'whyt': w_in[l, :, 2 * D_RNN:].T.astype(BF16),
            'rg_conv_w': rg_conv_w[l], 'rg_conv_b': rg_conv_b[l].reshape(1, D_RNN),
            'wg': wg, 'bg': bg, 'ap': ap,
            'hy_cw': hy_conv_w[l].reshape(3, 3, D_HY, 1), 'hy_cb': hy_conv_b[l].reshape(3, D_HY, 1),
            'hy_bias': hy_bias[l].reshape(2, D_HY, 1),
            'g_rnn_out': g_rnn_out[l].reshape(1, D_RNN), 'g_hy_out': g_hy_out[l].reshape(D_HY, 1),
            'w_out_r': w_out[l, :D_RNN].astype(BF16), 'w_out_h': w_out[l, D_RNN:].astype(BF16),
            'w_up': w_up[l].astype(BF16), 'ffn_conv_w': ffn_conv_w[l], 'ffn_conv_b': ffn_conv_b[l].reshape(1, D_FF),
            'w_down': w_down[l].astype(BF16),
        }
        fargs = (hf_w1[l], hf_b1[l], hf_w2[l], hf_b2[l], hf_w3[l], hf_freq[l])
        kf_ctx = _hyena_filters(l_ctx, *fargs, tab_ctx)
        kf_lat = _hyena_filters(l_lat, *fargs, tab_lat)
        final = l == depth - 1
        xp, st = _trunk_layer(xp, mods[l, 0:1], zero_h, p, kf_ctx, tab_ctx, False, final, gf)
        new_states.append(st)
        xs, _ = _trunk_layer(xs, mods[l, 1:1 + nb_lat], state_rglru[:, l], p, kf_lat, tab_lat, True, final, gf)
    return (xp, xs, jnp.stack(new_states, axis=1))
```

```python
import functools
import math

import jax
import jax.numpy as jnp
import ml_dtypes
import numpy as np
from jax import lax
from jax.experimental import pallas as pl
from jax.experimental.pallas import tpu as pltpu

F32 = jnp.float32
BF16 = jnp.bfloat16

D_MODEL = 1024
D_RNN = 512
D_HY = 512
N_HEADS = 8
HEAD_DIM = D_RNN // N_HEADS
RG_C = 8.0
GRID_W = 64
HY_BANDS = 16
HY_FH = 64
D_FF = 2816
N_MOD = 6
EPS = 1e-6

SUBLANES = 8
LANES = 128
VMEM_LIMIT = 56 * 1024 * 1024

DFT2 = LANES
FFT_PASSES = 2


def _cparams(sem):
    return pltpu.CompilerParams(dimension_semantics=sem, vmem_limit_bytes=VMEM_LIMIT)


def _rms(x):
    return x * lax.rsqrt(jnp.mean(x * x, axis=-1, keepdims=True) + EPS)


def _mod_kernel(c_ref, w_ref, b_ref, o_ref):
    c = c_ref[...]
    s = c * jax.nn.sigmoid(c)
    o_ref[0] = jnp.dot(s, w_ref[0], precision=lax.Precision.HIGHEST, preferred_element_type=F32) + b_ref[0]


def _modulation(cc, w_ada, b_ada):
    depth, _, n = w_ada.shape
    tn = 1536
    return pl.pallas_call(
        _mod_kernel,
        grid=(depth, n // tn),
        in_specs=[pl.BlockSpec((SUBLANES, D_MODEL), lambda l, j: (0, 0)),
                  pl.BlockSpec((1, D_MODEL, tn), lambda l, j: (l, 0, j)),
                  pl.BlockSpec((1, 1, tn), lambda l, j: (l, 0, j))],
        out_specs=pl.BlockSpec((1, SUBLANES, tn), lambda l, j: (l, 0, j)),
        out_shape=jax.ShapeDtypeStruct((depth, SUBLANES, n), F32),
        compiler_params=_cparams(("parallel", "parallel")),
        name="adaln_mod",
    )(cc, w_ada, b_ada.reshape(depth, 1, n))


def _in_proj_kernel(x_ref, mod_ref, g_ref, wxy_ref, whyt_ref, xy_ref, hyt_ref):
    x = x_ref[0]
    xn = _rms(x) * g_ref[...]
    xn = (xn * (1.0 + mod_ref[0, 1:2, :]) + mod_ref[0, 0:1, :]).astype(BF16)
    xy_ref[0] = jnp.dot(xn, wxy_ref[...], preferred_element_type=F32)
    hyt_ref[0] = lax.dot_general(whyt_ref[...], xn, (((1,), (1,)), ((), ())), preferred_element_type=F32)


def _in_proj(x, mod, g, wxy, whyt, tm):
    b, l, _ = x.shape
    per_batch_mod = mod.shape[0] > 1
    mod_map = (lambda i, j: (i, 0, 0)) if per_batch_mod else (lambda i, j: (0, 0, 0))
    return pl.pallas_call(
        _in_proj_kernel,
        grid=(b, l // tm),
        in_specs=[pl.BlockSpec((1, tm, D_MODEL), lambda i, j: (i, j, 0)),
                  pl.BlockSpec((1, N_MOD, D_MODEL), mod_map),
                  pl.BlockSpec((1, D_MODEL), lambda i, j: (0, 0)),
                  pl.BlockSpec((D_MODEL, 2 * D_RNN), lambda i, j: (0, 0)),
                  pl.BlockSpec((3 * D_HY, D_MODEL), lambda i, j: (0, 0))],
        out_specs=[pl.BlockSpec((1, tm, 2 * D_RNN), lambda i, j: (i, j, 0)),
                   pl.BlockSpec((1, 3 * D_HY, tm), lambda i, j: (i, 0, j))],
        out_shape=[jax.ShapeDtypeStruct((b, l, 2 * D_RNN), F32),
                   jax.ShapeDtypeStruct((b, 3 * D_HY, l), F32)],
        compiler_params=_cparams(("parallel", "parallel")),
        name="in_proj",
    )(x, mod, g, wxy, whyt)


RG_HALF = D_RNN // 2
RG_TILES = RG_HALF // LANES


def _rglru_kernel(xr_ref, yr_ref, cw_ref, cb_ref, wg_ref, bg_ref, ap_ref, h0_ref, o_ref, st_ref,
                  ext, xc_s, hf, hb, a_s, b_s, *, seq, tc):
    nchunks = seq // tc
    nblk = tc // SUBLANES
    ext[0:SUBLANES, :] = jnp.zeros((SUBLANES, RG_HALF), F32)
    ext[SUBLANES:SUBLANES + seq, :] = xr_ref[0]
    ext[SUBLANES + seq:2 * SUBLANES + seq, :] = jnp.zeros((SUBLANES, RG_HALF), F32)
    for c in range(nchunks):
        xc = cb_ref[...] + cw_ref[0:1, :] * ext[pl.ds(c * tc + SUBLANES - 2, tc), :]
        for k in range(1, 4):
            xc = xc + cw_ref[k:k + 1, :] * ext[pl.ds(c * tc + SUBLANES - 2 + k, tc), :]
        xc_s[pl.ds(c * tc, tc), :] = xc

    row = lax.broadcasted_iota(jnp.int32, (SUBLANES, LANES), 0)
    neg_c_sp = [-RG_C * jax.nn.softplus(-ap_ref[d, 0]) for d in range(2)]

    def gates(c0, d):
        xc = xc_s[pl.ds(c0, tc), :]
        g = jnp.dot(xc.astype(BF16), wg_ref[d, 0], preferred_element_type=F32) + bg_ref[d, 0]
        r = 0.5 * jnp.tanh(0.5 * g[:, :RG_HALF]) + 0.5
        i = 0.5 * jnp.tanh(0.5 * g[:, RG_HALF:]) + 0.5
        log_a = neg_c_sp[d] * r
        a = jnp.exp(log_a)
        mult = jnp.sqrt(jnp.tanh(-log_a) * (a * a + 1.0))
        a_s[d] = a
        b_s[d] = xc * i * mult

    def local_scan(a, b, reverse):
        for s in (1, 2, 4):
            if reverse:
                keep = row < SUBLANES - s
                shift = SUBLANES - s
            else:
                keep = row >= s
                shift = s
            a_sh = jnp.where(keep, pltpu.roll(a, shift, axis=0), 1.0)
            b_sh = jnp.where(keep, pltpu.roll(b, shift, axis=0), 0.0)
            b = a * b_sh + b
            a = a * a_sh
        return a, b

    def block_body(cf0, cb0):
        def body(j, carry):
            rf = pl.multiple_of(j * SUBLANES, SUBLANES)
            rb = pl.multiple_of(tc - SUBLANES - j * SUBLANES, SUBLANES)
            out = []
            for t in range(RG_TILES):
                lanes = slice(t * LANES, (t + 1) * LANES)
                pa, pb = local_scan(a_s[0, pl.ds(rf, SUBLANES), lanes], b_s[0, pl.ds(rf, SUBLANES), lanes], False)
                h = pa * carry[2 * t] + pb
                hf[pl.ds(cf0 + rf, SUBLANES), lanes] = h
                out.append(jnp.broadcast_to(h[SUBLANES - 1:SUBLANES, :], (SUBLANES, LANES)))
                pa, pb = local_scan(a_s[1, pl.ds(rb, SUBLANES), lanes], b_s[1, pl.ds(rb, SUBLANES), lanes], True)
                h = pa * carry[2 * t + 1] + pb
                hb[pl.ds(cb0 + rb, SUBLANES), lanes] = h
                out.append(jnp.broadcast_to(h[0:1, :], (SUBLANES, LANES)))
            return tuple(out)
        return body

    carry = []
    for t in range(RG_TILES):
        carry.append(jnp.broadcast_to(h0_ref[0, 0:1, t * LANES:(t + 1) * LANES], (SUBLANES, LANES)))
        carry.append(jnp.broadcast_to(h0_ref[0, 1:2, t * LANES:(t + 1) * LANES], (SUBLANES, LANES)))
    carry = tuple(carry)
    for c in range(nchunks):
        cf0 = c * tc
        cb0 = (nchunks - 1 - c) * tc
        gates(cf0, 0)
        gates(cb0, 1)
        carry = lax.fori_loop(0, nblk, block_body(cf0, cb0), carry, unroll=2)
    for t in range(RG_TILES):
        st_ref[0, 0:1, t * LANES:(t + 1) * LANES] = carry[2 * t][0:1, :]
        st_ref[0, 1:2, t * LANES:(t + 1) * LANES] = carry[2 * t + 1][0:1, :]
    for c in range(nchunks):
        rows = pl.ds(c * tc, tc)
        o_ref[0, rows, :] = (hf[rows, :] + hb[rows, :]) * jax.nn.gelu(yr_ref[0, rows, :])


def _rglru(xy, cw, cb, wg, bg, ap, h0):
    b, seq, _ = xy.shape
    tc = min(512, seq)
    nh = D_RNN // RG_HALF
    kern = functools.partial(_rglru_kernel, seq=seq, tc=tc)
    return pl.pallas_call(
        kern,
        grid=(b, nh),
        in_specs=[pl.BlockSpec((1, seq, RG_HALF), lambda i, h: (i, 0, h)),
                  pl.BlockSpec((1, seq, RG_HALF), lambda i, h: (i, 0, nh + h)),
                  pl.BlockSpec((4, RG_HALF), lambda i, h: (0, h)),
                  pl.BlockSpec((1, RG_HALF), lambda i, h: (0, h)),
                  pl.BlockSpec((2, 1, RG_HALF, 2 * RG_HALF), lambda i, h: (0, h, 0, 0)),
                  pl.BlockSpec((2, 1, 1, 2 * RG_HALF), lambda i, h: (0, h, 0, 0)),
                  pl.BlockSpec((2, 1, 1, RG_HALF), lambda i, h: (0, h, 0, 0)),
                  pl.BlockSpec((1, 2, RG_HALF), lambda i, h: (i, 0, h))],
        out_specs=[pl.BlockSpec((1, seq, RG_HALF), lambda i, h: (i, 0, h)),
                   pl.BlockSpec((1, 2, RG_HALF), lambda i, h: (i, 0, h))],
        out_shape=[jax.ShapeDtypeStruct((b, seq, D_RNN), F32),
                   jax.ShapeDtypeStruct((b, 2, D_RNN), F32)],
        scratch_shapes=[pltpu.VMEM((seq + 2 * SUBLANES, RG_HALF), F32),
                        pltpu.VMEM((seq, RG_HALF), F32),
                        pltpu.VMEM((seq, RG_HALF), F32),
                        pltpu.VMEM((seq, RG_HALF), F32),
                        pltpu.VMEM((2, tc, RG_HALF), F32),
                        pltpu.VMEM((2, tc, RG_HALF), F32)],
        compiler_params=_cparams(("parallel", "parallel")),
        name="rglru",
    )(xy, xy, cw, cb, wg, bg, ap, h0)


def _gate_weights(gate_w, gate_b, a_param):
    nh = D_RNN // RG_HALF
    hp = N_HEADS // nh
    w = gate_w.reshape(2, 2, nh, hp, HEAD_DIM, HEAD_DIM)
    eye = jnp.eye(hp, dtype=F32)
    dense = jnp.einsum('dghpio,pq->dhpigqo', w, eye)
    dense = dense.reshape(2, nh, RG_HALF, 2 * RG_HALF).astype(BF16)
    bias = gate_b.reshape(2, 2, nh, RG_HALF).transpose(0, 2, 1, 3).reshape(2, nh, 1, 2 * RG_HALF)
    ap = a_param.reshape(2, nh, 1, RG_HALF)
    return dense, bias, ap


def _add(a, b):
    if a is None:
        return b
    if b is None:
        return a
    return a + b


def _sub(a, b):
    if b is None:
        return a
    if a is None:
        return -b
    return a - b


def _scale(a, s):
    if a is None or s == 0.0:
        return None
    if s == 1.0:
        return a
    if s == -1.0:
        return -a
    return a * s


def _cmul_const(z, wr, wi):
    if z is None:
        return None
    re, im = z
    if abs(wr) < 1e-15:
        wr = 0.0
    if abs(wi) < 1e-15:
        wi = 0.0
    return (_sub(_scale(re, wr), _scale(im, wi)), _add(_scale(re, wi), _scale(im, wr)))


def _cadd(a, b):
    if a is None:
        return b
    if b is None:
        return a
    return (_add(a[0], b[0]), _add(a[1], b[1]))


def _csub(a, b):
    if b is None:
        return a
    if a is None:
        return (_sub(None, b[0]), _sub(None, b[1]))
    return (_sub(a[0], b[0]), _sub(a[1], b[1]))


def _cfft(xs, sign):
    n = len(xs)
    if n == 1:
        return list(xs)
    even = _cfft(xs[0::2], sign)
    odd = _cfft(xs[1::2], sign)
    out = [None] * n
    for k in range(n // 2):
        ang = sign * 2.0 * math.pi * k / n
        t = _cmul_const(odd[k], math.cos(ang), math.sin(ang))
        out[k] = _cadd(even[k], t)
        out[k + n // 2] = _csub(even[k], t)
    return out


def _dft_tables(n1, rc):
    n = n1 * DFT2
    k = np.arange(DFT2, dtype=np.float64)
    ang = 2.0 * np.pi * np.outer(k, k) / DFT2
    c, s = np.cos(ang), np.sin(ang)
    fwd = np.block([[c, -s], [s, c]])
    inv = np.block([[c, s], [-s, c]]) / n

    def split(m):
        hi = m.astype(np.float32).astype(ml_dtypes.bfloat16)
        lo = (m - hi.astype(np.float64)).astype(ml_dtypes.bfloat16)
        return jnp.asarray(hi), jnp.asarray(lo)

    tw_ang = 2.0 * np.pi * np.outer(np.arange(n1, dtype=np.float64), k) / n
    tw = np.concatenate([np.cos(tw_ang), -np.sin(tw_ang)], axis=1)
    tw = np.repeat(tw, rc, axis=0).astype(np.float32)
    return split(fwd) + split(inv) + (jnp.asarray(tw),)


HY_CB = 32


def _fft_rows(n1, cb):
    return SUBLANES if n1 >= 16 else cb


def _dft_mm(x, hi_ref, lo_ref):
    xh = x.astype(BF16)
    acc = jnp.dot(xh, hi_ref[...], preferred_element_type=F32)
    if FFT_PASSES >= 2:
        acc = acc + jnp.dot(xh, lo_ref[...], preferred_element_type=F32)
    if FFT_PASSES >= 3:
        xl = (x - xh.astype(F32)).astype(BF16)
        acc = acc + jnp.dot(xl, hi_ref[...], preferred_element_type=F32)
    return acc


def _fwd_lane_fft(blocks, tw_ref, n1, rc):
    xs = _cfft(blocks, -1)
    out = []
    for k1 in range(n1):
        z = xs[k1]
        if k1 > 0 and z is not None:
            twr = tw_ref[k1 * rc:(k1 + 1) * rc, 0:LANES]
            twi = tw_ref[k1 * rc:(k1 + 1) * rc, LANES:2 * LANES]
            re, im = z
            if im is None:
                z = (re * twr, re * twi)
            else:
                z = (re * twr - im * twi, re * twi + im * twr)
        out.append(z)
    return out


def _inv_lane_fft(blocks, tw_ref, n1, rc):
    gs = []
    for k1 in range(n1):
        re, im = blocks[k1]
        if k1 > 0:
            twr = tw_ref[k1 * rc:(k1 + 1) * rc, 0:LANES]
            twi = tw_ref[k1 * rc:(k1 + 1) * rc, LANES:2 * LANES]
            re, im = re * twr + im * twi, im * twr - re * twi
        gs.append((re, im))
    return _cfft(gs, +1)


def _filter_kernel(feat_ref, w1_ref, b1_ref, w2_ref, b2_ref, fr_ref, w3_ref, fh_ref, fl_ref, tw_ref, kf_ref,
                   h2_s, k_s, s_s, *, seq, n1, cb, rc):
    n = 2 * seq
    hi = lax.Precision.HIGHEST

    @pl.when(pl.program_id(0) == 0)
    def _():
        h = jnp.dot(w1_ref[...], feat_ref[...], precision=hi, preferred_element_type=F32) + b1_ref[...]
        h = jnp.sin(fr_ref[:, 0:1] * h)
        h = jnp.dot(w2_ref[...], h, precision=hi, preferred_element_type=F32) + b2_ref[...]
        h2_s[...] = jnp.sin(fr_ref[:, 1:2] * h)

    lane = lax.broadcasted_iota(jnp.int32, (cb, n), 1)
    pos = jnp.where(lane < seq, lane, n - lane).astype(F32)
    t = pos / float(max(seq - 1, 1))
    d_idx = (lax.broadcasted_iota(jnp.int32, (cb, n), 0) + pl.program_id(0) * cb).astype(F32)
    min_decay = math.log(1e-2) / 1.5
    max_decay = math.log(1e-2) / 0.3
    delta = jnp.abs(min_decay + d_idx * ((max_decay - min_decay) / (D_HY - 1)))
    decay = jnp.exp(-t * delta)
    hfb = jnp.dot(w3_ref[...].reshape(4 * cb, HY_FH), h2_s[...], precision=hi, preferred_element_type=F32)
    for o in range(2):
        hf = hfb[(2 * o) * cb:(2 * o + 1) * cb]
        hb = hfb[(2 * o + 1) * cb:(2 * o + 2) * cb]
        k_s[o] = jnp.where(lane < seq, hf, jnp.where(lane == seq, 0.0, hb)) * decay

    nrc = cb // rc
    for o in range(2):
        def rows(i, _, o=o):
            r0 = pl.multiple_of(i * rc, rc)
            blocks = [(k_s[o, pl.ds(r0, rc), j * LANES:(j + 1) * LANES], None) for j in range(n1)]
            zs = _fwd_lane_fft(blocks, tw_ref, n1, rc)
            for k1 in range(n1):
                re, im = zs[k1]
                s_s[pl.ds(k1 * cb + r0, rc), 0:LANES] = re
                s_s[pl.ds(k1 * cb + r0, rc), LANES:2 * LANES] = im if im is not None else jnp.zeros_like(re)
            return 0
        lax.fori_loop(0, nrc, rows, 0)
        x = s_s[...]
        xh = x.astype(BF16)
        xl = (x - xh.astype(F32)).astype(BF16)
        z = (jnp.dot(xh, fh_ref[...], preferred_element_type=F32) + jnp.dot(xh, fl_ref[...], preferred_element_type=F32)
             + jnp.dot(xl, fh_ref[...], preferred_element_type=F32))
        for k1 in range(n1):
            kf_ref[o, k1] = z[k1 * cb:(k1 + 1) * cb, :]


def _hyena_filters(seq, w1, b1, w2, b2, w3, freq, tables):
    n1 = 2 * seq // DFT2
    n = 2 * seq
    cb = HY_CB
    rc = _fft_rows(n1, cb)
    fh, fl, _, _, tw = tables
    pos = np.arange(n, dtype=np.float64)
    pos = np.where(pos < seq, pos, n - pos)
    tt = pos / max(seq - 1, 1)
    omega = 2.0 * math.pi * pos / seq
    bands = np.linspace(1e-4, HY_BANDS - 1, HY_BANDS)
    ang = omega[None, :] * bands[:, None]
    feats = np.concatenate([tt[None, :], np.cos(ang), np.sin(ang)], axis=0)
    nfeat = LANES
    feats = np.pad(feats, ((0, nfeat - feats.shape[0]), (0, 0))).astype(np.float32)
    w1 = jnp.pad(w1, ((0, nfeat - w1.shape[0]), (0, 0)))
    w3t = w3.T.reshape(4, D_HY, HY_FH)
    kern = functools.partial(_filter_kernel, seq=seq, n1=n1, cb=cb, rc=rc)
    full = lambda *shape: pl.BlockSpec(shape, lambda i: (0,) * len(shape))
    return pl.pallas_call(
        kern,
        grid=(D_HY // cb,),
        in_specs=[full(nfeat, n), full(HY_FH, nfeat), full(HY_FH, 1), full(HY_FH, HY_FH), full(HY_FH, 1),
                  full(HY_FH, 2),
                  pl.BlockSpec((4, cb, HY_FH), lambda i: (0, i, 0)),
                  full(2 * DFT2, 2 * DFT2), full(2 * DFT2, 2 * DFT2), full(n1 * rc, 2 * DFT2)],
        out_specs=pl.BlockSpec((2, n1, cb, 2 * DFT2), lambda i: (0, 0, i, 0)),
        out_shape=jax.ShapeDtypeStruct((2, n1, D_HY, 2 * DFT2), F32),
        scratch_shapes=[pltpu.VMEM((HY_FH, n), F32), pltpu.VMEM((2, cb, n), F32),
                        pltpu.VMEM((n1 * cb, 2 * DFT2), F32)],
        compiler_params=_cparams(("arbitrary",)),
        name="hyena_filters",
    )(jnp.asarray(feats), w1.T, b1.reshape(HY_FH, 1), w2.T, b2.reshape(HY_FH, 1), freq.T, w3t, fh, fl, tw)


def _hyena_kernel(v_ref, x1_ref, x2_ref, cw_ref, cb_ref, bias_ref, kf_ref, fh_ref, fl_ref, ih_ref, il_ref, tw_ref,
                  o_ref, v_s, z_s, s_s, *, seq, n1, cb, nb, rc, unroll):
    npairs = nb // 2
    nin = n1 // 2
    nrc = cb // rc
    lane = lax.broadcasted_iota(jnp.int32, (rc, seq), 1)

    def conv3(h, part, r0):
        w0 = cw_ref[0, part, pl.ds(r0, rc), :]
        w1 = cw_ref[1, part, pl.ds(r0, rc), :]
        w2 = cw_ref[2, part, pl.ds(r0, rc), :]
        bb = cb_ref[part, pl.ds(r0, rc), :]
        hm = jnp.where(lane == 0, 0.0, pltpu.roll(h, 1, axis=1))
        hp = jnp.where(lane == seq - 1, 0.0, pltpu.roll(h, seq - 1, axis=1))
        return w0 * hm + w1 * h + w2 * hp + bb

    def prep(i, _):
        b = i // nrc
        r0 = pl.multiple_of((i % nrc) * rc, rc)
        v_s[b, pl.ds(r0, rc), :] = conv3(v_ref[b, pl.ds(r0, rc), :], 0, r0)
        return 0
    lax.fori_loop(0, nb * nrc, prep, 0, unroll=2)

    rows_per_k1 = npairs * cb
    total_rows = n1 * rows_per_k1
    chunk = 512 if total_rows % 512 == 0 and (512 % rows_per_k1 == 0 or rows_per_k1 % 512 == 0) else rows_per_k1

    for o in range(2):
        src = v_s if o == 0 else z_s
        dst = z_s if o == 0 else o_ref
        xg_ref = x1_ref if o == 0 else x2_ref

        def fwd(i, _, src=src):
            p = i // nrc
            r0 = pl.multiple_of((i % nrc) * rc, rc)
            blocks = [(src[2 * p, pl.ds(r0, rc), j * LANES:(j + 1) * LANES],
                       src[2 * p + 1, pl.ds(r0, rc), j * LANES:(j + 1) * LANES]) if j < nin else None
                      for j in range(n1)]
            zs = _fwd_lane_fft(blocks, tw_ref, n1, rc)
            base = pl.multiple_of(p * cb + r0, rc)
            for k1 in range(n1):
                s_s[pl.ds(k1 * rows_per_k1 + base, rc), 0:LANES] = zs[k1][0]
                s_s[pl.ds(k1 * rows_per_k1 + base, rc), LANES:2 * LANES] = zs[k1][1]
            return 0
        lax.fori_loop(0, npairs * nrc, fwd, 0, unroll=unroll)

        for c in range(total_rows // chunk):
            rows = slice(c * chunk, (c + 1) * chunk)
            z = _dft_mm(s_s[rows, :], fh_ref, fl_ref)
            if chunk >= rows_per_k1:
                kparts = []
                for k1 in range(c * chunk // rows_per_k1, (c + 1) * chunk // rows_per_k1):
                    kparts += [kf_ref[o, k1]] * npairs
                kk = jnp.concatenate(kparts, axis=0) if len(kparts) > 1 else kparts[0]
            else:
                k1 = c * chunk // rows_per_k1
                kk = jnp.concatenate([kf_ref[o, k1]] * (chunk // cb), axis=0)
            zr, zi = z[:, :LANES], z[:, LANES:]
            kr, ki = kk[:, :LANES], kk[:, LANES:]
            w = jnp.concatenate([zr * kr - zi * ki, zr * ki + zi * kr], axis=1)
            s_s[rows, :] = _dft_mm(w, ih_ref, il_ref)

        def inv(i, _, src=src, dst=dst, xg_ref=xg_ref, o=o):
            p = i // nrc
            r0 = pl.multiple_of((i % nrc) * rc, rc)
            base = pl.multiple_of(p * cb + r0, rc)
            blocks = [(s_s[pl.ds(k1 * rows_per_k1 + base, rc), 0:LANES],
                       s_s[pl.ds(k1 * rows_per_k1 + base, rc), LANES:2 * LANES]) for k1 in range(n1)]
            ys = _inv_lane_fft(blocks, tw_ref, n1, rc)
            bias = bias_ref[o, pl.ds(r0, rc), :]
            for q in range(2):
                xg = conv3(xg_ref[2 * p + q, pl.ds(r0, rc), :], 1 + o, r0)
                for j in range(nin):
                    u = src[2 * p + q, pl.ds(r0, rc), j * LANES:(j + 1) * LANES]
                    y = ys[j][q]
                    dst[2 * p + q, pl.ds(r0, rc), j * LANES:(j + 1) * LANES] = (
                        xg[:, j * LANES:(j + 1) * LANES] * (y + u * bias))
            return 0
        lax.fori_loop(0, npairs * nrc, inv, 0, unroll=unroll)


def _hyena(hyt, cw, cbias, bias, kf, tables):
    nb, _, seq = hyt.shape
    n1 = 2 * seq // DFT2
    cb = HY_CB
    rc = _fft_rows(n1, cb)
    nblk = D_HY // cb
    fh, fl, ih, il, tw = tables
    kern = functools.partial(_hyena_kernel, seq=seq, n1=n1, cb=cb, nb=nb, rc=rc, unroll=1 if n1 >= 16 else 2)
    full = lambda *shape: pl.BlockSpec(shape, lambda i: (0,) * len(shape))
    rows = n1 * (nb // 2) * cb
    return pl.pallas_call(
        kern,
        grid=(nblk,),
        in_specs=[pl.BlockSpec((nb, cb, seq), lambda i: (0, i, 0)),
                  pl.BlockSpec((nb, cb, seq), lambda i: (0, nblk + i, 0)),
                  pl.BlockSpec((nb, cb, seq), lambda i: (0, 2 * nblk + i, 0)),
                  pl.BlockSpec((3, 3, cb, 1), lambda i: (0, 0, i, 0)),
                  pl.BlockSpec((3, cb, 1), lambda i: (0, i, 0)),
                  pl.BlockSpec((2, cb, 1), lambda i: (0, i, 0)),
                  pl.BlockSpec((2, n1, cb, 2 * DFT2), lambda i: (0, 0, i, 0)),
                  full(2 * DFT2, 2 * DFT2), full(2 * DFT2, 2 * DFT2),
                  full(2 * DFT2, 2 * DFT2), full(2 * DFT2, 2 * DFT2),
                  full(n1 * rc, 2 * DFT2)],
        out_specs=pl.BlockSpec((nb, cb, seq), lambda i: (0, i, 0)),
        out_shape=jax.ShapeDtypeStruct((nb, D_HY, seq), F32),
        scratch_shapes=[pltpu.VMEM((nb, cb, seq), F32), pltpu.VMEM((nb, cb, seq), F32),
                        pltpu.VMEM((rows, 2 * DFT2), F32)],
        compiler_params=_cparams(("parallel",)),
        name="hyena",
    )(hyt, hyt, hyt, cw, cbias, bias, kf, fh, fl, ih, il, tw)


def _out_proj_kernel(x_ref, or_ref, oht_ref, mod_ref, gr_ref, gh_ref, wr_ref, wh_ref, g2_ref, x1_ref, xn_ref):
    orn = (_rms(or_ref[0]) * gr_ref[...]).astype(BF16)
    oh = oht_ref[0]
    ohn = oh * lax.rsqrt(jnp.mean(oh * oh, axis=0, keepdims=True) + EPS) * gh_ref[...]
    o = jnp.dot(orn, wr_ref[...], preferred_element_type=F32)
    o = o + lax.dot_general(ohn.astype(BF16), wh_ref[...], (((0,), (0,)), ((), ())), preferred_element_type=F32)
    x1 = x_ref[0] + mod_ref[0, 2:3, :] * o
    x1_ref[0] = x1
    xn = _rms(x1) * g2_ref[...]
    xn_ref[0] = (xn * (1.0 + mod_ref[0, 4:5, :]) + mod_ref[0, 3:4, :]).astype(BF16)


def _out_proj(x, o_r, oht, mod, gr, gh, wr, wh, g2, tm):
    b, l, _ = x.shape
    mod_map = (lambda i, j: (i, 0, 0)) if mod.shape[0] > 1 else (lambda i, j: (0, 0, 0))
    return pl.pallas_call(
        _out_proj_kernel,
        grid=(b, l // tm),
        in_specs=[pl.BlockSpec((1, tm, D_MODEL), lambda i, j: (i, j, 0)),
                  pl.BlockSpec((1, tm, D_RNN), lambda i, j: (i, j, 0)),
                  pl.BlockSpec((1, D_HY, tm), lambda i, j: (i, 0, j)),
                  pl.BlockSpec((1, N_MOD, D_MODEL), mod_map),
                  pl.BlockSpec((1, D_RNN), lambda i, j: (0, 0)),
                  pl.BlockSpec((D_HY, 1), lambda i, j: (0, 0)),
                  pl.BlockSpec((D_RNN, D_MODEL), lambda i, j: (0, 0)),
                  pl.BlockSpec((D_HY, D_MODEL), lambda i, j: (0, 0)),
                  pl.BlockSpec((1, D_MODEL), lambda i, j: (0, 0))],
        out_specs=[pl.BlockSpec((1, tm, D_MODEL), lambda i, j: (i, j, 0)),
                   pl.BlockSpec((1, tm, D_MODEL), lambda i, j: (i, j, 0))],
        out_shape=[jax.ShapeDtypeStruct((b, l, D_MODEL), F32),
                   jax.ShapeDtypeStruct((b, l, D_MODEL), BF16)],
        compiler_params=_cparams(("parallel", "parallel")),
        name="out_proj",
    )(x, o_r, oht, mod, gr, gh, wr, wh, g2)


FF_TM = 512
FF_SUB = 2 * LANES


def _ffn_kernel(x1_ref, xn_ref, xp_ref, xq_ref, mod_ref, wu_ref, cw_ref, cb_ref, wd_ref, gf_ref, o_ref, xe_s, h_s,
                *, seg, halo, on_grid, tiles_per_seq, final_norm):
    t = pl.program_id(0)
    tm = x1_ref.shape[1]
    rows = tm + 2 * halo
    if halo:
        first = (t % tiles_per_seq) == 0
        last = (t % tiles_per_seq) == tiles_per_seq - 1
        xe_s[0:halo, :] = jnp.where(first, jnp.zeros_like(xp_ref[0]), xp_ref[0])
        xe_s[halo + tm:rows, :] = jnp.where(last, jnp.zeros_like(xq_ref[0]), xq_ref[0])
    xe_s[halo:halo + tm, :] = xn_ref[0]

    pos = lax.broadcasted_iota(jnp.int32, (rows, 1), 0) % seg
    for c0 in range(0, D_FF, FF_SUB):
        w = min(FF_SUB, D_FF - c0)
        g = jnp.dot(xe_s[...], wu_ref[:, D_FF + c0:D_FF + c0 + w], preferred_element_type=F32)
        a = jnp.dot(xe_s[halo:halo + tm, :], wu_ref[:, c0:c0 + w], preferred_element_type=F32)
        gls = pltpu.roll(jnp.where(pos == seg - 1, 0.0, g), 1, axis=0)
        grs = pltpu.roll(jnp.where(pos == 0, 0.0, g), rows - 1, axis=0)
        acc = cb_ref[:, c0:c0 + w]
        for dr in ((-1, 0, 1) if on_grid else (0,)):
            lo = halo + dr * seg
            acc = (acc + cw_ref[dr + 1, 0:1, c0:c0 + w] * gls[lo:lo + tm]
                   + cw_ref[dr + 1, 1:2, c0:c0 + w] * g[lo:lo + tm]
                   + cw_ref[dr + 1, 2:3, c0:c0 + w] * grs[lo:lo + tm])
        h_s[:, c0:c0 + w] = (jax.nn.gelu(acc) * a).astype(BF16)
    y = jnp.dot(h_s[...], wd_ref[...], preferred_element_type=F32)
    x2 = x1_ref[0] + mod_ref[0, 5:6, :] * y
    if final_norm:
        x2 = _rms(x2) * gf_ref[...]
    o_ref[0] = x2


def _ffn(x1, xn2, mod, wu, cw, cb, wd, gf, on_grid, final_norm):
    b, l, _ = x1.shape
    tm = FF_TM
    if on_grid:
        seg, halo = GRID_W, GRID_W
        tps = l // tm
    else:
        seg, halo = l, 0
        tps = 1
    nt = b * l // tm
    hb = GRID_W
    nhb = tm // hb
    last_hb = b * l // hb - 1
    mod_map = (lambda t: (t // tps, 0, 0)) if mod.shape[0] > 1 else (lambda t: (0, 0, 0))
    kern = functools.partial(_ffn_kernel, seg=seg, halo=halo, on_grid=on_grid, tiles_per_seq=tps, final_norm=final_norm)
    const = lambda *shape: pl.BlockSpec(shape, lambda t: (0,) * len(shape), pipeline_mode=pl.Buffered(1))
    xh = xn2.reshape(b * l // hb, hb, D_MODEL)
    out = pl.pallas_call(
        kern,
        grid=(nt,),
        in_specs=[pl.BlockSpec((1, tm, D_MODEL), lambda t: (t, 0, 0)),
                  pl.BlockSpec((1, tm, D_MODEL), lambda t: (t, 0, 0)),
                  pl.BlockSpec((1, hb, D_MODEL), lambda t: (jnp.maximum(t * nhb - 1, 0), 0, 0)),
                  pl.BlockSpec((1, hb, D_MODEL), lambda t: (jnp.minimum((t + 1) * nhb, last_hb), 0, 0)),
                  pl.BlockSpec((1, N_MOD, D_MODEL), mod_map),
                  const(D_MODEL, 2 * D_FF), const(3, 3, D_FF), const(1, D_FF), const(D_FF, D_MODEL), const(1, D_MODEL)],
        out_specs=pl.BlockSpec((1, tm, D_MODEL), lambda t: (t, 0, 0)),
        out_shape=jax.ShapeDtypeStruct((nt, tm, D_MODEL), F32),
        scratch_shapes=[pltpu.VMEM((tm + 2 * halo, D_MODEL), BF16), pltpu.VMEM((tm, D_FF), BF16)],
        compiler_params=_cparams(("parallel",)),
        name="ffn",
    )(x1.reshape(nt, tm, D_MODEL), xn2.reshape(nt, tm, D_MODEL), xh, xh, mod, wu, cw, cb, wd, gf)
    return out.reshape(b, l, D_MODEL)


def _trunk_layer(x, mod, h0, p, kf, tables, on_grid, final_norm, g_final):
    b, l, _ = x.shape
    tm = min(512, l)
    xy, hyt = _in_proj(x, mod, p['g_norm1'], p['wxy'], p['whyt'], tm)
    o_r, states = _rglru(xy, p['rg_conv_w'], p['rg_conv_b'], p['wg'], p['bg'], p['ap'], h0)
    oht = _hyena(hyt, p['hy_cw'], p['hy_cb'], p['hy_bias'], kf, tables)
    x1, xn2 = _out_proj(x, o_r, oht, mod, p['g_rnn_out'], p['g_hy_out'], p['w_out_r'], p['w_out_h'], p['g_norm2'], tm)
    x2 = _ffn(x1, xn2, mod, p['w_up'], p['ffn_conv_w'], p['ffn_conv_b'], p['w_down'], g_final, on_grid, final_norm)
    return x2, states


def kernel(x_prompt, x_sample, state_rglru, c, c_ctx, w_ada, b_ada, g_norm1, g_norm2, w_in, rg_conv_w, rg_conv_b, rg_gate_w, rg_gate_b, rg_a, hy_conv_w, hy_conv_b, hf_w1, hf_b1, hf_w2, hf_b2, hf_w3, hf_freq, hy_bias, g_rnn_out, g_hy_out, w_out, w_up, ffn_conv_w, ffn_conv_b, w_down, g_final):
    depth = w_in.shape[0]
    nb_ctx, l_ctx, _ = x_prompt.shape
    nb_lat, l_lat, _ = x_sample.shape

    cc = jnp.zeros((SUBLANES, D_MODEL), F32).at[0].set(c_ctx).at[1:1 + nb_lat].set(c)
    mods = _modulation(cc, w_ada, b_ada).reshape(depth, SUBLANES, N_MOD, D_MODEL)

    tab_ctx = _dft_tables(2 * l_ctx // DFT2, _fft_rows(2 * l_ctx // DFT2, HY_CB))
    tab_lat = _dft_tables(2 * l_lat // DFT2, _fft_rows(2 * l_lat // DFT2, HY_CB))
    gf = g_final.reshape(1, D_MODEL)
    zero_h = jnp.zeros((nb_ctx, 2, D_RNN), F32)

    xp, xs = x_prompt, x_sample
    new_states = []
    for l in range(depth):
        wg, bg, ap = _gate_weights(rg_gate_w[l], rg_gate_b[l], rg_a[l])
        p = {
            'g_norm1': g_norm1[l].reshape(1, D_MODEL), 'g_norm2': g_norm2[l].reshape(1, D_MODEL),
            'wxy': w_in[l, :, :2 * D_RNN].astype(BF16), 'whyt': w_in[l, :, 2 * D_RNN:].T.astype(BF16),
            'rg_conv_w': rg_conv_w[l], 'rg_conv_b': rg_conv_b[l].reshape(1, D_RNN),
            'wg': wg, 'bg': bg, 'ap': ap,
            'hy_cw': hy_conv_w[l].reshape(3, 3, D_HY, 1), 'hy_cb': hy_conv_b[l].reshape(3, D_HY, 1),
            'hy_bias': hy_bias[l].reshape(2, D_HY, 1),
            'g_rnn_out': g_rnn_out[l].reshape(1, D_RNN), 'g_hy_out': g_hy_out[l].reshape(D_HY, 1),
            'w_out_r': w_out[l, :D_RNN].astype(BF16), 'w_out_h': w_out[l, D_RNN:].astype(BF16),
            'w_up': w_up[l].astype(BF16), 'ffn_conv_w': ffn_conv_w[l], 'ffn_conv_b': ffn_conv_b[l].reshape(1, D_FF),
            'w_down': w_down[l].astype(BF16),
        }
        fargs = (hf_w1[l], hf_b1[l], hf_w2[l], hf_b2[l], hf_w3[l], hf_freq[l])
        kf_ctx = _hyena_filters(l_ctx, *fargs, tab_ctx)
        kf_lat = _hyena_filters(l_lat, *fargs, tab_lat)
        final = l == depth - 1
        xp, st = _trunk_layer(xp, mods[l, 0:1], zero_h, p, kf_ctx, tab_ctx, False, final, gf)
        new_states.append(st)
        xs, _ = _trunk_layer(xs, mods[l, 1:1 + nb_lat], state_rglru[:, l], p, kf_lat, tab_lat, True, final, gf)
    return (xp, xs, jnp.stack(new_states, axis=1))
```

```python
import functools
import math

import jax
import jax.numpy as jnp
import ml_dtypes
import numpy as np
from jax import lax
from jax.experimental import pallas as pl
from jax.experimental.pallas import tpu as pltpu

F32 = jnp.float32
BF16 = jnp.bfloat16

D_MODEL = 1024
D_RNN = 512
D_HY = 512
N_HEADS = 8
HEAD_DIM = D_RNN // N_HEADS
RG_C = 8.0
GRID_W = 64
HY_BANDS = 16
HY_FH = 64
D_FF = 2816
N_MOD = 6
EPS = 1e-6

SUBLANES = 8
LANES = 128
VMEM_LIMIT = 56 * 1024 * 1024

DFT2 = LANES
FFT_PASSES = 2


def _cparams(sem):
    return pltpu.CompilerParams(dimension_semantics=sem, vmem_limit_bytes=VMEM_LIMIT)


def _rms(x):
    return x * lax.rsqrt(jnp.mean(x * x, axis=-1, keepdims=True) + EPS)


def _mod_kernel(c_ref, w_ref, b_ref, o_ref):
    c = c_ref[...]
    s = c * jax.nn.sigmoid(c)
    o_ref[0] = jnp.dot(s, w_ref[0], precision=lax.Precision.HIGHEST, preferred_element_type=F32) + b_ref[0]


def _modulation(cc, w_ada, b_ada):
    depth, _, n = w_ada.shape
    tn = 1536
    return pl.pallas_call(
        _mod_kernel,
        grid=(depth, n // tn),
        in_specs=[pl.BlockSpec((SUBLANES, D_MODEL), lambda l, j: (0, 0)),
                  pl.BlockSpec((1, D_MODEL, tn), lambda l, j: (l, 0, j)),
                  pl.BlockSpec((1, 1, tn), lambda l, j: (l, 0, j))],
        out_specs=pl.BlockSpec((1, SUBLANES, tn), lambda l, j: (l, 0, j)),
        out_shape=jax.ShapeDtypeStruct((depth, SUBLANES, n), F32),
        compiler_params=_cparams(("parallel", "parallel")),
        name="adaln_mod",
    )(cc, w_ada, b_ada.reshape(depth, 1, n))


def _in_proj_kernel(x_ref, mod_ref, g_ref, wxy_ref, whyt_ref, xy_ref, hyt_ref):
    x = x_ref[0]
    xn = _rms(x) * g_ref[...]
    xn = (xn * (1.0 + mod_ref[0, 1:2, :]) + mod_ref[0, 0:1, :]).astype(BF16)
    xy_ref[0] = jnp.dot(xn, wxy_ref[...], preferred_element_type=F32)
    hyt_ref[0] = lax.dot_general(whyt_ref[...], xn, (((1,), (1,)), ((), ())), preferred_element_type=F32)


def _in_proj(x, mod, g, wxy, whyt, tm):
    b, l, _ = x.shape
    per_batch_mod = mod.shape[0] > 1
    mod_map = (lambda i, j: (i, 0, 0)) if per_batch_mod else (lambda i, j: (0, 0, 0))
    return pl.pallas_call(
        _in_proj_kernel,
        grid=(b, l // tm),
        in_specs=[pl.BlockSpec((1, tm, D_MODEL), lambda i, j: (i, j, 0)),
                  pl.BlockSpec((1, N_MOD, D_MODEL), mod_map),
                  pl.BlockSpec((1, D_MODEL), lambda i, j: (0, 0)),
                  pl.BlockSpec((D_MODEL, 2 * D_RNN), lambda i, j: (0, 0)),
                  pl.BlockSpec((3 * D_HY, D_MODEL), lambda i, j: (0, 0))],
        out_specs=[pl.BlockSpec((1, tm, 2 * D_RNN), lambda i, j: (i, j, 0)),
                   pl.BlockSpec((1, 3 * D_HY, tm), lambda i, j: (i, 0, j))],
        out_shape=[jax.ShapeDtypeStruct((b, l, 2 * D_RNN), F32),
                   jax.ShapeDtypeStruct((b, 3 * D_HY, l), F32)],
        compiler_params=_cparams(("parallel", "parallel")),
        name="in_proj",
    )(x, mod, g, wxy, whyt)


RG_HALF = D_RNN // 2
RG_TILES = RG_HALF // LANES


def _rglru_kernel(xr_ref, yr_ref, cw_ref, cb_ref, wg_ref, bg_ref, ap_ref, h0_ref, o_ref, st_ref,
                  ext, xc_s, hf, hb, a_s, b_s, *, seq, tc):
    nchunks = seq // tc
    nblk = tc // SUBLANES
    ext[0:SUBLANES, :] = jnp.zeros((SUBLANES, RG_HALF), F32)
    ext[SUBLANES:SUBLANES + seq, :] = xr_ref[0]
    ext[SUBLANES + seq:2 * SUBLANES + seq, :] = jnp.zeros((SUBLANES, RG_HALF), F32)
    for c in range(nchunks):
        xc = cb_ref[...] + cw_ref[0:1, :] * ext[pl.ds(c * tc + SUBLANES - 2, tc), :]
        for k in range(1, 4):
            xc = xc + cw_ref[k:k + 1, :] * ext[pl.ds(c * tc + SUBLANES - 2 + k, tc), :]
        xc_s[pl.ds(c * tc, tc), :] = xc

    row = lax.broadcasted_iota(jnp.int32, (SUBLANES, LANES), 0)
    neg_c_sp = [-RG_C * jax.nn.softplus(-ap_ref[d, 0]) for d in range(2)]

    def gates(c0, d):
        xc = xc_s[pl.ds(c0, tc), :]
        g = jnp.dot(xc.astype(BF16), wg_ref[d, 0], preferred_element_type=F32) + bg_ref[d, 0]
        r = 0.5 * jnp.tanh(0.5 * g[:, :RG_HALF]) + 0.5
        i = 0.5 * jnp.tanh(0.5 * g[:, RG_HALF:]) + 0.5
        log_a = neg_c_sp[d] * r
        a = jnp.exp(log_a)
        mult = jnp.sqrt(jnp.tanh(-log_a) * (a * a + 1.0))
        a_s[d] = a
        b_s[d] = xc * i * mult

    def local_scan(a, b, reverse):
        for s in (1, 2, 4):
            if reverse:
                keep = row < SUBLANES - s
                shift = SUBLANES - s
            else:
                keep = row >= s
                shift = s
            a_sh = jnp.where(keep, pltpu.roll(a, shift, axis=0), 1.0)
            b_sh = jnp.where(keep, pltpu.roll(b, shift, axis=0), 0.0)
            b = a * b_sh + b
            a = a * a_sh
        return a, b

    def block_body(cf0, cb0):
        def body(j, carry):
            rf = pl.multiple_of(j * SUBLANES, SUBLANES)
            rb = pl.multiple_of(tc - SUBLANES - j * SUBLANES, SUBLANES)
            out = []
            for t in range(RG_TILES):
                lanes = slice(t * LANES, (t + 1) * LANES)
                pa, pb = local_scan(a_s[0, pl.ds(rf, SUBLANES), lanes], b_s[0, pl.ds(rf, SUBLANES), lanes], False)
                h = pa * carry[2 * t] + pb
                hf[pl.ds(cf0 + rf, SUBLANES), lanes] = h
                out.append(jnp.broadcast_to(h[SUBLANES - 1:SUBLANES, :], (SUBLANES, LANES)))
                pa, pb = local_scan(a_s[1, pl.ds(rb, SUBLANES), lanes], b_s[1, pl.ds(rb, SUBLANES), lanes], True)
                h = pa * carry[2 * t + 1] + pb
                hb[pl.ds(cb0 + rb, SUBLANES), lanes] = h
                out.append(jnp.broadcast_to(h[0:1, :], (SUBLANES, LANES)))
            return tuple(out)
        return body

    carry = []
    for t in range(RG_TILES):
        carry.append(jnp.broadcast_to(h0_ref[0, 0:1, t * LANES:(t + 1) * LANES], (SUBLANES, LANES)))
        carry.append(jnp.broadcast_to(h0_ref[0, 1:2, t * LANES:(t + 1) * LANES], (SUBLANES, LANES)))
    carry = tuple(carry)
    for c in range(nchunks):
        cf0 = c * tc
        cb0 = (nchunks - 1 - c) * tc
        gates(cf0, 0)
        gates(cb0, 1)
        carry = lax.fori_loop(0, nblk, block_body(cf0, cb0), carry, unroll=2)
    for t in range(RG_TILES):
        st_ref[0, 0:1, t * LANES:(t + 1) * LANES] = carry[2 * t][0:1, :]
        st_ref[0, 1:2, t * LANES:(t + 1) * LANES] = carry[2 * t + 1][0:1, :]
    for c in range(nchunks):
        rows = pl.ds(c * tc, tc)
        o_ref[0, rows, :] = (hf[rows, :] + hb[rows, :]) * jax.nn.gelu(yr_ref[0, rows, :])


def _rglru(xy, cw, cb, wg, bg, ap, h0):
    b, seq, _ = xy.shape
    tc = min(512, seq)
    nh = D_RNN // RG_HALF
    kern = functools.partial(_rglru_kernel, seq=seq, tc=tc)
    return pl.pallas_call(
        kern,
        grid=(b, nh),
        in_specs=[pl.BlockSpec((1, seq, RG_HALF), lambda i, h: (i, 0, h)),
                  pl.BlockSpec((1, seq, RG_HALF), lambda i, h: (i, 0, nh + h)),
                  pl.BlockSpec((4, RG_HALF), lambda i, h: (0, h)),
                  pl.BlockSpec((1, RG_HALF), lambda i, h: (0, h)),
                  pl.BlockSpec((2, 1, RG_HALF, 2 * RG_HALF), lambda i, h: (0, h, 0, 0)),
                  pl.BlockSpec((2, 1, 1, 2 * RG_HALF), lambda i, h: (0, h, 0, 0)),
                  pl.BlockSpec((2, 1, 1, RG_HALF), lambda i, h: (0, h, 0, 0)),
                  pl.BlockSpec((1, 2, RG_HALF), lambda i, h: (i, 0, h))],
        out_specs=[pl.BlockSpec((1, seq, RG_HALF), lambda i, h: (i, 0, h)),
                   pl.BlockSpec((1, 2, RG_HALF), lambda i, h: (i, 0, h))],
        out_shape=[jax.ShapeDtypeStruct((b, seq, D_RNN), F32),
                   jax.ShapeDtypeStruct((b, 2, D_RNN), F32)],
        scratch_shapes=[pltpu.VMEM((seq + 2 * SUBLANES, RG_HALF), F32),
                        pltpu.VMEM((seq, RG_HALF), F32),
                        pltpu.VMEM((seq, RG_HALF), F32),
                        pltpu.VMEM((seq, RG_HALF), F32),
                        pltpu.VMEM((2, tc, RG_HALF), F32),
                        pltpu.VMEM((2, tc, RG_HALF), F32)],
        compiler_params=_cparams(("parallel", "parallel")),
        name="rglru",
    )(xy, xy, cw, cb, wg, bg, ap, h0)


def _gate_weights(gate_w, gate_b, a_param):
    nh = D_RNN // RG_HALF
    hp = N_HEADS // nh
    w = gate_w.reshape(2, 2, nh, hp, HEAD_DIM, HEAD_DIM)
    eye = jnp.eye(hp, dtype=F32)
    dense = jnp.einsum('dghpio,pq->dhpigqo', w, eye)
    dense = dense.reshape(2, nh, RG_HALF, 2 * RG_HALF).astype(BF16)
    bias = gate_b.reshape(2, 2, nh, RG_HALF).transpose(0, 2, 1, 3).reshape(2, nh, 1, 2 * RG_HALF)
    ap = a_param.reshape(2, nh, 1, RG_HALF)
    return dense, bias, ap


def _add(a, b):
    if a is None:
        return b
    if b is None:
        return a
    return a + b


def _sub(a, b):
    if b is None:
        return a
    if a is None:
        return -b
    return a - b


def _scale(a, s):
    if a is None or s == 0.0:
        return None
    if s == 1.0:
        return a
    if s == -1.0:
        return -a
    return a * s


def _cmul_const(z, wr, wi):
    if z is None:
        return None
    re, im = z
    if abs(wr) < 1e-15:
        wr = 0.0
    if abs(wi) < 1e-15:
        wi = 0.0
    return (_sub(_scale(re, wr), _scale(im, wi)), _add(_scale(re, wi), _scale(im, wr)))


def _cadd(a, b):
    if a is None:
        return b
    if b is None:
        return a
    return (_add(a[0], b[0]), _add(a[1], b[1]))


def _csub(a, b):
    if b is None:
        return a
    if a is None:
        return (_sub(None, b[0]), _sub(None, b[1]))
    return (_sub(a[0], b[0]), _sub(a[1], b[1]))


def _cfft(xs, sign):
    n = len(xs)
    if n == 1:
        return list(xs)
    even = _cfft(xs[0::2], sign)
    odd = _cfft(xs[1::2], sign)
    out = [None] * n
    for k in range(n // 2):
        ang = sign * 2.0 * math.pi * k / n
        t = _cmul_const(odd[k], math.cos(ang), math.sin(ang))
        out[k] = _cadd(even[k], t)
        out[k + n // 2] = _csub(even[k], t)
    return out


def _dft_tables(n1, rc):
    n = n1 * DFT2
    k = np.arange(DFT2, dtype=np.float64)
    ang = 2.0 * np.pi * np.outer(k, k) / DFT2
    c, s = np.cos(ang), np.sin(ang)
    fwd = np.block([[c, -s], [s, c]])
    inv = np.block([[c, s], [-s, c]]) / n

    def split(m):
        hi = m.astype(np.float32).astype(ml_dtypes.bfloat16)
        lo = (m - hi.astype(np.float64)).astype(ml_dtypes.bfloat16)
        return jnp.asarray(hi), jnp.asarray(lo)

    tw_ang = 2.0 * np.pi * np.outer(np.arange(n1, dtype=np.float64), k) / n
    tw = np.concatenate([np.cos(tw_ang), -np.sin(tw_ang)], axis=1)
    tw = np.repeat(tw, rc, axis=0).astype(np.float32)
    return split(fwd) + split(inv) + (jnp.asarray(tw),)


HY_CB = 32


def _fft_rows(n1, cb):
    return SUBLANES if n1 >= 16 else cb


def _dft_mm(x, hi_ref, lo_ref):
    xh = x.astype(BF16)
    acc = jnp.dot(xh, hi_ref[...], preferred_element_type=F32)
    if FFT_PASSES >= 2:
        acc = acc + jnp.dot(xh, lo_ref[...], preferred_element_type=F32)
    if FFT_PASSES >= 3:
        xl = (x - xh.astype(F32)).astype(BF16)
        acc = acc + jnp.dot(xl, hi_ref[...], preferred_element_type=F32)
    return acc


def _fwd_lane_fft(blocks, tw_ref, n1, rc):
    xs = _cfft(blocks, -1)
    out = []
    for k1 in range(n1):
        z = xs[k1]
        if k1 > 0 and z is not None:
            twr = tw_ref[k1 * rc:(k1 + 1) * rc, 0:LANES]
            twi = tw_ref[k1 * rc:(k1 + 1) * rc, LANES:2 * LANES]
            re, im = z
            if im is None:
                z = (re * twr, re * twi)
            else:
                z = (re * twr - im * twi, re * twi + im * twr)
        out.append(z)
    return out


def _inv_lane_fft(blocks, tw_ref, n1, rc):
    gs = []
    for k1 in range(n1):
        re, im = blocks[k1]
        if k1 > 0:
            twr = tw_ref[k1 * rc:(k1 + 1) * rc, 0:LANES]
            twi = tw_ref[k1 * rc:(k1 + 1) * rc, LANES:2 * LANES]
            re, im = re * twr + im * twi, im * twr - re * twi
        gs.append((re, im))
    return _cfft(gs, +1)


def _filter_kernel(feat_ref, w1_ref, b1_ref, w2_ref, b2_ref, fr_ref, w3_ref, fh_ref, fl_ref, tw_ref, kf_ref,
                   h2_s, k_s, s_s, *, seq, n1, cb, rc):
    n = 2 * seq
    hi = lax.Precision.HIGHEST

    @pl.when(pl.program_id(0) == 0)
    def _():
        h = jnp.dot(w1_ref[...], feat_ref[...], precision=hi, preferred_element_type=F32) + b1_ref[...]
        h = jnp.sin(fr_ref[:, 0:1] * h)
        h = jnp.dot(w2_ref[...], h, precision=hi, preferred_element_type=F32) + b2_ref[...]
        h2_s[...] = jnp.sin(fr_ref[:, 1:2] * h)

    lane = lax.broadcasted_iota(jnp.int32, (cb, n), 1)
    pos = jnp.where(lane < seq, lane, n - lane).astype(F32)
    t = pos / float(max(seq - 1, 1))
    d_idx = (lax.broadcasted_iota(jnp.int32, (cb, n), 0) + pl.program_id(0) * cb).astype(F32)
    min_decay = math.log(1e-2) / 1.5
    max_decay = math.log(1e-2) / 0.3
    delta = jnp.abs(min_decay + d_idx * ((max_decay - min_decay) / (D_HY - 1)))
    decay = jnp.exp(-t * delta)
    hfb = jnp.dot(w3_ref[...].reshape(4 * cb, HY_FH), h2_s[...], precision=hi, preferred_element_type=F32)
    for o in range(2):
        hf = hfb[(2 * o) * cb:(2 * o + 1) * cb]
        hb = hfb[(2 * o + 1) * cb:(2 * o + 2) * cb]
        k_s[o] = jnp.where(lane < seq, hf, jnp.where(lane == seq, 0.0, hb)) * decay

    nrc = cb // rc
    for o in range(2):
        def rows(i, _, o=o):
            r0 = pl.multiple_of(i * rc, rc)
            blocks = [(k_s[o, pl.ds(r0, rc), j * LANES:(j + 1) * LANES], None) for j in range(n1)]
            zs = _fwd_lane_fft(blocks, tw_ref, n1, rc)
            for k1 in range(n1):
                re, im = zs[k1]
                s_s[pl.ds(k1 * cb + r0, rc), 0:LANES] = re
                s_s[pl.ds(k1 * cb + r0, rc), LANES:2 * LANES] = im if im is not None else jnp.zeros_like(re)
            return 0
        lax.fori_loop(0, nrc, rows, 0)
        x = s_s[...]
        xh = x.astype(BF16)
        xl = (x - xh.astype(F32)).astype(BF16)
        z = (jnp.dot(xh, fh_ref[...], preferred_element_type=F32) + jnp.dot(xh, fl_ref[...], preferred_element_type=F32)
             + jnp.dot(xl, fh_ref[...], preferred_element_type=F32))
        for i in range(nrc):
            for k1 in range(n1):
                kf_ref[o, i, k1 * rc:(k1 + 1) * rc, :] = z[k1 * cb + i * rc:k1 * cb + (i + 1) * rc, :]


def _hyena_filters(seq, w1, b1, w2, b2, w3, freq, tables):
    n1 = 2 * seq // DFT2
    n = 2 * seq
    cb = HY_CB
    rc = _fft_rows(n1, cb)
    fh, fl, _, _, tw = tables
    pos = np.arange(n, dtype=np.float64)
    pos = np.where(pos < seq, pos, n - pos)
    tt = pos / max(seq - 1, 1)
    omega = 2.0 * math.pi * pos / seq
    bands = np.linspace(1e-4, HY_BANDS - 1, HY_BANDS)
    ang = omega[None, :] * bands[:, None]
    feats = np.concatenate([tt[None, :], np.cos(ang), np.sin(ang)], axis=0)
    nfeat = LANES
    feats = np.pad(feats, ((0, nfeat - feats.shape[0]), (0, 0))).astype(np.float32)
    w1 = jnp.pad(w1, ((0, nfeat - w1.shape[0]), (0, 0)))
    w3t = w3.T.reshape(4, D_HY, HY_FH)
    kern = functools.partial(_filter_kernel, seq=seq, n1=n1, cb=cb, rc=rc)
    full = lambda *shape: pl.BlockSpec(shape, lambda i: (0,) * len(shape))
    return pl.pallas_call(
        kern,
        grid=(D_HY // cb,),
        in_specs=[full(nfeat, n), full(HY_FH, nfeat), full(HY_FH, 1), full(HY_FH, HY_FH), full(HY_FH, 1),
                  full(HY_FH, 2),
                  pl.BlockSpec((4, cb, HY_FH), lambda i: (0, i, 0)),
                  full(2 * DFT2, 2 * DFT2), full(2 * DFT2, 2 * DFT2), full(n1 * rc, 2 * DFT2)],
        out_specs=pl.BlockSpec((2, cb // rc, n1 * rc, 2 * DFT2), lambda i: (0, i, 0, 0)),
        out_shape=jax.ShapeDtypeStruct((2, D_HY // rc, n1 * rc, 2 * DFT2), F32),
        scratch_shapes=[pltpu.VMEM((HY_FH, n), F32), pltpu.VMEM((2, cb, n), F32),
                        pltpu.VMEM((n1 * cb, 2 * DFT2), F32)],
        compiler_params=_cparams(("arbitrary",)),
        name="hyena_filters",
    )(jnp.asarray(feats), w1.T, b1.reshape(HY_FH, 1), w2.T, b2.reshape(HY_FH, 1), freq.T, w3t, fh, fl, tw)


HY_UNIT = 512


def _hyena_kernel(hy_ref, cw_ref, cb_ref, bias_ref, kf_ref, fh_ref, fl_ref, ih_ref, il_ref, tw_ref, o_ref,
                  buf, sa, sb, *, seq, n1, cb, nb, rc, pu):
    npairs = nb // 2
    nin = n1 // 2
    nrc = cb // rc
    upo = (npairs // pu) * nrc
    nitems = 2 * upo
    assert upo >= 3, "an item's second-order stage A must come after its first-order stage C"
    lane = lax.broadcasted_iota(jnp.int32, (rc, seq), 1)

    def conv3(h, part, r0):
        w0 = cw_ref[0, part, pl.ds(r0, rc), :]
        w1 = cw_ref[1, part, pl.ds(r0, rc), :]
        w2 = cw_ref[2, part, pl.ds(r0, rc), :]
        bb = cb_ref[part, pl.ds(r0, rc), :]
        hm = jnp.where(lane == 0, 0.0, pltpu.roll(h, 1, axis=1))
        hp = jnp.where(lane == seq - 1, 0.0, pltpu.roll(h, seq - 1, axis=1))
        return w0 * hm + w1 * h + w2 * hp + bb

    def prep(i, _):
        b = i // nrc
        r0 = pl.multiple_of((i % nrc) * rc, rc)
        buf[0, b, pl.ds(r0, rc), :] = conv3(hy_ref[b, 0, pl.ds(r0, rc), :], 0, r0)
        return 0
    lax.fori_loop(0, nb * nrc, prep, 0, unroll=2)

    def item(j):
        o = j // upo
        u = j % upo
        return o, u // nrc, u % nrc

    def stage_a(j, slot):
        o, pg, rg = item(j)
        r0 = pl.multiple_of(rg * rc, rc)
        for pp in range(pu):
            p = pg * pu + pp
            blocks = [(buf[o, 2 * p, pl.ds(r0, rc), jb * LANES:(jb + 1) * LANES],
                       buf[o, 2 * p + 1, pl.ds(r0, rc), jb * LANES:(jb + 1) * LANES]) if jb < nin else None
                      for jb in range(n1)]
            zs = _fwd_lane_fft(blocks, tw_ref, n1, rc)
            for k1 in range(n1):
                rows = pl.ds((pp * n1 + k1) * rc, rc)
                sa[slot, rows, 0:LANES] = zs[k1][0]
                sa[slot, rows, LANES:2 * LANES] = zs[k1][1]

    def stage_b(j, slot):
        o, _, rg = item(j)
        z = _dft_mm(sa[slot], fh_ref, fl_ref)
        kk = kf_ref[o, rg]
        if pu > 1:
            kk = jnp.concatenate([kk] * pu, axis=0)
        zr, zi = z[:, :LANES], z[:, LANES:]
        kr, ki = kk[:, :LANES], kk[:, LANES:]
        w = jnp.concatenate([zr * kr - zi * ki, zr * ki + zi * kr], axis=1)
        sb[slot] = _dft_mm(w, ih_ref, il_ref)

    def stage_c(j, slot):
        o, pg, rg = item(j)
        r0 = pl.multiple_of(rg * rc, rc)
        bias = bias_ref[o, pl.ds(r0, rc), :]
        for pp in range(pu):
            p = pg * pu + pp
            blocks = [(sb[slot, pl.ds((pp * n1 + k1) * rc, rc), 0:LANES],
                       sb[slot, pl.ds((pp * n1 + k1) * rc, rc), LANES:2 * LANES]) for k1 in range(n1)]
            ys = _inv_lane_fft(blocks, tw_ref, n1, rc)
            for q in range(2):
                xg = conv3(hy_ref[2 * p + q, 1 + o, pl.ds(r0, rc), :], 1 + o, r0)
                for jb in range(nin):
                    lanes = slice(jb * LANES, (jb + 1) * LANES)
                    u = buf[o, 2 * p + q, pl.ds(r0, rc), lanes]
                    buf[o + 1, 2 * p + q, pl.ds(r0, rc), lanes] = xg[:, lanes] * (ys[jb][q] + u * bias)

    stage_a(0, 0)
    stage_a(1, 1)
    stage_b(0, 0)

    assert nitems % 2 == 0

    def steady(i, _):
        t = 2 + 2 * i
        stage_b(t - 1, 1)
        stage_a(t, 0)
        stage_c(t - 2, 0)
        stage_b(t, 0)
        stage_a(t + 1, 1)
        stage_c(t - 1, 1)
        return 0
    lax.fori_loop(0, (nitems - 2) // 2, steady, 0)
    stage_b(nitems - 1, (nitems - 1) % 2)
    stage_c(nitems - 2, nitems % 2)
    stage_c(nitems - 1, (nitems - 1) % 2)
    for b in range(nb):
        o_ref[b] = buf[2, b]


def _hyena(hyt, cw, cbias, bias, kf, tables):
    nb, _, seq = hyt.shape
    n1 = 2 * seq // DFT2
    cb = HY_CB
    rc = _fft_rows(n1, cb)
    pu = HY_UNIT // (n1 * rc)
    nblk = D_HY // cb
    fh, fl, ih, il, tw = tables
    kern = functools.partial(_hyena_kernel, seq=seq, n1=n1, cb=cb, nb=nb, rc=rc, pu=pu)
    full = lambda *shape: pl.BlockSpec(shape, lambda i: (0,) * len(shape))
    return pl.pallas_call(
        kern,
        grid=(nblk,),
        in_specs=[pl.BlockSpec((nb, 3, cb, seq), lambda i: (0, 0, i, 0)),
                  pl.BlockSpec((3, 3, cb, 1), lambda i: (0, 0, i, 0)),
                  pl.BlockSpec((3, cb, 1), lambda i: (0, i, 0)),
                  pl.BlockSpec((2, cb, 1), lambda i: (0, i, 0)),
                  pl.BlockSpec((2, cb // rc, n1 * rc, 2 * DFT2), lambda i: (0, i, 0, 0)),
                  full(2 * DFT2, 2 * DFT2), full(2 * DFT2, 2 * DFT2),
                  full(2 * DFT2, 2 * DFT2), full(2 * DFT2, 2 * DFT2),
                  full(n1 * rc, 2 * DFT2)],
        out_specs=pl.BlockSpec((nb, cb, seq), lambda i: (0, i, 0)),
        out_shape=jax.ShapeDtypeStruct((nb, D_HY, seq), F32),
        scratch_shapes=[pltpu.VMEM((3, nb, cb, seq), F32),
                        pltpu.VMEM((2, HY_UNIT, 2 * DFT2), F32), pltpu.VMEM((2, HY_UNIT, 2 * DFT2), F32)],
        compiler_params=_cparams(("parallel",)),
        name="hyena",
    )(hyt.reshape(nb, 3, D_HY, seq), cw, cbias, bias, kf, fh, fl, ih, il, tw)


def _out_proj_kernel(x_ref, or_ref, oht_ref, mod_ref, gr_ref, gh_ref, wr_ref, wh_ref, g2_ref, x1_ref, xn_ref):
    orn = (_rms(or_ref[0]) * gr_ref[...]).astype(BF16)
    oh = oht_ref[0]
    ohn = oh * lax.rsqrt(jnp.mean(oh * oh, axis=0, keepdims=True) + EPS) * gh_ref[...]
    o = jnp.dot(orn, wr_ref[...], preferred_element_type=F32)
    o = o + lax.dot_general(ohn.astype(BF16), wh_ref[...], (((0,), (0,)), ((), ())), preferred_element_type=F32)
    x1 = x_ref[0] + mod_ref[0, 2:3, :] * o
    x1_ref[0] = x1
    xn = _rms(x1) * g2_ref[...]
    xn_ref[0] = (xn * (1.0 + mod_ref[0, 4:5, :]) + mod_ref[0, 3:4, :]).astype(BF16)


def _out_proj(x, o_r, oht, mod, gr, gh, wr, wh, g2, tm):
    b, l, _ = x.shape
    mod_map = (lambda i, j: (i, 0, 0)) if mod.shape[0] > 1 else (lambda i, j: (0, 0, 0))
    return pl.pallas_call(
        _out_proj_kernel,
        grid=(b, l // tm),
        in_specs=[pl.BlockSpec((1, tm, D_MODEL), lambda i, j: (i, j, 0)),
                  pl.BlockSpec((1, tm, D_RNN), lambda i, j: (i, j, 0)),
                  pl.BlockSpec((1, D_HY, tm), lambda i, j: (i, 0, j)),
                  pl.BlockSpec((1, N_MOD, D_MODEL), mod_map),
                  pl.BlockSpec((1, D_RNN), lambda i, j: (0, 0)),
                  pl.BlockSpec((D_HY, 1), lambda i, j: (0, 0)),
                  pl.BlockSpec((D_RNN, D_MODEL), lambda i, j: (0, 0)),
                  pl.BlockSpec((D_HY, D_MODEL), lambda i, j: (0, 0)),
                  pl.BlockSpec((1, D_MODEL), lambda i, j: (0, 0))],
        out_specs=[pl.BlockSpec((1, tm, D_MODEL), lambda i, j: (i, j, 0)),
                   pl.BlockSpec((1, tm, D_MODEL), lambda i, j: (i, j, 0))],
        out_shape=[jax.ShapeDtypeStruct((b, l, D_MODEL), F32),
                   jax.ShapeDtypeStruct((b, l, D_MODEL), BF16)],
        compiler_params=_cparams(("parallel", "parallel")),
        name="out_proj",
    )(x, o_r, oht, mod, gr, gh, wr, wh, g2)


FF_TM = 512
FF_SUB = 2 * LANES


def _ffn_kernel(x1_ref, xn_ref, xp_ref, xq_ref, mod_ref, wu_ref, cw_ref, cb_ref, wd_ref, gf_ref, o_ref, xe_s, h_s,
                *, seg, halo, on_grid, tiles_per_seq, final_norm):
    t = pl.program_id(0)
    tm = x1_ref.shape[1]
    rows = tm + 2 * halo
    if halo:
        first = (t % tiles_per_seq) == 0
        last = (t % tiles_per_seq) == tiles_per_seq - 1
        xe_s[0:halo, :] = jnp.where(first, jnp.zeros_like(xp_ref[0]), xp_ref[0])
        xe_s[halo + tm:rows, :] = jnp.where(last, jnp.zeros_like(xq_ref[0]), xq_ref[0])
    xe_s[halo:halo + tm, :] = xn_ref[0]

    pos = lax.broadcasted_iota(jnp.int32, (rows, 1), 0) % seg
    for c0 in range(0, D_FF, FF_SUB):
        w = min(FF_SUB, D_FF - c0)
        g = jnp.dot(xe_s[...], wu_ref[:, D_FF + c0:D_FF + c0 + w], preferred_element_type=F32)
        a = jnp.dot(xe_s[halo:halo + tm, :], wu_ref[:, c0:c0 + w], preferred_element_type=F32)
        gls = pltpu.roll(jnp.where(pos == seg - 1, 0.0, g), 1, axis=0)
        grs = pltpu.roll(jnp.where(pos == 0, 0.0, g), rows - 1, axis=0)
        acc = cb_ref[:, c0:c0 + w]
        for dr in ((-1, 0, 1) if on_grid else (0,)):
            lo = halo + dr * seg
            acc = (acc + cw_ref[dr + 1, 0:1, c0:c0 + w] * gls[lo:lo + tm]
                   + cw_ref[dr + 1, 1:2, c0:c0 + w] * g[lo:lo + tm]
                   + cw_ref[dr + 1, 2:3, c0:c0 + w] * grs[lo:lo + tm])
        h_s[:, c0:c0 + w] = (jax.nn.gelu(acc) * a).astype(BF16)
    y = jnp.dot(h_s[...], wd_ref[...], preferred_element_type=F32)
    x2 = x1_ref[0] + mod_ref[0, 5:6, :] * y
    if final_norm:
        x2 = _rms(x2) * gf_ref[...]
    o_ref[0] = x2


def _ffn(x1, xn2, mod, wu, cw, cb, wd, gf, on_grid, final_norm):
    b, l, _ = x1.shape
    tm = FF_TM
    if on_grid:
        seg, halo = GRID_W, GRID_W
        tps = l // tm
    else:
        seg, halo = l, 0
        tps = 1
    nt = b * l // tm
    hb = GRID_W
    nhb = tm // hb
    last_hb = b * l // hb - 1
    mod_map = (lambda t: (t // tps, 0, 0)) if mod.shape[0] > 1 else (lambda t: (0, 0, 0))
    kern = functools.partial(_ffn_kernel, seg=seg, halo=halo, on_grid=on_grid, tiles_per_seq=tps, final_norm=final_norm)
    const = lambda *shape: pl.BlockSpec(shape, lambda t: (0,) * len(shape), pipeline_mode=pl.Buffered(1))
    xh = xn2.reshape(b * l // hb, hb, D_MODEL)
    out = pl.pallas_call(
        kern,
        grid=(nt,),
        in_specs=[pl.BlockSpec((1, tm, D_MODEL), lambda t: (t, 0, 0)),
                  pl.BlockSpec((1, tm, D_MODEL), lambda t: (t, 0, 0)),
                  pl.BlockSpec((1, hb, D_MODEL), lambda t: (jnp.maximum(t * nhb - 1, 0), 0, 0)),
                  pl.BlockSpec((1, hb, D_MODEL), lambda t: (jnp.minimum((t + 1) * nhb, last_hb), 0, 0)),
                  pl.BlockSpec((1, N_MOD, D_MODEL), mod_map),
                  const(D_MODEL, 2 * D_FF), const(3, 3, D_FF), const(1, D_FF), const(D_FF, D_MODEL), const(1, D_MODEL)],
        out_specs=pl.BlockSpec((1, tm, D_MODEL), lambda t: (t, 0, 0)),
        out_shape=jax.ShapeDtypeStruct((nt, tm, D_MODEL), F32),
        scratch_shapes=[pltpu.VMEM((tm + 2 * halo, D_MODEL), BF16), pltpu.VMEM((tm, D_FF), BF16)],
        compiler_params=_cparams(("parallel",)),
        name="ffn",
    )(x1.reshape(nt, tm, D_MODEL), xn2.reshape(nt, tm, D_MODEL), xh, xh, mod, wu, cw, cb, wd, gf)
    return out.reshape(b, l, D_MODEL)


def _trunk_layer(x, mod, h0, p, kf, tables, on_grid, final_norm, g_final):
    b, l, _ = x.shape
    tm = min(512, l)
    xy, hyt = _in_proj(x, mod, p['g_norm1'], p['wxy'], p['whyt'], tm)
    o_r, states = _rglru(xy, p['rg_conv_w'], p['rg_conv_b'], p['wg'], p['bg'], p['ap'], h0)
    oht = _hyena(hyt, p['hy_cw'], p['hy_cb'], p['hy_bias'], kf, tables)
    x1, xn2 = _out_proj(x, o_r, oht, mod, p['g_rnn_out'], p['g_hy_out'], p['w_out_r'], p['w_out_h'], p['g_norm2'], tm)
    x2 = _ffn(x1, xn2, mod, p['w_up'], p['ffn_conv_w'], p['ffn_conv_b'], p['w_down'], g_final, on_grid, final_norm)
    return x2, states


def kernel(x_prompt, x_sample, state_rglru, c, c_ctx, w_ada, b_ada, g_norm1, g_norm2, w_in, rg_conv_w, rg_conv_b, rg_gate_w, rg_gate_b, rg_a, hy_conv_w, hy_conv_b, hf_w1, hf_b1, hf_w2, hf_b2, hf_w3, hf_freq, hy_bias, g_rnn_out, g_hy_out, w_out, w_up, ffn_conv_w, ffn_conv_b, w_down, g_final):
    depth = w_in.shape[0]
    nb_ctx, l_ctx, _ = x_prompt.shape
    nb_lat, l_lat, _ = x_sample.shape

    cc = jnp.zeros((SUBLANES, D_MODEL), F32).at[0].set(c_ctx).at[1:1 + nb_lat].set(c)
    mods = _modulation(cc, w_ada, b_ada).reshape(depth, SUBLANES, N_MOD, D_MODEL)

    tab_ctx = _dft_tables(2 * l_ctx // DFT2, _fft_rows(2 * l_ctx // DFT2, HY_CB))
    tab_lat = _dft_tables(2 * l_lat // DFT2, _fft_rows(2 * l_lat // DFT2, HY_CB))
    gf = g_final.reshape(1, D_MODEL)
    zero_h = jnp.zeros((nb_ctx, 2, D_RNN), F32)

    xp, xs = x_prompt, x_sample
    new_states = []
    for l in range(depth):
        wg, bg, ap = _gate_weights(rg_gate_w[l], rg_gate_b[l], rg_a[l])
        p = {
            'g_norm1': g_norm1[l].reshape(1, D_MODEL), 'g_norm2': g_norm2[l].reshape(1, D_MODEL),
            'wxy': w_in[l, :, :2 * D_RNN].astype(BF16), 'whyt': w_in[l, :, 2 * D_RNN:].T.astype(BF16),
            'rg_conv_w': rg_conv_w[l], 'rg_conv_b': rg_conv_b[l].reshape(1, D_RNN),
            'wg': wg, 'bg': bg, 'ap': ap,
            'hy_cw': hy_conv_w[l].reshape(3, 3, D_HY, 1), 'hy_cb': hy_conv_b[l].reshape(3, D_HY, 1),
            'hy_bias': hy_bias[l].reshape(2, D_HY, 1),
            'g_rnn_out': g_rnn_out[l].reshape(1, D_RNN), 'g_hy_out': g_hy_out[l].reshape(D_HY, 1),
            'w_out_r': w_out[l, :D_RNN].astype(BF16), 'w_out_h': w_out[l, D_RNN:].astype(BF16),
            'w_up': w_up[l].astype(BF16), 'ffn_conv_w': ffn_conv_w[l], 'ffn_conv_b': ffn_conv_b[l].reshape(1, D_FF),
            'w_down': w_down[l].astype(BF16),
        }
        fargs = (hf_w1[l], hf_b1[l], hf_w2[l], hf_b2[l], hf_w3[l], hf_freq[l])
        kf_ctx = _hyena_filters(l_ctx, *fargs, tab_ctx)
        kf_lat = _hyena_filters(l_lat, *fargs, tab_lat)
        final = l == depth - 1
        xp, st = _trunk_layer(xp, mods[l, 0:1], zero_h, p, kf_ctx, tab_ctx, False, final, gf)
        new_states.append(st)
        xs, _ = _trunk_layer(xs, mods[l, 1:1 + nb_lat], state_rglru[:, l], p, kf_lat, tab_lat, True, final, gf)
    return (xp, xs, jnp.stack(new_states, axis=1))
```

```python
import functools
import math

import jax
import jax.numpy as jnp
import ml_dtypes
import numpy as np
from jax import lax
from jax.experimental import pallas as pl
from jax.experimental.pallas import tpu as pltpu

F32 = jnp.float32
BF16 = jnp.bfloat16

D_MODEL = 1024
D_RNN = 512
D_HY = 512
N_HEADS = 8
HEAD_DIM = D_RNN // N_HEADS
RG_C = 8.0
GRID_W = 64
HY_BANDS = 16
HY_FH = 64
D_FF = 2816
N_MOD = 6
EPS = 1e-6

SUBLANES = 8
LANES = 128
VMEM_LIMIT = 56 * 1024 * 1024

DFT2 = LANES
FFT_PASSES = 2


def _cparams(sem):
    return pltpu.CompilerParams(dimension_semantics=sem, vmem_limit_bytes=VMEM_LIMIT)


def _rms(x):
    return x * lax.rsqrt(jnp.mean(x * x, axis=-1, keepdims=True) + EPS)


def _mod_kernel(c_ref, w_ref, b_ref, o_ref):
    c = c_ref[...]
    s = c * jax.nn.sigmoid(c)
    o_ref[0] = jnp.dot(s, w_ref[0], precision=lax.Precision.HIGHEST, preferred_element_type=F32) + b_ref[0]


def _modulation(cc, w_ada, b_ada):
    depth, _, n = w_ada.shape
    tn = 1536
    return pl.pallas_call(
        _mod_kernel,
        grid=(depth, n // tn),
        in_specs=[pl.BlockSpec((SUBLANES, D_MODEL), lambda l, j: (0, 0)),
                  pl.BlockSpec((1, D_MODEL, tn), lambda l, j: (l, 0, j)),
                  pl.BlockSpec((1, 1, tn), lambda l, j: (l, 0, j))],
        out_specs=pl.BlockSpec((1, SUBLANES, tn), lambda l, j: (l, 0, j)),
        out_shape=jax.ShapeDtypeStruct((depth, SUBLANES, n), F32),
        compiler_params=_cparams(("parallel", "parallel")),
        name="adaln_mod",
    )(cc, w_ada, b_ada.reshape(depth, 1, n))


def _in_proj_kernel(x_ref, mod_ref, g_ref, wxy_ref, whyt_ref, xy_ref, hyt_ref):
    x = x_ref[0]
    xn = _rms(x) * g_ref[...]
    xn = (xn * (1.0 + mod_ref[0, 1:2, :]) + mod_ref[0, 0:1, :]).astype(BF16)
    xy_ref[0] = jnp.dot(xn, wxy_ref[...], preferred_element_type=F32)
    hyt_ref[0] = lax.dot_general(whyt_ref[...], xn, (((1,), (1,)), ((), ())), preferred_element_type=F32)


def _in_proj(x, mod, g, wxy, whyt, tm):
    b, l, _ = x.shape
    per_batch_mod = mod.shape[0] > 1
    mod_map = (lambda i, j: (i, 0, 0)) if per_batch_mod else (lambda i, j: (0, 0, 0))
    return pl.pallas_call(
        _in_proj_kernel,
        grid=(b, l // tm),
        in_specs=[pl.BlockSpec((1, tm, D_MODEL), lambda i, j: (i, j, 0)),
                  pl.BlockSpec((1, N_MOD, D_MODEL), mod_map),
                  pl.BlockSpec((1, D_MODEL), lambda i, j: (0, 0)),
                  pl.BlockSpec((D_MODEL, 2 * D_RNN), lambda i, j: (0, 0)),
                  pl.BlockSpec((3 * D_HY, D_MODEL), lambda i, j: (0, 0))],
        out_specs=[pl.BlockSpec((1, tm, 2 * D_RNN), lambda i, j: (i, j, 0)),
                   pl.BlockSpec((1, 3 * D_HY, tm), lambda i, j: (i, 0, j))],
        out_shape=[jax.ShapeDtypeStruct((b, l, 2 * D_RNN), F32),
                   jax.ShapeDtypeStruct((b, 3 * D_HY, l), F32)],
        compiler_params=_cparams(("parallel", "parallel")),
        name="in_proj",
    )(x, mod, g, wxy, whyt)


RG_HALF = D_RNN // 2
RG_TILES = RG_HALF // LANES


def _rglru_kernel(xr_ref, yr_ref, cw_ref, cb_ref, wg_ref, bg_ref, ap_ref, h0_ref, o_ref, st_ref,
                  ext, xc_s, hf, hb, a_s, b_s, *, seq, tc):
    nchunks = seq // tc
    nblk = tc // SUBLANES
    ext[0:SUBLANES, :] = jnp.zeros((SUBLANES, RG_HALF), F32)
    ext[SUBLANES:SUBLANES + seq, :] = xr_ref[0]
    ext[SUBLANES + seq:2 * SUBLANES + seq, :] = jnp.zeros((SUBLANES, RG_HALF), F32)
    for c in range(nchunks):
        xc = cb_ref[...] + cw_ref[0:1, :] * ext[pl.ds(c * tc + SUBLANES - 2, tc), :]
        for k in range(1, 4):
            xc = xc + cw_ref[k:k + 1, :] * ext[pl.ds(c * tc + SUBLANES - 2 + k, tc), :]
        xc_s[pl.ds(c * tc, tc), :] = xc

    row = lax.broadcasted_iota(jnp.int32, (SUBLANES, LANES), 0)
    half_neg_c_sp = [(-0.5 * RG_C) * jax.nn.softplus(-ap_ref[d, 0]) for d in range(2)]

    def gates(c0, d):
        xc = xc_s[pl.ds(c0, tc), :]
        g = jnp.dot(xc.astype(BF16), wg_ref[d, 0], preferred_element_type=F32) + bg_ref[d, 0]
        i = 0.5 * jnp.tanh(g[:, RG_HALF:]) + 0.5
        log_a = half_neg_c_sp[d] * jnp.tanh(g[:, :RG_HALF]) + half_neg_c_sp[d]
        a = jnp.exp(log_a)
        y = jnp.tanh(-log_a) * (a * a + 1.0)
        mult = jnp.where(y > 0.0, y * lax.rsqrt(y), 0.0)
        a_s[d] = a
        b_s[d] = xc * i * mult

    def local_scan(a, b, reverse):
        for s in (1, 2, 4):
            if reverse:
                keep = row < SUBLANES - s
                shift = SUBLANES - s
            else:
                keep = row >= s
                shift = s
            a_sh = jnp.where(keep, pltpu.roll(a, shift, axis=0), 1.0)
            b_sh = jnp.where(keep, pltpu.roll(b, shift, axis=0), 0.0)
            b = a * b_sh + b
            a = a * a_sh
        return a, b

    def block_body(cf0, cb0):
        def body(j, carry):
            rf = pl.multiple_of(j * SUBLANES, SUBLANES)
            rb = pl.multiple_of(tc - SUBLANES - j * SUBLANES, SUBLANES)
            out = []
            for t in range(RG_TILES):
                lanes = slice(t * LANES, (t + 1) * LANES)
                pa, pb = local_scan(a_s[0, pl.ds(rf, SUBLANES), lanes], b_s[0, pl.ds(rf, SUBLANES), lanes], False)
                h = pa * carry[2 * t] + pb
                hf[pl.ds(cf0 + rf, SUBLANES), lanes] = h
                out.append(jnp.broadcast_to(h[SUBLANES - 1:SUBLANES, :], (SUBLANES, LANES)))
                pa, pb = local_scan(a_s[1, pl.ds(rb, SUBLANES), lanes], b_s[1, pl.ds(rb, SUBLANES), lanes], True)
                h = pa * carry[2 * t + 1] + pb
                hb[pl.ds(cb0 + rb, SUBLANES), lanes] = h
                out.append(jnp.broadcast_to(h[0:1, :], (SUBLANES, LANES)))
            return tuple(out)
        return body

    carry = []
    for t in range(RG_TILES):
        carry.append(jnp.broadcast_to(h0_ref[0, 0:1, t * LANES:(t + 1) * LANES], (SUBLANES, LANES)))
        carry.append(jnp.broadcast_to(h0_ref[0, 1:2, t * LANES:(t + 1) * LANES], (SUBLANES, LANES)))
    carry = tuple(carry)
    for c in range(nchunks):
        cf0 = c * tc
        cb0 = (nchunks - 1 - c) * tc
        gates(cf0, 0)
        gates(cb0, 1)
        carry = lax.fori_loop(0, nblk, block_body(cf0, cb0), carry, unroll=2)
    for t in range(RG_TILES):
        st_ref[0, 0:1, t * LANES:(t + 1) * LANES] = carry[2 * t][0:1, :]
        st_ref[0, 1:2, t * LANES:(t + 1) * LANES] = carry[2 * t + 1][0:1, :]
    for c in range(nchunks):
        rows = pl.ds(c * tc, tc)
        o_ref[0, rows, :] = (hf[rows, :] + hb[rows, :]) * jax.nn.gelu(yr_ref[0, rows, :])


def _rglru(xy, cw, cb, wg, bg, ap, h0):
    b, seq, _ = xy.shape
    tc = min(512, seq)
    nh = D_RNN // RG_HALF
    kern = functools.partial(_rglru_kernel, seq=seq, tc=tc)
    return pl.pallas_call(
        kern,
        grid=(b, nh),
        in_specs=[pl.BlockSpec((1, seq, RG_HALF), lambda i, h: (i, 0, h)),
                  pl.BlockSpec((1, seq, RG_HALF), lambda i, h: (i, 0, nh + h)),
                  pl.BlockSpec((4, RG_HALF), lambda i, h: (0, h)),
                  pl.BlockSpec((1, RG_HALF), lambda i, h: (0, h)),
                  pl.BlockSpec((2, 1, RG_HALF, 2 * RG_HALF), lambda i, h: (0, h, 0, 0)),
                  pl.BlockSpec((2, 1, 1, 2 * RG_HALF), lambda i, h: (0, h, 0, 0)),
                  pl.BlockSpec((2, 1, 1, RG_HALF), lambda i, h: (0, h, 0, 0)),
                  pl.BlockSpec((1, 2, RG_HALF), lambda i, h: (i, 0, h))],
        out_specs=[pl.BlockSpec((1, seq, RG_HALF), lambda i, h: (i, 0, h)),
                   pl.BlockSpec((1, 2, RG_HALF), lambda i, h: (i, 0, h))],
        out_shape=[jax.ShapeDtypeStruct((b, seq, D_RNN), F32),
                   jax.ShapeDtypeStruct((b, 2, D_RNN), F32)],
        scratch_shapes=[pltpu.VMEM((seq + 2 * SUBLANES, RG_HALF), F32),
                        pltpu.VMEM((seq, RG_HALF), F32),
                        pltpu.VMEM((seq, RG_HALF), F32),
                        pltpu.VMEM((seq, RG_HALF), F32),
                        pltpu.VMEM((2, tc, RG_HALF), F32),
                        pltpu.VMEM((2, tc, RG_HALF), F32)],
        compiler_params=_cparams(("parallel", "parallel")),
        name="rglru",
    )(xy, xy, cw, cb, wg, bg, ap, h0)


def _gate_weights(gate_w, gate_b, a_param):
    nh = D_RNN // RG_HALF
    hp = N_HEADS // nh
    w = gate_w.reshape(2, 2, nh, hp, HEAD_DIM, HEAD_DIM)
    eye = jnp.eye(hp, dtype=F32)
    dense = jnp.einsum('dghpio,pq->dhpigqo', w, eye)
    dense = (0.5 * dense).reshape(2, nh, RG_HALF, 2 * RG_HALF).astype(BF16)
    bias = 0.5 * gate_b.reshape(2, 2, nh, RG_HALF).transpose(0, 2, 1, 3).reshape(2, nh, 1, 2 * RG_HALF)
    ap = a_param.reshape(2, nh, 1, RG_HALF)
    return dense, bias, ap


def _add(a, b):
    if a is None:
        return b
    if b is None:
        return a
    return a + b


def _sub(a, b):
    if b is None:
        return a
    if a is None:
        return -b
    return a - b


def _scale(a, s):
    if a is None or s == 0.0:
        return None
    if s == 1.0:
        return a
    if s == -1.0:
        return -a
    return a * s


def _cmul_const(z, wr, wi):
    if z is None:
        return None
    re, im = z
    if abs(wr) < 1e-15:
        wr = 0.0
    if abs(wi) < 1e-15:
        wi = 0.0
    if wr != 0.0 and abs(abs(wr) - abs(wi)) < 1e-15 and re is not None and im is not None:
        sr, si = math.copysign(1.0, wr), math.copysign(1.0, wi)
        return (_scale(_sub(_scale(re, sr), _scale(im, si)), abs(wr)), _scale(_add(_scale(re, si), _scale(im, sr)), abs(wr)))
    return (_sub(_scale(re, wr), _scale(im, wi)), _add(_scale(re, wi), _scale(im, wr)))


def _cadd(a, b):
    if a is None:
        return b
    if b is None:
        return a
    return (_add(a[0], b[0]), _add(a[1], b[1]))


def _csub(a, b):
    if b is None:
        return a
    if a is None:
        return (_sub(None, b[0]), _sub(None, b[1]))
    return (_sub(a[0], b[0]), _sub(a[1], b[1]))


def _cfft(xs, sign):
    n = len(xs)
    if n == 1:
        return list(xs)
    even = _cfft(xs[0::2], sign)
    odd = _cfft(xs[1::2], sign)
    out = [None] * n
    for k in range(n // 2):
        ang = sign * 2.0 * math.pi * k / n
        t = _cmul_const(odd[k], math.cos(ang), math.sin(ang))
        out[k] = _cadd(even[k], t)
        out[k + n // 2] = _csub(even[k], t)
    return out


def _fft64(get, put, sign, t_ref, rc, nin, nout):
    r8 = 8
    for p in range(r8):
        if sign < 0:
            xs = [get(r8 * a + p) if r8 * a + p < nin else None for a in range(r8)]
        else:
            xs = [get(p + r8 * d) for d in range(r8)]
        ts = _cfft(xs, sign)
        for q in range(r8):
            ang = sign * 2.0 * math.pi * p * q / 64.0
            re, im = _cmul_const(ts[q], math.cos(ang), math.sin(ang))
            rows = pl.ds((p * r8 + q) * rc, rc)
            t_ref[rows, 0:LANES] = re
            t_ref[rows, LANES:2 * LANES] = im
    for q in range(r8):
        ys = [(t_ref[pl.ds((p * r8 + q) * rc, rc), 0:LANES], t_ref[pl.ds((p * r8 + q) * rc, rc), LANES:2 * LANES])
              for p in range(r8)]
        zs = _cfft(ys, sign)
        for m in range(r8):
            k = q + r8 * m if sign < 0 else r8 * m + q
            if k < nout:
                put(k, zs[m])


def _dft_tables(n1, rc):
    n = n1 * DFT2
    k = np.arange(DFT2, dtype=np.float64)
    ang = 2.0 * np.pi * np.outer(k, k) / DFT2
    c, s = np.cos(ang), np.sin(ang)
    fwd = np.block([[c, -s], [s, c]])
    inv = np.block([[c, s], [-s, c]]) / n

    def split(m):
        hi = m.astype(np.float32).astype(ml_dtypes.bfloat16)
        lo = (m - hi.astype(np.float64)).astype(ml_dtypes.bfloat16)
        return jnp.asarray(hi), jnp.asarray(lo)

    tw_ang = 2.0 * np.pi * np.outer(np.arange(n1, dtype=np.float64), k) / n
    tw = np.concatenate([np.cos(tw_ang), -np.sin(tw_ang)], axis=1)
    tw = np.repeat(tw, rc, axis=0).astype(np.float32)
    return split(fwd) + split(inv) + (jnp.asarray(tw),)


HY_CB = 32


def _fft_rows(n1, cb):
    return SUBLANES if n1 >= 16 else cb


def _dft_mm(x, hi_ref, lo_ref):
    xh = x.astype(BF16)
    acc = jnp.dot(xh, hi_ref[...], preferred_element_type=F32)
    if FFT_PASSES >= 2:
        acc = acc + jnp.dot(xh, lo_ref[...], preferred_element_type=F32)
    if FFT_PASSES >= 3:
        xl = (x - xh.astype(F32)).astype(BF16)
        acc = acc + jnp.dot(xl, hi_ref[...], preferred_element_type=F32)
    return acc


def _fwd_lane_fft(blocks, tw_ref, n1, rc):
    xs = _cfft(blocks, -1)
    out = []
    for k1 in range(n1):
        z = xs[k1]
        if k1 > 0 and z is not None:
            twr = tw_ref[k1 * rc:(k1 + 1) * rc, 0:LANES]
            twi = tw_ref[k1 * rc:(k1 + 1) * rc, LANES:2 * LANES]
            re, im = z
            if im is None:
                z = (re * twr, re * twi)
            else:
                z = (re * twr - im * twi, re * twi + im * twr)
        out.append(z)
    return out


def _filter_kernel(feat_ref, w1_ref, b1_ref, w2_ref, b2_ref, fr_ref, w3_ref, fh_ref, fl_ref, tw_ref, kf_ref,
                   h2_s, k_s, s_s, *, seq, n1, cb, rc):
    n = 2 * seq
    hi = lax.Precision.HIGHEST

    @pl.when(pl.program_id(0) == 0)
    def _():
        h = jnp.dot(w1_ref[...], feat_ref[...], precision=hi, preferred_element_type=F32) + b1_ref[...]
        h = jnp.sin(fr_ref[:, 0:1] * h)
        h = jnp.dot(w2_ref[...], h, precision=hi, preferred_element_type=F32) + b2_ref[...]
        h2_s[...] = jnp.sin(fr_ref[:, 1:2] * h)

    lane = lax.broadcasted_iota(jnp.int32, (cb, n), 1)
    pos = jnp.where(lane < seq, lane, n - lane).astype(F32)
    t = pos / float(max(seq - 1, 1))
    d_idx = (lax.broadcasted_iota(jnp.int32, (cb, n), 0) + pl.program_id(0) * cb).astype(F32)
    min_decay = math.log(1e-2) / 1.5
    max_decay = math.log(1e-2) / 0.3
    delta = jnp.abs(min_decay + d_idx * ((max_decay - min_decay) / (D_HY - 1)))
    decay = jnp.exp(-t * delta)
    hfb = jnp.dot(w3_ref[...].reshape(4 * cb, HY_FH), h2_s[...], precision=hi, preferred_element_type=F32)
    for o in range(2):
        hf = hfb[(2 * o) * cb:(2 * o + 1) * cb]
        hb = hfb[(2 * o + 1) * cb:(2 * o + 2) * cb]
        k_s[o] = jnp.where(lane < seq, hf, jnp.where(lane == seq, 0.0, hb)) * decay

    nrc = cb // rc
    for o in range(2):
        def rows(i, _, o=o):
            r0 = pl.multiple_of(i * rc, rc)
            blocks = [(k_s[o, pl.ds(r0, rc), j * LANES:(j + 1) * LANES], None) for j in range(n1)]
            zs = _fwd_lane_fft(blocks, tw_ref, n1, rc)
            for k1 in range(n1):
                re, im = zs[k1]
                s_s[pl.ds(k1 * cb + r0, rc), 0:LANES] = re
                s_s[pl.ds(k1 * cb + r0, rc), LANES:2 * LANES] = im if im is not None else jnp.zeros_like(re)
            return 0
        lax.fori_loop(0, nrc, rows, 0)
        x = s_s[...]
        xh = x.astype(BF16)
        xl = (x - xh.astype(F32)).astype(BF16)
        z = (jnp.dot(xh, fh_ref[...], preferred_element_type=F32) + jnp.dot(xh, fl_ref[...], preferred_element_type=F32)
             + jnp.dot(xl, fh_ref[...], preferred_element_type=F32))
        for i in range(nrc):
            for k1 in range(n1):
                kf_ref[o, i, k1 * rc:(k1 + 1) * rc, :] = z[k1 * cb + i * rc:k1 * cb + (i + 1) * rc, :]


def _hyena_filters(seq, w1, b1, w2, b2, w3, freq, tables):
    n1 = 2 * seq // DFT2
    n = 2 * seq
    cb = HY_CB
    rc = _fft_rows(n1, cb)
    fh, fl, _, _, tw = tables
    pos = np.arange(n, dtype=np.float64)
    pos = np.where(pos < seq, pos, n - pos)
    tt = pos / max(seq - 1, 1)
    omega = 2.0 * math.pi * pos / seq
    bands = np.linspace(1e-4, HY_BANDS - 1, HY_BANDS)
    ang = omega[None, :] * bands[:, None]
    feats = np.concatenate([tt[None, :], np.cos(ang), np.sin(ang)], axis=0)
    nfeat = LANES
    feats = np.pad(feats, ((0, nfeat - feats.shape[0]), (0, 0))).astype(np.float32)
    w1 = jnp.pad(w1, ((0, nfeat - w1.shape[0]), (0, 0)))
    w3t = w3.T.reshape(4, D_HY, HY_FH)
    kern = functools.partial(_filter_kernel, seq=seq, n1=n1, cb=cb, rc=rc)
    full = lambda *shape: pl.BlockSpec(shape, lambda i: (0,) * len(shape))
    return pl.pallas_call(
        kern,
        grid=(D_HY // cb,),
        in_specs=[full(nfeat, n), full(HY_FH, nfeat), full(HY_FH, 1), full(HY_FH, HY_FH), full(HY_FH, 1),
                  full(HY_FH, 2),
                  pl.BlockSpec((4, cb, HY_FH), lambda i: (0, i, 0)),
                  full(2 * DFT2, 2 * DFT2), full(2 * DFT2, 2 * DFT2), full(n1 * rc, 2 * DFT2)],
        out_specs=pl.BlockSpec((2, cb // rc, n1 * rc, 2 * DFT2), lambda i: (0, i, 0, 0)),
        out_shape=jax.ShapeDtypeStruct((2, D_HY // rc, n1 * rc, 2 * DFT2), F32),
        scratch_shapes=[pltpu.VMEM((HY_FH, n), F32), pltpu.VMEM((2, cb, n), F32),
                        pltpu.VMEM((n1 * cb, 2 * DFT2), F32)],
        compiler_params=_cparams(("arbitrary",)),
        name="hyena_filters",
    )(jnp.asarray(feats), w1.T, b1.reshape(HY_FH, 1), w2.T, b2.reshape(HY_FH, 1), freq.T, w3t, fh, fl, tw)


HY_UNIT = 512


def _hyena_kernel(hy_ref, cw_ref, cb_ref, bias_ref, kf_ref, fh_ref, fl_ref, ih_ref, il_ref, tw_ref, o_ref,
                  buf, sa, sb, ta, tc, *, seq, n1, cb, nb, rc, pu):
    npairs = nb // 2
    nin = n1 // 2
    nrc = cb // rc
    upo = (npairs // pu) * nrc
    nitems = 2 * upo
    assert upo >= 3, "an item's second-order stage A must come after its first-order stage C"
    lane = lax.broadcasted_iota(jnp.int32, (rc, seq), 1)

    def conv3(h, part, r0):
        w0 = cw_ref[0, part, pl.ds(r0, rc), :]
        w1 = cw_ref[1, part, pl.ds(r0, rc), :]
        w2 = cw_ref[2, part, pl.ds(r0, rc), :]
        bb = cb_ref[part, pl.ds(r0, rc), :]
        hm = jnp.where(lane == 0, 0.0, pltpu.roll(h, 1, axis=1))
        hp = jnp.where(lane == seq - 1, 0.0, pltpu.roll(h, seq - 1, axis=1))
        return w0 * hm + w1 * h + w2 * hp + bb

    def prep(i, _):
        b = i // nrc
        r0 = pl.multiple_of((i % nrc) * rc, rc)
        buf[0, b, pl.ds(r0, rc), :] = conv3(hy_ref[b, 0, pl.ds(r0, rc), :], 0, r0)
        return 0
    lax.fori_loop(0, nb * nrc, prep, 0, unroll=2)

    def item(j):
        o = j // upo
        u = j % upo
        return o, u // nrc, u % nrc

    def stage_a(j, slot):
        o, pg, rg = item(j)
        r0 = pl.multiple_of(rg * rc, rc)
        for pp in range(pu):
            p = pg * pu + pp
            def get(jb, p=p):
                return (buf[o, 2 * p, pl.ds(r0, rc), jb * LANES:(jb + 1) * LANES],
                        buf[o, 2 * p + 1, pl.ds(r0, rc), jb * LANES:(jb + 1) * LANES])

            def put(k1, z, pp=pp):
                re, im = z
                if k1 > 0:
                    twr = tw_ref[k1 * rc:(k1 + 1) * rc, 0:LANES]
                    twi = tw_ref[k1 * rc:(k1 + 1) * rc, LANES:2 * LANES]
                    re, im = re * twr - im * twi, re * twi + im * twr
                rows = pl.ds((pp * n1 + k1) * rc, rc)
                sa[slot, rows, 0:LANES] = re
                sa[slot, rows, LANES:2 * LANES] = im

            if n1 == 64:
                _fft64(get, put, -1, ta, rc, nin, n1)
            else:
                zs = _cfft([get(jb) if jb < nin else None for jb in range(n1)], -1)
                for k1 in range(n1):
                    put(k1, zs[k1])

    def stage_b(j, slot):
        o, _, rg = item(j)
        z = _dft_mm(sa[slot], fh_ref, fl_ref)
        kk = kf_ref[o, rg]
        if pu > 1:
            kk = jnp.concatenate([kk] * pu, axis=0)
        zr, zi = z[:, :LANES], z[:, LANES:]
        kr, ki = kk[:, :LANES], kk[:, LANES:]
        w = jnp.concatenate([zr * kr - zi * ki, zr * ki + zi * kr], axis=1)
        sb[slot] = _dft_mm(w, ih_ref, il_ref)

    def stage_c(j, slot):
        o, pg, rg = item(j)
        r0 = pl.multiple_of(rg * rc, rc)
        bias = bias_ref[o, pl.ds(r0, rc), :]
        for pp in range(pu):
            p = pg * pu + pp
            xg = [conv3(hy_ref[2 * p + q, 1 + o, pl.ds(r0, rc), :], 1 + o, r0) for q in range(2)]

            def get(k1, pp=pp):
                re = sb[slot, pl.ds((pp * n1 + k1) * rc, rc), 0:LANES]
                im = sb[slot, pl.ds((pp * n1 + k1) * rc, rc), LANES:2 * LANES]
                if k1 > 0:
                    twr = tw_ref[k1 * rc:(k1 + 1) * rc, 0:LANES]
                    twi = tw_ref[k1 * rc:(k1 + 1) * rc, LANES:2 * LANES]
                    re, im = re * twr + im * twi, im * twr - re * twi
                return (re, im)

            def put(jb, y, p=p, xg=xg):
                lanes = slice(jb * LANES, (jb + 1) * LANES)
                for q in range(2):
                    u = buf[o, 2 * p + q, pl.ds(r0, rc), lanes]
                    buf[o + 1, 2 * p + q, pl.ds(r0, rc), lanes] = xg[q][:, lanes] * (y[q] + u * bias)

            if n1 == 64:
                _fft64(get, put, +1, tc, rc, n1, nin)
            else:
                ys = _cfft([get(k1) for k1 in range(n1)], +1)
                for jb in range(nin):
                    put(jb, ys[jb])

    stage_a(0, 0)
    stage_a(1, 1)
    stage_b(0, 0)

    assert nitems % 2 == 0

    def steady(i, _):
        t = 2 + 2 * i
        stage_b(t - 1, 1)
        stage_a(t, 0)
        stage_c(t - 2, 0)
        stage_b(t, 0)
        stage_a(t + 1, 1)
        stage_c(t - 1, 1)
        return 0
    lax.fori_loop(0, (nitems - 2) // 2, steady, 0)
    stage_b(nitems - 1, (nitems - 1) % 2)
    stage_c(nitems - 2, nitems % 2)
    stage_c(nitems - 1, (nitems - 1) % 2)
    for b in range(nb):
        o_ref[b] = buf[2, b]


def _hyena(hyt, cw, cbias, bias, kf, tables):
    nb, _, seq = hyt.shape
    n1 = 2 * seq // DFT2
    cb = HY_CB
    rc = _fft_rows(n1, cb)
    pu = HY_UNIT // (n1 * rc)
    nblk = D_HY // cb
    fh, fl, ih, il, tw = tables
    kern = functools.partial(_hyena_kernel, seq=seq, n1=n1, cb=cb, nb=nb, rc=rc, pu=pu)
    full = lambda *shape: pl.BlockSpec(shape, lambda i: (0,) * len(shape))
    return pl.pallas_call(
        kern,
        grid=(nblk,),
        in_specs=[pl.BlockSpec((nb, 3, cb, seq), lambda i: (0, 0, i, 0)),
                  pl.BlockSpec((3, 3, cb, 1), lambda i: (0, 0, i, 0)),
                  pl.BlockSpec((3, cb, 1), lambda i: (0, i, 0)),
                  pl.BlockSpec((2, cb, 1), lambda i: (0, i, 0)),
                  pl.BlockSpec((2, cb // rc, n1 * rc, 2 * DFT2), lambda i: (0, i, 0, 0)),
                  full(2 * DFT2, 2 * DFT2), full(2 * DFT2, 2 * DFT2),
                  full(2 * DFT2, 2 * DFT2), full(2 * DFT2, 2 * DFT2),
                  full(n1 * rc, 2 * DFT2)],
        out_specs=pl.BlockSpec((nb, cb, seq), lambda i: (0, i, 0)),
        out_shape=jax.ShapeDtypeStruct((nb, D_HY, seq), F32),
        scratch_shapes=[pltpu.VMEM((3, nb, cb, seq), F32),
                        pltpu.VMEM((2, HY_UNIT, 2 * DFT2), F32), pltpu.VMEM((2, HY_UNIT, 2 * DFT2), F32),
                        pltpu.VMEM((n1 * rc, 2 * DFT2), F32), pltpu.VMEM((n1 * rc, 2 * DFT2), F32)],
        compiler_params=_cparams(("parallel",)),
        name="hyena",
    )(hyt.reshape(nb, 3, D_HY, seq), cw, cbias, bias, kf, fh, fl, ih, il, tw)


def _out_proj_kernel(x_ref, or_ref, oht_ref, mod_ref, gr_ref, gh_ref, wr_ref, wh_ref, g2_ref, x1_ref, xn_ref):
    orn = (_rms(or_ref[0]) * gr_ref[...]).astype(BF16)
    oh = oht_ref[0]
    ohn = oh * lax.rsqrt(jnp.mean(oh * oh, axis=0, keepdims=True) + EPS) * gh_ref[...]
    o = jnp.dot(orn, wr_ref[...], preferred_element_type=F32)
    o = o + lax.dot_general(ohn.astype(BF16), wh_ref[...], (((0,), (0,)), ((), ())), preferred_element_type=F32)
    x1 = x_ref[0] + mod_ref[0, 2:3, :] * o
    x1_ref[0] = x1
    xn = _rms(x1) * g2_ref[...]
    xn_ref[0] = (xn * (1.0 + mod_ref[0, 4:5, :]) + mod_ref[0, 3:4, :]).astype(BF16)


def _out_proj(x, o_r, oht, mod, gr, gh, wr, wh, g2, tm):
    b, l, _ = x.shape
    mod_map = (lambda i, j: (i, 0, 0)) if mod.shape[0] > 1 else (lambda i, j: (0, 0, 0))
    return pl.pallas_call(
        _out_proj_kernel,
        grid=(b, l // tm),
        in_specs=[pl.BlockSpec((1, tm, D_MODEL), lambda i, j: (i, j, 0)),
                  pl.BlockSpec((1, tm, D_RNN), lambda i, j: (i, j, 0)),
                  pl.BlockSpec((1, D_HY, tm), lambda i, j: (i, 0, j)),
                  pl.BlockSpec((1, N_MOD, D_MODEL), mod_map),
                  pl.BlockSpec((1, D_RNN), lambda i, j: (0, 0)),
                  pl.BlockSpec((D_HY, 1), lambda i, j: (0, 0)),
                  pl.BlockSpec((D_RNN, D_MODEL), lambda i, j: (0, 0)),
                  pl.BlockSpec((D_HY, D_MODEL), lambda i, j: (0, 0)),
                  pl.BlockSpec((1, D_MODEL), lambda i, j: (0, 0))],
        out_specs=[pl.BlockSpec((1, tm, D_MODEL), lambda i, j: (i, j, 0)),
                   pl.BlockSpec((1, tm, D_MODEL), lambda i, j: (i, j, 0))],
        out_shape=[jax.ShapeDtypeStruct((b, l, D_MODEL), F32),
                   jax.ShapeDtypeStruct((b, l, D_MODEL), BF16)],
        compiler_params=_cparams(("parallel", "parallel")),
        name="out_proj",
    )(x, o_r, oht, mod, gr, gh, wr, wh, g2)


FF_TM = 512
FF_SUB = 2 * LANES


def _ffn_kernel(x1_ref, xn_ref, xp_ref, xq_ref, mod_ref, wu_ref, cw_ref, cb_ref, wd_ref, gf_ref, o_ref, xe_s, h_s,
                *, seg, halo, on_grid, tiles_per_seq, final_norm):
    t = pl.program_id(0)
    tm = x1_ref.shape[1]
    rows = tm + 2 * halo
    if halo:
        first = (t % tiles_per_seq) == 0
        last = (t % tiles_per_seq) == tiles_per_seq - 1
        xe_s[0:halo, :] = jnp.where(first, jnp.zeros_like(xp_ref[0]), xp_ref[0])
        xe_s[halo + tm:rows, :] = jnp.where(last, jnp.zeros_like(xq_ref[0]), xq_ref[0])
    xe_s[halo:halo + tm, :] = xn_ref[0]

    pos = lax.broadcasted_iota(jnp.int32, (rows, 1), 0) % seg
    for c0 in range(0, D_FF, FF_SUB):
        w = min(FF_SUB, D_FF - c0)
        g = jnp.dot(xe_s[...], wu_ref[:, D_FF + c0:D_FF + c0 + w], preferred_element_type=F32)
        a = jnp.dot(xe_s[halo:halo + tm, :], wu_ref[:, c0:c0 + w], preferred_element_type=F32)
        gls = pltpu.roll(jnp.where(pos == seg - 1, 0.0, g), 1, axis=0)
        grs = pltpu.roll(jnp.where(pos == 0, 0.0, g), rows - 1, axis=0)
        acc = cb_ref[:, c0:c0 + w]
        for dr in ((-1, 0, 1) if on_grid else (0,)):
            lo = halo + dr * seg
            acc = (acc + cw_ref[dr + 1, 0:1, c0:c0 + w] * gls[lo:lo + tm]
                   + cw_ref[dr + 1, 1:2, c0:c0 + w] * g[lo:lo + tm]
                   + cw_ref[dr + 1, 2:3, c0:c0 + w] * grs[lo:lo + tm])
        h_s[:, c0:c0 + w] = (jax.nn.gelu(acc) * a).astype(BF16)
    y = jnp.dot(h_s[...], wd_ref[...], preferred_element_type=F32)
    x2 = x1_ref[0] + mod_ref[0, 5:6, :] * y
    if final_norm:
        x2 = _rms(x2) * gf_ref[...]
    o_ref[0] = x2


def _ffn(x1, xn2, mod, wu, cw, cb, wd, gf, on_grid, final_norm):
    b, l, _ = x1.shape
    tm = FF_TM
    if on_grid:
        seg, halo = GRID_W, GRID_W
        tps = l // tm
    else:
        seg, halo = l, 0
        tps = 1
    nt = b * l // tm
    hb = GRID_W
    nhb = tm // hb
    last_hb = b * l // hb - 1
    mod_map = (lambda t: (t // tps, 0, 0)) if mod.shape[0] > 1 else (lambda t: (0, 0, 0))
    kern = functools.partial(_ffn_kernel, seg=seg, halo=halo, on_grid=on_grid, tiles_per_seq=tps, final_norm=final_norm)
    const = lambda *shape: pl.BlockSpec(shape, lambda t: (0,) * len(shape), pipeline_mode=pl.Buffered(1))
    xh = xn2.reshape(b * l // hb, hb, D_MODEL)
    out = pl.pallas_call(
        kern,
        grid=(nt,),
        in_specs=[pl.BlockSpec((1, tm, D_MODEL), lambda t: (t, 0, 0)),
                  pl.BlockSpec((1, tm, D_MODEL), lambda t: (t, 0, 0)),
                  pl.BlockSpec((1, hb, D_MODEL), lambda t: (jnp.maximum(t * nhb - 1, 0), 0, 0)),
                  pl.BlockSpec((1, hb, D_MODEL), lambda t: (jnp.minimum((t + 1) * nhb, last_hb), 0, 0)),
                  pl.BlockSpec((1, N_MOD, D_MODEL), mod_map),
                  const(D_MODEL, 2 * D_FF), const(3, 3, D_FF), const(1, D_FF), const(D_FF, D_MODEL), const(1, D_MODEL)],
        out_specs=pl.BlockSpec((1, tm, D_MODEL), lambda t: (t, 0, 0)),
        out_shape=jax.ShapeDtypeStruct((nt, tm, D_MODEL), F32),
        scratch_shapes=[pltpu.VMEM((tm + 2 * halo, D_MODEL), BF16), pltpu.VMEM((tm, D_FF), BF16)],
        compiler_params=_cparams(("parallel",)),
        name="ffn",
    )(x1.reshape(nt, tm, D_MODEL), xn2.reshape(nt, tm, D_MODEL), xh, xh, mod, wu, cw, cb, wd, gf)
    return out.reshape(b, l, D_MODEL)


def _trunk_layer(x, mod, h0, p, kf, tables, on_grid, final_norm, g_final):
    b, l, _ = x.shape
    tm = min(512, l)
    xy, hyt = _in_proj(x, mod, p['g_norm1'], p['wxy'], p['whyt'], tm)
    o_r, states = _rglru(xy, p['rg_conv_w'], p['rg_conv_b'], p['wg'], p['bg'], p['ap'], h0)
    oht = _hyena(hyt, p['hy_cw'], p['hy_cb'], p['hy_bias'], kf, tables)
    x1, xn2 = _out_proj(x, o_r, oht, mod, p['g_rnn_out'], p['g_hy_out'], p['w_out_r'], p['w_out_h'], p['g_norm2'], tm)
    x2 = _ffn(x1, xn2, mod, p['w_up'], p['ffn_conv_w'], p['ffn_conv_b'], p['w_down'], g_final, on_grid, final_norm)
    return x2, states


def kernel(x_prompt, x_sample, state_rglru, c, c_ctx, w_ada, b_ada, g_norm1, g_norm2, w_in, rg_conv_w, rg_conv_b, rg_gate_w, rg_gate_b, rg_a, hy_conv_w, hy_conv_b, hf_w1, hf_b1, hf_w2, hf_b2, hf_w3, hf_freq, hy_bias, g_rnn_out, g_hy_out, w_out, w_up, ffn_conv_w, ffn_conv_b, w_down, g_final):
    depth = w_in.shape[0]
    nb_ctx, l_ctx, _ = x_prompt.shape
    nb_lat, l_lat, _ = x_sample.shape

    cc = jnp.zeros((SUBLANES, D_MODEL), F32).at[0].set(c_ctx).at[1:1 + nb_lat].set(c)
    mods = _modulation(cc, w_ada, b_ada).reshape(depth, SUBLANES, N_MOD, D_MODEL)

    tab_ctx = _dft_tables(2 * l_ctx // DFT2, _fft_rows(2 * l_ctx // DFT2, HY_CB))
    tab_lat = _dft_tables(2 * l_lat // DFT2, _fft_rows(2 * l_lat // DFT2, HY_CB))
    gf = g_final.reshape(1, D_MODEL)
    zero_h = jnp.zeros((nb_ctx, 2, D_RNN), F32)

    xp, xs = x_prompt, x_sample
    new_states = []
    for l in range(depth):
        wg, bg, ap = _gate_weights(rg_gate_w[l], rg_gate_b[l], rg_a[l])
        p = {
            'g_norm1': g_norm1[l].reshape(1, D_MODEL), 'g_norm2': g_norm2[l].reshape(1, D_MODEL),
            'wxy': w_in[l, :, :2 * D_RNN].astype(BF16), 'whyt': w_in[l, :, 2 * D_RNN:].T.astype(BF16),
            'rg_conv_w': rg_conv_w[l], 'rg_conv_b': rg_conv_b[l].reshape(1, D_RNN),
            'wg': wg, 'bg': bg, 'ap': ap,
            'hy_cw': hy_conv_w[l].reshape(3, 3, D_HY, 1), 'hy_cb': hy_conv_b[l].reshape(3, D_HY, 1),
            'hy_bias': hy_bias[l].reshape(2, D_HY, 1),
            'g_rnn_out': g_rnn_out[l].reshape(1, D_RNN), 'g_hy_out': g_hy_out[l].reshape(D_HY, 1),
            'w_out_r': w_out[l, :D_RNN].astype(BF16), 'w_out_h': w_out[l, D_RNN:].astype(BF16),
            'w_up': w_up[l].astype(BF16), 'ffn_conv_w': ffn_conv_w[l], 'ffn_conv_b': ffn_conv_b[l].reshape(1, D_FF),
            'w_down': w_down[l].astype(BF16),
        }
        fargs = (hf_w1[l], hf_b1[l], hf_w2[l], hf_b2[l], hf_w3[l], hf_freq[l])
        kf_ctx = _hyena_filters(l_ctx, *fargs, tab_ctx)
        kf_lat = _hyena_filters(l_lat, *fargs, tab_lat)
        final = l == depth - 1
        xp, st = _trunk_layer(xp, mods[l, 0:1], zero_h, p, kf_ctx, tab_ctx, False, final, gf)
        new_states.append(st)
        xs, _ = _trunk_layer(xs, mods[l, 1:1 + nb_lat], state_rglru[:, l], p, kf_lat, tab_lat, True, final, gf)
    return (xp, xs, jnp.stack(new_states, axis=1))
```

```python
import functools
import math

import jax
import jax.numpy as jnp
import ml_dtypes
import numpy as np
from jax import lax
from jax.experimental import pallas as pl
from jax.experimental.pallas import tpu as pltpu

F32 = jnp.float32
BF16 = jnp.bfloat16

D_MODEL = 1024
D_RNN = 512
D_HY = 512
N_HEADS = 8
HEAD_DIM = D_RNN // N_HEADS
RG_C = 8.0
GRID_W = 64
HY_BANDS = 16
HY_FH = 64
D_FF = 2816
N_MOD = 6
EPS = 1e-6

SUBLANES = 8
LANES = 128
VMEM_LIMIT = 56 * 1024 * 1024

DFT2 = LANES
FFT_PASSES = 2


def _cparams(sem):
    return pltpu.CompilerParams(dimension_semantics=sem, vmem_limit_bytes=VMEM_LIMIT)


def _rms(x):
    return x * lax.rsqrt(jnp.mean(x * x, axis=-1, keepdims=True) + EPS)


def _mod_kernel(c_ref, w_ref, b_ref, o_ref):
    c = c_ref[...]
    s = c * jax.nn.sigmoid(c)
    o_ref[0] = jnp.dot(s, w_ref[0], precision=lax.Precision.HIGHEST, preferred_element_type=F32) + b_ref[0]


def _modulation(cc, w_ada, b_ada):
    depth, _, n = w_ada.shape
    tn = 1536
    return pl.pallas_call(
        _mod_kernel,
        grid=(depth, n // tn),
        in_specs=[pl.BlockSpec((SUBLANES, D_MODEL), lambda l, j: (0, 0)),
                  pl.BlockSpec((1, D_MODEL, tn), lambda l, j: (l, 0, j)),
                  pl.BlockSpec((1, 1, tn), lambda l, j: (l, 0, j))],
        out_specs=pl.BlockSpec((1, SUBLANES, tn), lambda l, j: (l, 0, j)),
        out_shape=jax.ShapeDtypeStruct((depth, SUBLANES, n), F32),
        compiler_params=_cparams(("parallel", "parallel")),
        name="adaln_mod",
    )(cc, w_ada, b_ada.reshape(depth, 1, n))


def _in_proj_kernel(x_ref, mod_ref, g_ref, wxy_ref, whyt_ref, xy_ref, hyt_ref):
    x = x_ref[0]
    xn = _rms(x) * g_ref[...]
    xn = (xn * (1.0 + mod_ref[0, 1:2, :]) + mod_ref[0, 0:1, :]).astype(BF16)
    xy_ref[0] = jnp.dot(xn, wxy_ref[...], preferred_element_type=F32)
    hyt_ref[0] = lax.dot_general(whyt_ref[...], xn, (((1,), (1,)), ((), ())), preferred_element_type=F32)


def _in_proj(x, mod, g, wxy, whyt, tm):
    b, l, _ = x.shape
    per_batch_mod = mod.shape[0] > 1
    mod_map = (lambda i, j: (i, 0, 0)) if per_batch_mod else (lambda i, j: (0, 0, 0))
    return pl.pallas_call(
        _in_proj_kernel,
        grid=(b, l // tm),
        in_specs=[pl.BlockSpec((1, tm, D_MODEL), lambda i, j: (i, j, 0)),
                  pl.BlockSpec((1, N_MOD, D_MODEL), mod_map),
                  pl.BlockSpec((1, D_MODEL), lambda i, j: (0, 0)),
                  pl.BlockSpec((D_MODEL, 2 * D_RNN), lambda i, j: (0, 0)),
                  pl.BlockSpec((3 * D_HY, D_MODEL), lambda i, j: (0, 0))],
        out_specs=[pl.BlockSpec((1, tm, 2 * D_RNN), lambda i, j: (i, j, 0)),
                   pl.BlockSpec((1, 3 * D_HY, tm), lambda i, j: (i, 0, j))],
        out_shape=[jax.ShapeDtypeStruct((b, l, 2 * D_RNN), F32),
                   jax.ShapeDtypeStruct((b, 3 * D_HY, l), F32)],
        compiler_params=_cparams(("parallel", "parallel")),
        name="in_proj",
    )(x, mod, g, wxy, whyt)


RG_HALF = D_RNN // 2
RG_TILES = RG_HALF // LANES


def _rglru_kernel(xr_ref, yr_ref, cw_ref, cb_ref, wg_ref, bg_ref, ap_ref, h0_ref, o_ref, st_ref,
                  ext, xc_s, hf, hb, a_s, b_s, *, seq, tc):
    nchunks = seq // tc
    nblk = tc // SUBLANES
    ext[0:SUBLANES, :] = jnp.zeros((SUBLANES, RG_HALF), F32)
    ext[SUBLANES:SUBLANES + seq, :] = xr_ref[0]
    ext[SUBLANES + seq:2 * SUBLANES + seq, :] = jnp.zeros((SUBLANES, RG_HALF), F32)
    for c in range(nchunks):
        xc = cb_ref[...] + cw_ref[0:1, :] * ext[pl.ds(c * tc + SUBLANES - 2, tc), :]
        for k in range(1, 4):
            xc = xc + cw_ref[k:k + 1, :] * ext[pl.ds(c * tc + SUBLANES - 2 + k, tc), :]
        xc_s[pl.ds(c * tc, tc), :] = xc

    row = lax.broadcasted_iota(jnp.int32, (SUBLANES, LANES), 0)
    half_neg_c_sp = [(-0.5 * RG_C) * jax.nn.softplus(-ap_ref[d, 0]) for d in range(2)]

    def gates(c0, d):
        xc = xc_s[pl.ds(c0, tc), :]
        g = jnp.dot(xc.astype(BF16), wg_ref[d, 0], preferred_element_type=F32) + bg_ref[d, 0]
        i = 0.5 * jnp.tanh(g[:, RG_HALF:]) + 0.5
        log_a = half_neg_c_sp[d] * jnp.tanh(g[:, :RG_HALF]) + half_neg_c_sp[d]
        a = jnp.exp(log_a)
        y = jnp.tanh(-log_a) * (a * a + 1.0)
        mult = jnp.where(y > 0.0, y * lax.rsqrt(y), 0.0)
        a_s[d] = a
        b_s[d] = xc * i * mult

    def local_scan(a, b, reverse):
        for s in (1, 2, 4):
            if reverse:
                keep = row < SUBLANES - s
                shift = SUBLANES - s
            else:
                keep = row >= s
                shift = s
            a_sh = jnp.where(keep, pltpu.roll(a, shift, axis=0), 1.0)
            b_sh = jnp.where(keep, pltpu.roll(b, shift, axis=0), 0.0)
            b = a * b_sh + b
            a = a * a_sh
        return a, b

    def block_body(cf0, cb0):
        def body(j, carry):
            rf = pl.multiple_of(j * SUBLANES, SUBLANES)
            rb = pl.multiple_of(tc - SUBLANES - j * SUBLANES, SUBLANES)
            out = []
            for t in range(RG_TILES):
                lanes = slice(t * LANES, (t + 1) * LANES)
                pa, pb = local_scan(a_s[0, pl.ds(rf, SUBLANES), lanes], b_s[0, pl.ds(rf, SUBLANES), lanes], False)
                h = pa * carry[2 * t] + pb
                hf[pl.ds(cf0 + rf, SUBLANES), lanes] = h
                out.append(jnp.broadcast_to(h[SUBLANES - 1:SUBLANES, :], (SUBLANES, LANES)))
                pa, pb = local_scan(a_s[1, pl.ds(rb, SUBLANES), lanes], b_s[1, pl.ds(rb, SUBLANES), lanes], True)
                h = pa * carry[2 * t + 1] + pb
                hb[pl.ds(cb0 + rb, SUBLANES), lanes] = h
                out.append(jnp.broadcast_to(h[0:1, :], (SUBLANES, LANES)))
            return tuple(out)
        return body

    carry = []
    for t in range(RG_TILES):
        carry.append(jnp.broadcast_to(h0_ref[0, 0:1, t * LANES:(t + 1) * LANES], (SUBLANES, LANES)))
        carry.append(jnp.broadcast_to(h0_ref[0, 1:2, t * LANES:(t + 1) * LANES], (SUBLANES, LANES)))
    carry = tuple(carry)
    for c in range(nchunks):
        cf0 = c * tc
        cb0 = (nchunks - 1 - c) * tc
        gates(cf0, 0)
        gates(cb0, 1)
        carry = lax.fori_loop(0, nblk, block_body(cf0, cb0), carry, unroll=2)
    for t in range(RG_TILES):
        st_ref[0, 0:1, t * LANES:(t + 1) * LANES] = carry[2 * t][0:1, :]
        st_ref[0, 1:2, t * LANES:(t + 1) * LANES] = carry[2 * t + 1][0:1, :]
    for c in range(nchunks):
        rows = pl.ds(c * tc, tc)
        o_ref[0, rows, :] = (hf[rows, :] + hb[rows, :]) * jax.nn.gelu(yr_ref[0, rows, :])


def _rglru(xy, cw, cb, wg, bg, ap, h0):
    b, seq, _ = xy.shape
    tc = min(512, seq)
    nh = D_RNN // RG_HALF
    kern = functools.partial(_rglru_kernel, seq=seq, tc=tc)
    return pl.pallas_call(
        kern,
        grid=(b, nh),
        in_specs=[pl.BlockSpec((1, seq, RG_HALF), lambda i, h: (i, 0, h)),
                  pl.BlockSpec((1, seq, RG_HALF), lambda i, h: (i, 0, nh + h)),
                  pl.BlockSpec((4, RG_HALF), lambda i, h: (0, h)),
                  pl.BlockSpec((1, RG_HALF), lambda i, h: (0, h)),
                  pl.BlockSpec((2, 1, RG_HALF, 2 * RG_HALF), lambda i, h: (0, h, 0, 0)),
                  pl.BlockSpec((2, 1, 1, 2 * RG_HALF), lambda i, h: (0, h, 0, 0)),
                  pl.BlockSpec((2, 1, 1, RG_HALF), lambda i, h: (0, h, 0, 0)),
                  pl.BlockSpec((1, 2, RG_HALF), lambda i, h: (i, 0, h))],
        out_specs=[pl.BlockSpec((1, seq, RG_HALF), lambda i, h: (i, 0, h)),
                   pl.BlockSpec((1, 2, RG_HALF), lambda i, h: (i, 0, h))],
        out_shape=[jax.ShapeDtypeStruct((b, seq, D_RNN), F32),
                   jax.ShapeDtypeStruct((b, 2, D_RNN), F32)],
        scratch_shapes=[pltpu.VMEM((seq + 2 * SUBLANES, RG_HALF), F32),
                        pltpu.VMEM((seq, RG_HALF), F32),
                        pltpu.VMEM((seq, RG_HALF), F32),
                        pltpu.VMEM((seq, RG_HALF), F32),
                        pltpu.VMEM((2, tc, RG_HALF), F32),
                        pltpu.VMEM((2, tc, RG_HALF), F32)],
        compiler_params=_cparams(("parallel", "parallel")),
        name="rglru",
    )(xy, xy, cw, cb, wg, bg, ap, h0)


def _gate_weights(gate_w, gate_b, a_param):
    nh = D_RNN // RG_HALF
    hp = N_HEADS // nh
    w = gate_w.reshape(2, 2, nh, hp, HEAD_DIM, HEAD_DIM)
    eye = jnp.eye(hp, dtype=F32)
    dense = jnp.einsum('dghpio,pq->dhpigqo', w, eye)
    dense = (0.5 * dense).reshape(2, nh, RG_HALF, 2 * RG_HALF).astype(BF16)
    bias = 0.5 * gate_b.reshape(2, 2, nh, RG_HALF).transpose(0, 2, 1, 3).reshape(2, nh, 1, 2 * RG_HALF)
    ap = a_param.reshape(2, nh, 1, RG_HALF)
    return dense, bias, ap


def _add(a, b):
    if a is None:
        return b
    if b is None:
        return a
    return a + b


def _sub(a, b):
    if b is None:
        return a
    if a is None:
        return -b
    return a - b


def _scale(a, s):
    if a is None or s == 0.0:
        return None
    if s == 1.0:
        return a
    if s == -1.0:
        return -a
    return a * s


def _cmul_const(z, wr, wi):
    if z is None:
        return None
    re, im = z
    if abs(wr) < 1e-15:
        wr = 0.0
    if abs(wi) < 1e-15:
        wi = 0.0
    if wr != 0.0 and abs(abs(wr) - abs(wi)) < 1e-15 and re is not None and im is not None:
        sr, si = math.copysign(1.0, wr), math.copysign(1.0, wi)
        return (_scale(_sub(_scale(re, sr), _scale(im, si)), abs(wr)), _scale(_add(_scale(re, si), _scale(im, sr)), abs(wr)))
    return (_sub(_scale(re, wr), _scale(im, wi)), _add(_scale(re, wi), _scale(im, wr)))


def _cadd(a, b):
    if a is None:
        return b
    if b is None:
        return a
    return (_add(a[0], b[0]), _add(a[1], b[1]))


def _csub(a, b):
    if b is None:
        return a
    if a is None:
        return (_sub(None, b[0]), _sub(None, b[1]))
    return (_sub(a[0], b[0]), _sub(a[1], b[1]))


def _cfft(xs, sign):
    n = len(xs)
    if n == 1:
        return list(xs)
    even = _cfft(xs[0::2], sign)
    odd = _cfft(xs[1::2], sign)
    out = [None] * n
    for k in range(n // 2):
        ang = sign * 2.0 * math.pi * k / n
        t = _cmul_const(odd[k], math.cos(ang), math.sin(ang))
        out[k] = _cadd(even[k], t)
        out[k + n // 2] = _csub(even[k], t)
    return out


def _fft64(get, put, sign, t_ref, rc, nin, nout):
    r8 = 8
    for p in range(r8):
        if sign < 0:
            xs = [get(r8 * a + p) if r8 * a + p < nin else None for a in range(r8)]
        else:
            xs = [get(p + r8 * d) for d in range(r8)]
        ts = _cfft(xs, sign)
        for q in range(r8):
            ang = sign * 2.0 * math.pi * p * q / 64.0
            re, im = _cmul_const(ts[q], math.cos(ang), math.sin(ang))
            rows = pl.ds((p * r8 + q) * rc, rc)
            t_ref[rows, 0:LANES] = re
            t_ref[rows, LANES:2 * LANES] = im
    for q in range(r8):
        ys = [(t_ref[pl.ds((p * r8 + q) * rc, rc), 0:LANES], t_ref[pl.ds((p * r8 + q) * rc, rc), LANES:2 * LANES])
              for p in range(r8)]
        zs = _cfft(ys, sign)
        for m in range(r8):
            k = q + r8 * m if sign < 0 else r8 * m + q
            if k < nout:
                put(k, zs[m])


def _dft_tables(n1, rc):
    n = n1 * DFT2
    k = np.arange(DFT2, dtype=np.float64)
    ang = 2.0 * np.pi * np.outer(k, k) / DFT2
    c, s = np.cos(ang), np.sin(ang)
    fwd = np.block([[c, -s], [s, c]])
    inv = np.block([[c, s], [-s, c]]) / n

    def split(m):
        hi = m.astype(np.float32).astype(ml_dtypes.bfloat16)
        lo = (m - hi.astype(np.float64)).astype(ml_dtypes.bfloat16)
        return jnp.asarray(hi), jnp.asarray(lo)

    tw_ang = 2.0 * np.pi * np.outer(np.arange(n1, dtype=np.float64), k) / n
    tw = np.concatenate([np.cos(tw_ang), -np.sin(tw_ang)], axis=1)
    tw = np.repeat(tw, rc, axis=0).astype(np.float32)
    return split(fwd) + split(inv) + (jnp.asarray(tw),)


HY_CB = 32


def _fft_rows(n1, cb):
    return SUBLANES if n1 >= 16 else cb


def _dft_mm(x, hi_ref, lo_ref):
    xh = x.astype(BF16)
    acc = jnp.dot(xh, hi_ref[...], preferred_element_type=F32)
    if FFT_PASSES >= 2:
        acc = acc + jnp.dot(xh, lo_ref[...], preferred_element_type=F32)
    if FFT_PASSES >= 3:
        xl = (x - xh.astype(F32)).astype(BF16)
        acc = acc + jnp.dot(xl, hi_ref[...], preferred_element_type=F32)
    return acc


def _fwd_lane_fft(blocks, tw_ref, n1, rc):
    xs = _cfft(blocks, -1)
    out = []
    for k1 in range(n1):
        z = xs[k1]
        if k1 > 0 and z is not None:
            twr = tw_ref[k1 * rc:(k1 + 1) * rc, 0:LANES]
            twi = tw_ref[k1 * rc:(k1 + 1) * rc, LANES:2 * LANES]
            re, im = z
            if im is None:
                z = (re * twr, re * twi)
            else:
                z = (re * twr - im * twi, re * twi + im * twr)
        out.append(z)
    return out


def _filter_kernel(feat_ref, w1_ref, b1_ref, w2_ref, b2_ref, fr_ref, w3_ref, fh_ref, fl_ref, tw_ref, kf_ref,
                   h2_s, k_s, s_s, *, seq, n1, cb, rc):
    n = 2 * seq
    hi = lax.Precision.HIGHEST

    @pl.when(pl.program_id(0) == 0)
    def _():
        h = jnp.dot(w1_ref[...], feat_ref[...], precision=hi, preferred_element_type=F32) + b1_ref[...]
        h = jnp.sin(fr_ref[:, 0:1] * h)
        h = jnp.dot(w2_ref[...], h, precision=hi, preferred_element_type=F32) + b2_ref[...]
        h2_s[...] = jnp.sin(fr_ref[:, 1:2] * h)

    lane = lax.broadcasted_iota(jnp.int32, (cb, n), 1)
    pos = jnp.where(lane < seq, lane, n - lane).astype(F32)
    t = pos / float(max(seq - 1, 1))
    d_idx = (lax.broadcasted_iota(jnp.int32, (cb, n), 0) + pl.program_id(0) * cb).astype(F32)
    min_decay = math.log(1e-2) / 1.5
    max_decay = math.log(1e-2) / 0.3
    delta = jnp.abs(min_decay + d_idx * ((max_decay - min_decay) / (D_HY - 1)))
    decay = jnp.exp(-t * delta)
    hfb = jnp.dot(w3_ref[...].reshape(4 * cb, HY_FH), h2_s[...], precision=hi, preferred_element_type=F32)
    for o in range(2):
        hf = hfb[(2 * o) * cb:(2 * o + 1) * cb]
        hb = hfb[(2 * o + 1) * cb:(2 * o + 2) * cb]
        k_s[o] = jnp.where(lane < seq, hf, jnp.where(lane == seq, 0.0, hb)) * decay

    nrc = cb // rc
    for o in range(2):
        def rows(i, _, o=o):
            r0 = pl.multiple_of(i * rc, rc)
            blocks = [(k_s[o, pl.ds(r0, rc), j * LANES:(j + 1) * LANES], None) for j in range(n1)]
            zs = _fwd_lane_fft(blocks, tw_ref, n1, rc)
            for k1 in range(n1):
                re, im = zs[k1]
                s_s[pl.ds(k1 * cb + r0, rc), 0:LANES] = re
                s_s[pl.ds(k1 * cb + r0, rc), LANES:2 * LANES] = im if im is not None else jnp.zeros_like(re)
            return 0
        lax.fori_loop(0, nrc, rows, 0)
        x = s_s[...]
        xh = x.astype(BF16)
        xl = (x - xh.astype(F32)).astype(BF16)
        z = (jnp.dot(xh, fh_ref[...], preferred_element_type=F32) + jnp.dot(xh, fl_ref[...], preferred_element_type=F32)
             + jnp.dot(xl, fh_ref[...], preferred_element_type=F32))
        for i in range(nrc):
            for k1 in range(n1):
                kf_ref[o, i, k1 * rc:(k1 + 1) * rc, :] = z[k1 * cb + i * rc:k1 * cb + (i + 1) * rc, :]


def _hyena_filters(seq, w1, b1, w2, b2, w3, freq, tables):
    n1 = 2 * seq // DFT2
    n = 2 * seq
    cb = HY_CB
    rc = _fft_rows(n1, cb)
    fh, fl, _, _, tw = tables
    pos = np.arange(n, dtype=np.float64)
    pos = np.where(pos < seq, pos, n - pos)
    tt = pos / max(seq - 1, 1)
    omega = 2.0 * math.pi * pos / seq
    bands = np.linspace(1e-4, HY_BANDS - 1, HY_BANDS)
    ang = omega[None, :] * bands[:, None]
    feats = np.concatenate([tt[None, :], np.cos(ang), np.sin(ang)], axis=0)
    nfeat = LANES
    feats = np.pad(feats, ((0, nfeat - feats.shape[0]), (0, 0))).astype(np.float32)
    w1 = jnp.pad(w1, ((0, nfeat - w1.shape[0]), (0, 0)))
    w3t = w3.T.reshape(4, D_HY, HY_FH)
    kern = functools.partial(_filter_kernel, seq=seq, n1=n1, cb=cb, rc=rc)
    full = lambda *shape: pl.BlockSpec(shape, lambda i: (0,) * len(shape))
    return pl.pallas_call(
        kern,
        grid=(D_HY // cb,),
        in_specs=[full(nfeat, n), full(HY_FH, nfeat), full(HY_FH, 1), full(HY_FH, HY_FH), full(HY_FH, 1),
                  full(HY_FH, 2),
                  pl.BlockSpec((4, cb, HY_FH), lambda i: (0, i, 0)),
                  full(2 * DFT2, 2 * DFT2), full(2 * DFT2, 2 * DFT2), full(n1 * rc, 2 * DFT2)],
        out_specs=pl.BlockSpec((2, cb // rc, n1 * rc, 2 * DFT2), lambda i: (0, i, 0, 0)),
        out_shape=jax.ShapeDtypeStruct((2, D_HY // rc, n1 * rc, 2 * DFT2), F32),
        scratch_shapes=[pltpu.VMEM((HY_FH, n), F32), pltpu.VMEM((2, cb, n), F32),
                        pltpu.VMEM((n1 * cb, 2 * DFT2), F32)],
        compiler_params=_cparams(("arbitrary",)),
        name="hyena_filters",
    )(jnp.asarray(feats), w1.T, b1.reshape(HY_FH, 1), w2.T, b2.reshape(HY_FH, 1), freq.T, w3t, fh, fl, tw)


HY_UNIT = 512


def _hyena_kernel(hy_ref, cw_ref, cb_ref, bias_ref, kf_ref, fh_ref, fl_ref, ih_ref, il_ref, tw_ref, o_ref,
                  buf, sa, sb, ta, tc, *, seq, n1, cb, nb, rc, pu):
    npairs = nb // 2
    nin = n1 // 2
    nrc = cb // rc
    upo = (npairs // pu) * nrc
    nitems = 2 * upo
    assert upo >= 3, "an item's second-order stage A must come after its first-order stage C"
    lane = lax.broadcasted_iota(jnp.int32, (rc, seq), 1)

    def conv3(h, part, r0):
        w0 = cw_ref[0, part, pl.ds(r0, rc), :]
        w1 = cw_ref[1, part, pl.ds(r0, rc), :]
        w2 = cw_ref[2, part, pl.ds(r0, rc), :]
        bb = cb_ref[part, pl.ds(r0, rc), :]
        hm = jnp.where(lane == 0, 0.0, pltpu.roll(h, 1, axis=1))
        hp = jnp.where(lane == seq - 1, 0.0, pltpu.roll(h, seq - 1, axis=1))
        return w0 * hm + w1 * h + w2 * hp + bb

    def prep(i, _):
        b = i // nrc
        r0 = pl.multiple_of((i % nrc) * rc, rc)
        buf[0, b, pl.ds(r0, rc), :] = conv3(hy_ref[b, 0, pl.ds(r0, rc), :], 0, r0)
        return 0
    lax.fori_loop(0, nb * nrc, prep, 0, unroll=2)

    def item(j):
        o = j // upo
        u = j % upo
        return o, u // nrc, u % nrc

    def stage_a(j, slot):
        o, pg, rg = item(j)
        r0 = pl.multiple_of(rg * rc, rc)
        for pp in range(pu):
            p = pg * pu + pp
            def get(jb, p=p):
                return (buf[o, 2 * p, pl.ds(r0, rc), jb * LANES:(jb + 1) * LANES],
                        buf[o, 2 * p + 1, pl.ds(r0, rc), jb * LANES:(jb + 1) * LANES])

            def put(k1, z, pp=pp):
                re, im = z
                if k1 > 0:
                    twr = tw_ref[k1 * rc:(k1 + 1) * rc, 0:LANES]
                    twi = tw_ref[k1 * rc:(k1 + 1) * rc, LANES:2 * LANES]
                    re, im = re * twr - im * twi, re * twi + im * twr
                rows = pl.ds((pp * n1 + k1) * rc, rc)
                sa[slot, rows, 0:LANES] = re
                sa[slot, rows, LANES:2 * LANES] = im

            if n1 == 64:
                _fft64(get, put, -1, ta, rc, nin, n1)
            else:
                zs = _cfft([get(jb) if jb < nin else None for jb in range(n1)], -1)
                for k1 in range(n1):
                    put(k1, zs[k1])

    def stage_b(j, slot):
        o, _, rg = item(j)
        z = _dft_mm(sa[slot], fh_ref, fl_ref)
        kk = kf_ref[o, rg]
        if pu > 1:
            kk = jnp.concatenate([kk] * pu, axis=0)
        zr, zi = z[:, :LANES], z[:, LANES:]
        kr, ki = kk[:, :LANES], kk[:, LANES:]
        w = jnp.concatenate([zr * kr - zi * ki, zr * ki + zi * kr], axis=1)
        sb[slot] = _dft_mm(w, ih_ref, il_ref)

    def stage_c(j, slot):
        o, pg, rg = item(j)
        r0 = pl.multiple_of(rg * rc, rc)
        bias = bias_ref[o, pl.ds(r0, rc), :]
        for pp in range(pu):
            p = pg * pu + pp
            xg = [conv3(hy_ref[2 * p + q, 1 + o, pl.ds(r0, rc), :], 1 + o, r0) for q in range(2)]

            def get(k1, pp=pp):
                re = sb[slot, pl.ds((pp * n1 + k1) * rc, rc), 0:LANES]
                im = sb[slot, pl.ds((pp * n1 + k1) * rc, rc), LANES:2 * LANES]
                if k1 > 0:
                    twr = tw_ref[k1 * rc:(k1 + 1) * rc, 0:LANES]
                    twi = tw_ref[k1 * rc:(k1 + 1) * rc, LANES:2 * LANES]
                    re, im = re * twr + im * twi, im * twr - re * twi
                return (re, im)

            def put(jb, y, p=p, xg=xg):
                lanes = slice(jb * LANES, (jb + 1) * LANES)
                for q in range(2):
                    u = buf[o, 2 * p + q, pl.ds(r0, rc), lanes]
                    buf[o + 1, 2 * p + q, pl.ds(r0, rc), lanes] = xg[q][:, lanes] * (y[q] + u * bias)

            if n1 == 64:
                _fft64(get, put, +1, tc, rc, n1, nin)
            else:
                ys = _cfft([get(k1) for k1 in range(n1)], +1)
                for jb in range(nin):
                    put(jb, ys[jb])

    stage_a(0, 0)
    stage_a(1, 1)
    stage_b(0, 0)

    assert nitems % 2 == 0

    def steady(i, _):
        t = 2 + 2 * i
        stage_b(t - 1, 1)
        stage_a(t, 0)
        stage_c(t - 2, 0)
        stage_b(t, 0)
        stage_a(t + 1, 1)
        stage_c(t - 1, 1)
        return 0
    lax.fori_loop(0, (nitems - 2) // 2, steady, 0)
    stage_b(nitems - 1, (nitems - 1) % 2)
    stage_c(nitems - 2, nitems % 2)
    stage_c(nitems - 1, (nitems - 1) % 2)
    for b in range(nb):
        o_ref[b] = buf[2, b]


def _hyena(hyt, cw, cbias, bias, kf, tables):
    nb, _, seq = hyt.shape
    n1 = 2 * seq // DFT2
    cb = HY_CB
    rc = _fft_rows(n1, cb)
    pu = HY_UNIT // (n1 * rc)
    nblk = D_HY // cb
    fh, fl, ih, il, tw = tables
    kern = functools.partial(_hyena_kernel, seq=seq, n1=n1, cb=cb, nb=nb, rc=rc, pu=pu)
    full = lambda *shape: pl.BlockSpec(shape, lambda i: (0,) * len(shape))
    return pl.pallas_call(
        kern,
        grid=(nblk,),
        in_specs=[pl.BlockSpec((nb, 3, cb, seq), lambda i: (0, 0, i, 0)),
                  pl.BlockSpec((3, 3, cb, 1), lambda i: (0, 0, i, 0)),
                  pl.BlockSpec((3, cb, 1), lambda i: (0, i, 0)),
                  pl.BlockSpec((2, cb, 1), lambda i: (0, i, 0)),
                  pl.BlockSpec((2, cb // rc, n1 * rc, 2 * DFT2), lambda i: (0, i, 0, 0)),
                  full(2 * DFT2, 2 * DFT2), full(2 * DFT2, 2 * DFT2),
                  full(2 * DFT2, 2 * DFT2), full(2 * DFT2, 2 * DFT2),
                  full(n1 * rc, 2 * DFT2)],
        out_specs=pl.BlockSpec((nb, cb, seq), lambda i: (0, i, 0)),
        out_shape=jax.ShapeDtypeStruct((nb, D_HY, seq), F32),
        scratch_shapes=[pltpu.VMEM((3, nb, cb, seq), F32),
                        pltpu.VMEM((2, HY_UNIT, 2 * DFT2), F32), pltpu.VMEM((2, HY_UNIT, 2 * DFT2), F32),
                        pltpu.VMEM((n1 * rc, 2 * DFT2), F32), pltpu.VMEM((n1 * rc, 2 * DFT2), F32)],
        compiler_params=_cparams(("parallel",)),
        name="hyena",
    )(hyt.reshape(nb, 3, D_HY, seq), cw, cbias, bias, kf, fh, fl, ih, il, tw)


FF_TM = 512
FF_SUB = 2 * LANES
FF_HALO_LANES = LANES


def _mix_ffn_kernel(x_ref, xp_ref, xq_ref, or_ref, orp_ref, orq_ref, oh_ref, ohp_ref, ohq_ref, mod_ref, gr_ref, gh_ref,
                    wr_ref, wh_ref, g2_ref, wu_ref, cw_ref, cb_ref, wd_ref, gf_ref, o_ref, x1_s, xe_s, h_s,
                    *, seg, halo, on_grid, tiles_per_seq, final_norm):
    t = pl.program_id(0)
    tm = x_ref.shape[1]
    rows = tm + 2 * halo

    def mixer_out(x, o_r, oh, lo, hi):
        orn = (_rms(o_r) * gr_ref[...]).astype(BF16)
        ohn = oh * lax.rsqrt(jnp.mean(oh * oh, axis=0, keepdims=True) + EPS) * gh_ref[...]
        o = lax.dot_general(ohn.astype(BF16), wh_ref[...], (((0,), (0,)), ((), ())), preferred_element_type=F32)
        o = jnp.dot(orn, wr_ref[...], preferred_element_type=F32) + o[lo:hi]
        return x + mod_ref[0, 2:3, :] * o

    nseq = oh_ref.shape[0]
    sl = tm // nseq
    for q in range(nseq):
        r = slice(q * sl, (q + 1) * sl)
        x1_s[halo + q * sl:halo + (q + 1) * sl, :] = mixer_out(x_ref[0, r, :], or_ref[0, r, :], oh_ref[q], 0, sl)
    if halo:
        hl = FF_HALO_LANES
        x1_s[0:halo, :] = mixer_out(xp_ref[0], orp_ref[0], ohp_ref[0], hl - halo, hl)
        x1_s[halo + tm:rows, :] = mixer_out(xq_ref[0], orq_ref[0], ohq_ref[0], 0, halo)
    x1 = x1_s[...]
    xn = _rms(x1) * g2_ref[...]
    xe_s[...] = (xn * (1.0 + mod_ref[0, 4:5, :]) + mod_ref[0, 3:4, :]).astype(BF16)
    if halo:
        @pl.when((t % tiles_per_seq) == 0)
        def _():
            xe_s[0:halo, :] = jnp.zeros((halo, D_MODEL), BF16)

        @pl.when((t % tiles_per_seq) == tiles_per_seq - 1)
        def _():
            xe_s[halo + tm:rows, :] = jnp.zeros((halo, D_MODEL), BF16)

    pos = lax.broadcasted_iota(jnp.int32, (rows, 1), 0) % seg
    for c0 in range(0, D_FF, FF_SUB):
        w = min(FF_SUB, D_FF - c0)
        g = jnp.dot(xe_s[...], wu_ref[:, D_FF + c0:D_FF + c0 + w], preferred_element_type=F32)
        a = jnp.dot(xe_s[halo:halo + tm, :], wu_ref[:, c0:c0 + w], preferred_element_type=F32)
        gls = pltpu.roll(jnp.where(pos == seg - 1, 0.0, g), 1, axis=0)
        grs = pltpu.roll(jnp.where(pos == 0, 0.0, g), rows - 1, axis=0)
        acc = cb_ref[:, c0:c0 + w]
        for dr in ((-1, 0, 1) if on_grid else (0,)):
            lo = halo + dr * seg
            acc = (acc + cw_ref[dr + 1, 0:1, c0:c0 + w] * gls[lo:lo + tm]
                   + cw_ref[dr + 1, 1:2, c0:c0 + w] * g[lo:lo + tm]
                   + cw_ref[dr + 1, 2:3, c0:c0 + w] * grs[lo:lo + tm])
        h_s[:, c0:c0 + w] = (jax.nn.gelu(acc) * a).astype(BF16)
    y = jnp.dot(h_s[...], wd_ref[...], preferred_element_type=F32)
    x2 = x1_s[halo:halo + tm, :] + mod_ref[0, 5:6, :] * y
    if final_norm:
        x2 = _rms(x2) * gf_ref[...]
    o_ref[0] = x2


def _mix_ffn(x, o_r, oht, mod, p, gf, on_grid, final_norm):
    b, l, _ = x.shape
    tm = FF_TM
    if on_grid:
        seg, halo = GRID_W, GRID_W
        tps = l // tm
        nseq = 1
    else:
        seg, halo = l, 0
        tps = 1
        nseq = tm // l
    nt = b * l // tm
    hb = GRID_W
    nhb = tm // hb
    last_hb = b * l // hb - 1
    hl = FF_HALO_LANES
    nhl = tm // hl
    last_hl = l // hl - 1
    mod_map = (lambda t: (t // tps, 0, 0)) if mod.shape[0] > 1 else (lambda t: (0, 0, 0))
    kern = functools.partial(_mix_ffn_kernel, seg=seg, halo=halo, on_grid=on_grid, tiles_per_seq=tps,
                             final_norm=final_norm)
    const = lambda *shape: pl.BlockSpec(shape, lambda t: (0,) * len(shape), pipeline_mode=pl.Buffered(1))
    prev_rows = lambda t: (jnp.maximum(t * nhb - 1, 0), 0, 0)
    next_rows = lambda t: (jnp.minimum((t + 1) * nhb, last_hb), 0, 0)
    out = pl.pallas_call(
        kern,
        grid=(nt,),
        in_specs=[pl.BlockSpec((1, tm, D_MODEL), lambda t: (t, 0, 0)),
                  pl.BlockSpec((1, hb, D_MODEL), prev_rows),
                  pl.BlockSpec((1, hb, D_MODEL), next_rows),
                  pl.BlockSpec((1, tm, D_RNN), lambda t: (t, 0, 0)),
                  pl.BlockSpec((1, hb, D_RNN), prev_rows),
                  pl.BlockSpec((1, hb, D_RNN), next_rows),
                  pl.BlockSpec((nseq, D_HY, tm // nseq), lambda t: (t // tps, 0, t % tps)),
                  pl.BlockSpec((1, D_HY, hl), lambda t: (t // tps, 0, jnp.maximum((t % tps) * nhl - 1, 0))),
                  pl.BlockSpec((1, D_HY, hl), lambda t: (t // tps, 0, jnp.minimum((t % tps + 1) * nhl, last_hl))),
                  pl.BlockSpec((1, N_MOD, D_MODEL), mod_map),
                  const(1, D_RNN), const(D_HY, 1), const(D_RNN, D_MODEL), const(D_HY, D_MODEL), const(1, D_MODEL),
                  const(D_MODEL, 2 * D_FF), const(3, 3, D_FF), const(1, D_FF), const(D_FF, D_MODEL), const(1, D_MODEL)],
        out_specs=pl.BlockSpec((1, tm, D_MODEL), lambda t: (t, 0, 0)),
        out_shape=jax.ShapeDtypeStruct((nt, tm, D_MODEL), F32),
        scratch_shapes=[pltpu.VMEM((tm + 2 * halo, D_MODEL), F32), pltpu.VMEM((tm + 2 * halo, D_MODEL), BF16),
                        pltpu.VMEM((tm, D_FF), BF16)],
        compiler_params=_cparams(("parallel",)),
        name="mix_ffn",
    )(x.reshape(nt, tm, D_MODEL), x.reshape(b * l // hb, hb, D_MODEL), x.reshape(b * l // hb, hb, D_MODEL),
      o_r.reshape(nt, tm, D_RNN), o_r.reshape(b * l // hb, hb, D_RNN), o_r.reshape(b * l // hb, hb, D_RNN),
      oht, oht, oht, mod, p['g_rnn_out'], p['g_hy_out'], p['w_out_r'], p['w_out_h'], p['g_norm2'],
      p['w_up'], p['ffn_conv_w'], p['ffn_conv_b'], p['w_down'], gf)
    return out.reshape(b, l, D_MODEL)


def _trunk_layer(x, mod, h0, p, kf, tables, on_grid, final_norm, g_final):
    b, l, _ = x.shape
    tm = min(512, l)
    xy, hyt = _in_proj(x, mod, p['g_norm1'], p['wxy'], p['whyt'], tm)
    o_r, states = _rglru(xy, p['rg_conv_w'], p['rg_conv_b'], p['wg'], p['bg'], p['ap'], h0)
    oht = _hyena(hyt, p['hy_cw'], p['hy_cb'], p['hy_bias'], kf, tables)
    x2 = _mix_ffn(x, o_r, oht, mod, p, g_final, on_grid, final_norm)
    return x2, states


def kernel(x_prompt, x_sample, state_rglru, c, c_ctx, w_ada, b_ada, g_norm1, g_norm2, w_in, rg_conv_w, rg_conv_b, rg_gate_w, rg_gate_b, rg_a, hy_conv_w, hy_conv_b, hf_w1, hf_b1, hf_w2, hf_b2, hf_w3, hf_freq, hy_bias, g_rnn_out, g_hy_out, w_out, w_up, ffn_conv_w, ffn_conv_b, w_down, g_final):
    depth = w_in.shape[0]
    nb_ctx, l_ctx, _ = x_prompt.shape
    nb_lat, l_lat, _ = x_sample.shape

    cc = jnp.zeros((SUBLANES, D_MODEL), F32).at[0].set(c_ctx).at[1:1 + nb_lat].set(c)
    mods = _modulation(cc, w_ada, b_ada).reshape(depth, SUBLANES, N_MOD, D_MODEL)

    tab_ctx = _dft_tables(2 * l_ctx // DFT2, _fft_rows(2 * l_ctx // DFT2, HY_CB))
    tab_lat = _dft_tables(2 * l_lat // DFT2, _fft_rows(2 * l_lat // DFT2, HY_CB))
    gf = g_final.reshape(1, D_MODEL)
    zero_h = jnp.zeros((nb_ctx, 2, D_RNN), F32)

    xp, xs = x_prompt, x_sample
    new_states = []
    for l in range(depth):
        wg, bg, ap = _gate_weights(rg_gate_w[l], rg_gate_b[l], rg_a[l])
        p = {
            'g_norm1': g_norm1[l].reshape(1, D_MODEL), 'g_norm2': g_norm2[l].reshape(1, D_MODEL),
            'wxy': w_in[l, :, :2 * D_RNN].astype(BF16), 'whyt': w_in[l, :, 2 * D_RNN:].T.astype(BF16),
            'rg_conv_w': rg_conv_w[l], 'rg_conv_b': rg_conv_b[l].reshape(1, D_RNN),
            'wg': wg, 'bg': bg, 'ap': ap,
            'hy_cw': hy_conv_w[l].reshape(3, 3, D_HY, 1), 'hy_cb': hy_conv_b[l].reshape(3, D_HY, 1),
            'hy_bias': hy_bias[l].reshape(2, D_HY, 1),
            'g_rnn_out': g_rnn_out[l].reshape(1, D_RNN), 'g_hy_out': g_hy_out[l].reshape(D_HY, 1),
            'w_out_r': w_out[l, :D_RNN].astype(BF16), 'w_out_h': w_out[l, D_RNN:].astype(BF16),
            'w_up': w_up[l].astype(BF16), 'ffn_conv_w': ffn_conv_w[l], 'ffn_conv_b': ffn_conv_b[l].reshape(1, D_FF),
            'w_down': w_down[l].astype(BF16),
        }
        fargs = (hf_w1[l], hf_b1[l], hf_w2[l], hf_b2[l], hf_w3[l], hf_freq[l])
        kf_ctx = _hyena_filters(l_ctx, *fargs, tab_ctx)
        kf_lat = _hyena_filters(l_lat, *fargs, tab_lat)
        final = l == depth - 1
        xp, st = _trunk_layer(xp, mods[l, 0:1], zero_h, p, kf_ctx, tab_ctx, False, final, gf)
        new_states.append(st)
        xs, _ = _trunk_layer(xs, mods[l, 1:1 + nb_lat], state_rglru[:, l], p, kf_lat, tab_lat, True, final, gf)
    return (xp, xs, jnp.stack(new_states, axis=1))
```

```python
import functools
import math

import jax
import jax.numpy as jnp
import ml_dtypes
import numpy as np
from jax import lax
from jax.experimental import pallas as pl
from jax.experimental.pallas import tpu as pltpu

F32 = jnp.float32
BF16 = jnp.bfloat16

D_MODEL = 1024
D_RNN = 512
D_HY = 512
N_HEADS = 8
HEAD_DIM = D_RNN // N_HEADS
RG_C = 8.0
GRID_W = 64
HY_BANDS = 16
HY_FH = 64
D_FF = 2816
N_MOD = 6
EPS = 1e-6

SUBLANES = 8
LANES = 128
VMEM_LIMIT = 56 * 1024 * 1024

DFT2 = LANES
FFT_PASSES = 2


def _cparams(sem):
    return pltpu.CompilerParams(dimension_semantics=sem, vmem_limit_bytes=VMEM_LIMIT)


def _rms(x):
    return x * lax.rsqrt(jnp.mean(x * x, axis=-1, keepdims=True) + EPS)


def _mod_kernel(c_ref, w_ref, b_ref, o_ref):
    c = c_ref[...]
    s = c * jax.nn.sigmoid(c)
    o_ref[0] = jnp.dot(s, w_ref[0], precision=lax.Precision.HIGHEST, preferred_element_type=F32) + b_ref[0]


def _modulation(cc, w_ada, b_ada):
    depth, _, n = w_ada.shape
    tn = 1536
    return pl.pallas_call(
        _mod_kernel,
        grid=(depth, n // tn),
        in_specs=[pl.BlockSpec((SUBLANES, D_MODEL), lambda l, j: (0, 0)),
                  pl.BlockSpec((1, D_MODEL, tn), lambda l, j: (l, 0, j)),
                  pl.BlockSpec((1, 1, tn), lambda l, j: (l, 0, j))],
        out_specs=pl.BlockSpec((1, SUBLANES, tn), lambda l, j: (l, 0, j)),
        out_shape=jax.ShapeDtypeStruct((depth, SUBLANES, n), F32),
        compiler_params=_cparams(("parallel", "parallel")),
        name="adaln_mod",
    )(cc, w_ada, b_ada.reshape(depth, 1, n))


def _in_proj_kernel(x_ref, mod_ref, g_ref, wxy_ref, whyt_ref, xy_ref, hyt_ref):
    x = x_ref[0]
    xn = _rms(x) * g_ref[...]
    xn = (xn * (1.0 + mod_ref[0, 1:2, :]) + mod_ref[0, 0:1, :]).astype(BF16)
    xy_ref[0] = jnp.dot(xn, wxy_ref[...], preferred_element_type=F32)
    hyt_ref[0] = lax.dot_general(whyt_ref[...], xn, (((1,), (1,)), ((), ())), preferred_element_type=F32)


def _in_proj(x, mod, g, wxy, whyt, tm):
    b, l, _ = x.shape
    per_batch_mod = mod.shape[0] > 1
    mod_map = (lambda i, j: (i, 0, 0)) if per_batch_mod else (lambda i, j: (0, 0, 0))
    return pl.pallas_call(
        _in_proj_kernel,
        grid=(b, l // tm),
        in_specs=[pl.BlockSpec((1, tm, D_MODEL), lambda i, j: (i, j, 0)),
                  pl.BlockSpec((1, N_MOD, D_MODEL), mod_map),
                  pl.BlockSpec((1, D_MODEL), lambda i, j: (0, 0)),
                  pl.BlockSpec((D_MODEL, 2 * D_RNN), lambda i, j: (0, 0)),
                  pl.BlockSpec((3 * D_HY, D_MODEL), lambda i, j: (0, 0))],
        out_specs=[pl.BlockSpec((1, tm, 2 * D_RNN), lambda i, j: (i, j, 0)),
                   pl.BlockSpec((1, 3 * D_HY, tm), lambda i, j: (i, 0, j))],
        out_shape=[jax.ShapeDtypeStruct((b, l, 2 * D_RNN), F32),
                   jax.ShapeDtypeStruct((b, 3 * D_HY, l), F32)],
        compiler_params=_cparams(("parallel", "parallel")),
        name="in_proj",
    )(x, mod, g, wxy, whyt)


RG_HALF = D_RNN // 2
RG_TILES = RG_HALF // LANES


def _rglru_kernel(xr_ref, yr_ref, cw_ref, cb_ref, wg_ref, bg_ref, ap_ref, h0_ref, o_ref, st_ref,
                  ext, xc_s, hf, hb, a_s, b_s, *, seq, tc):
    nchunks = seq // tc
    nblk = tc // SUBLANES
    ext[0:SUBLANES, :] = jnp.zeros((SUBLANES, RG_HALF), F32)
    ext[SUBLANES:SUBLANES + seq, :] = xr_ref[0]
    ext[SUBLANES + seq:2 * SUBLANES + seq, :] = jnp.zeros((SUBLANES, RG_HALF), F32)
    for c in range(nchunks):
        xc = cb_ref[...] + cw_ref[0:1, :] * ext[pl.ds(c * tc + SUBLANES - 2, tc), :]
        for k in range(1, 4):
            xc = xc + cw_ref[k:k + 1, :] * ext[pl.ds(c * tc + SUBLANES - 2 + k, tc), :]
        xc_s[pl.ds(c * tc, tc), :] = xc

    row = lax.broadcasted_iota(jnp.int32, (SUBLANES, LANES), 0)
    half_neg_c_sp = [(-0.5 * RG_C) * jax.nn.softplus(-ap_ref[d, 0]) for d in range(2)]

    def gates(c0, d):
        xc = xc_s[pl.ds(c0, tc), :]
        g = jnp.dot(xc.astype(BF16), wg_ref[d, 0], preferred_element_type=F32) + bg_ref[d, 0]
        i = 0.5 * jnp.tanh(g[:, RG_HALF:]) + 0.5
        log_a = half_neg_c_sp[d] * jnp.tanh(g[:, :RG_HALF]) + half_neg_c_sp[d]
        a = jnp.exp(log_a)
        y = jnp.tanh(-log_a) * (a * a + 1.0)
        mult = jnp.where(y > 0.0, y * lax.rsqrt(y), 0.0)
        a_s[d] = a
        b_s[d] = xc * i * mult

    def local_scan(a, b, reverse):
        for s in (1, 2, 4):
            if reverse:
                keep = row < SUBLANES - s
                shift = SUBLANES - s
            else:
                keep = row >= s
                shift = s
            a_sh = jnp.where(keep, pltpu.roll(a, shift, axis=0), 1.0)
            b_sh = jnp.where(keep, pltpu.roll(b, shift, axis=0), 0.0)
            b = a * b_sh + b
            a = a * a_sh
        return a, b

    def block_body(cf0, cb0):
        def body(j, carry):
            rf = pl.multiple_of(j * SUBLANES, SUBLANES)
            rb = pl.multiple_of(tc - SUBLANES - j * SUBLANES, SUBLANES)
            out = []
            for t in range(RG_TILES):
                lanes = slice(t * LANES, (t + 1) * LANES)
                pa, pb = local_scan(a_s[0, pl.ds(rf, SUBLANES), lanes], b_s[0, pl.ds(rf, SUBLANES), lanes], False)
                h = pa * carry[2 * t] + pb
                hf[pl.ds(cf0 + rf, SUBLANES), lanes] = h
                out.append(jnp.broadcast_to(h[SUBLANES - 1:SUBLANES, :], (SUBLANES, LANES)))
                pa, pb = local_scan(a_s[1, pl.ds(rb, SUBLANES), lanes], b_s[1, pl.ds(rb, SUBLANES), lanes], True)
                h = pa * carry[2 * t + 1] + pb
                hb[pl.ds(cb0 + rb, SUBLANES), lanes] = h
                out.append(jnp.broadcast_to(h[0:1, :], (SUBLANES, LANES)))
            return tuple(out)
        return body

    carry = []
    for t in range(RG_TILES):
        carry.append(jnp.broadcast_to(h0_ref[0, 0:1, t * LANES:(t + 1) * LANES], (SUBLANES, LANES)))
        carry.append(jnp.broadcast_to(h0_ref[0, 1:2, t * LANES:(t + 1) * LANES], (SUBLANES, LANES)))
    carry = tuple(carry)
    for c in range(nchunks):
        cf0 = c * tc
        cb0 = (nchunks - 1 - c) * tc
        gates(cf0, 0)
        gates(cb0, 1)
        carry = lax.fori_loop(0, nblk, block_body(cf0, cb0), carry, unroll=2)
    for t in range(RG_TILES):
        st_ref[0, 0:1, t * LANES:(t + 1) * LANES] = carry[2 * t][0:1, :]
        st_ref[0, 1:2, t * LANES:(t + 1) * LANES] = carry[2 * t + 1][0:1, :]
    for c in range(nchunks):
        rows = pl.ds(c * tc, tc)
        o_ref[0, rows, :] = (hf[rows, :] + hb[rows, :]) * jax.nn.gelu(yr_ref[0, rows, :])


def _rglru(xy, cw, cb, wg, bg, ap, h0):
    b, seq, _ = xy.shape
    tc = min(512, seq)
    nh = D_RNN // RG_HALF
    kern = functools.partial(_rglru_kernel, seq=seq, tc=tc)
    return pl.pallas_call(
        kern,
        grid=(b, nh),
        in_specs=[pl.BlockSpec((1, seq, RG_HALF), lambda i, h: (i, 0, h)),
                  pl.BlockSpec((1, seq, RG_HALF), lambda i, h: (i, 0, nh + h)),
                  pl.BlockSpec((4, RG_HALF), lambda i, h: (0, h)),
                  pl.BlockSpec((1, RG_HALF), lambda i, h: (0, h)),
                  pl.BlockSpec((2, 1, RG_HALF, 2 * RG_HALF), lambda i, h: (0, h, 0, 0)),
                  pl.BlockSpec((2, 1, 1, 2 * RG_HALF), lambda i, h: (0, h, 0, 0)),
                  pl.BlockSpec((2, 1, 1, RG_HALF), lambda i, h: (0, h, 0, 0)),
                  pl.BlockSpec((1, 2, RG_HALF), lambda i, h: (i, 0, h))],
        out_specs=[pl.BlockSpec((1, seq, RG_HALF), lambda i, h: (i, 0, h)),
                   pl.BlockSpec((1, 2, RG_HALF), lambda i, h: (i, 0, h))],
        out_shape=[jax.ShapeDtypeStruct((b, seq, D_RNN), F32),
                   jax.ShapeDtypeStruct((b, 2, D_RNN), F32)],
        scratch_shapes=[pltpu.VMEM((seq + 2 * SUBLANES, RG_HALF), F32),
                        pltpu.VMEM((seq, RG_HALF), F32),
                        pltpu.VMEM((seq, RG_HALF), F32),
                        pltpu.VMEM((seq, RG_HALF), F32),
                        pltpu.VMEM((2, tc, RG_HALF), F32),
                        pltpu.VMEM((2, tc, RG_HALF), F32)],
        compiler_params=_cparams(("parallel", "parallel")),
        name="rglru",
    )(xy, xy, cw, cb, wg, bg, ap, h0)


def _gate_weights(gate_w, gate_b, a_param):
    nh = D_RNN // RG_HALF
    hp = N_HEADS // nh
    w = gate_w.reshape(2, 2, nh, hp, HEAD_DIM, HEAD_DIM)
    eye = jnp.eye(hp, dtype=F32)
    dense = jnp.einsum('dghpio,pq->dhpigqo', w, eye)
    dense = (0.5 * dense).reshape(2, nh, RG_HALF, 2 * RG_HALF).astype(BF16)
    bias = 0.5 * gate_b.reshape(2, 2, nh, RG_HALF).transpose(0, 2, 1, 3).reshape(2, nh, 1, 2 * RG_HALF)
    ap = a_param.reshape(2, nh, 1, RG_HALF)
    return dense, bias, ap


def _add(a, b):
    if a is None:
        return b
    if b is None:
        return a
    return a + b


def _sub(a, b):
    if b is None:
        return a
    if a is None:
        return -b
    return a - b


def _scale(a, s):
    if a is None or s == 0.0:
        return None
    if s == 1.0:
        return a
    if s == -1.0:
        return -a
    return a * s


def _cmul_const(z, wr, wi):
    if z is None:
        return None
    re, im = z
    if abs(wr) < 1e-15:
        wr = 0.0
    if abs(wi) < 1e-15:
        wi = 0.0
    if wr != 0.0 and abs(abs(wr) - abs(wi)) < 1e-15 and re is not None and im is not None:
        sr, si = math.copysign(1.0, wr), math.copysign(1.0, wi)
        return (_scale(_sub(_scale(re, sr), _scale(im, si)), abs(wr)), _scale(_add(_scale(re, si), _scale(im, sr)), abs(wr)))
    return (_sub(_scale(re, wr), _scale(im, wi)), _add(_scale(re, wi), _scale(im, wr)))


def _cadd(a, b):
    if a is None:
        return b
    if b is None:
        return a
    return (_add(a[0], b[0]), _add(a[1], b[1]))


def _csub(a, b):
    if b is None:
        return a
    if a is None:
        return (_sub(None, b[0]), _sub(None, b[1]))
    return (_sub(a[0], b[0]), _sub(a[1], b[1]))


def _cfft(xs, sign):
    n = len(xs)
    if n == 1:
        return list(xs)
    even = _cfft(xs[0::2], sign)
    odd = _cfft(xs[1::2], sign)
    out = [None] * n
    for k in range(n // 2):
        ang = sign * 2.0 * math.pi * k / n
        t = _cmul_const(odd[k], math.cos(ang), math.sin(ang))
        out[k] = _cadd(even[k], t)
        out[k + n // 2] = _csub(even[k], t)
    return out


def _fft64(get, put, sign, t_ref, rc, nin, nout):
    r8 = 8
    for p in range(r8):
        if sign < 0:
            xs = [get(r8 * a + p) if r8 * a + p < nin else None for a in range(r8)]
        else:
            xs = [get(p + r8 * d) for d in range(r8)]
        ts = _cfft(xs, sign)
        for q in range(r8):
            ang = sign * 2.0 * math.pi * p * q / 64.0
            re, im = _cmul_const(ts[q], math.cos(ang), math.sin(ang))
            rows = pl.ds((p * r8 + q) * rc, rc)
            t_ref[rows, 0:LANES] = re if re is not None else jnp.zeros_like(im)
            t_ref[rows, LANES:2 * LANES] = im if im is not None else jnp.zeros_like(re)
    for q in range(r8):
        ys = [(t_ref[pl.ds((p * r8 + q) * rc, rc), 0:LANES], t_ref[pl.ds((p * r8 + q) * rc, rc), LANES:2 * LANES])
              for p in range(r8)]
        zs = _cfft(ys, sign)
        for m in range(r8):
            k = q + r8 * m if sign < 0 else r8 * m + q
            if k < nout:
                put(k, zs[m])


def _dft_tables(n1, rc):
    n = n1 * DFT2
    k = np.arange(DFT2, dtype=np.float64)
    ang = 2.0 * np.pi * np.outer(k, k) / DFT2
    c, s = np.cos(ang), np.sin(ang)
    fwd = np.block([[c, -s], [s, c]])
    inv = np.block([[c, s], [-s, c]]) / n

    def split(m):
        hi = m.astype(np.float32).astype(ml_dtypes.bfloat16)
        lo = (m - hi.astype(np.float64)).astype(ml_dtypes.bfloat16)
        return jnp.asarray(hi), jnp.asarray(lo)

    tw_ang = 2.0 * np.pi * np.outer(np.arange(n1, dtype=np.float64), k) / n
    tw = np.concatenate([np.cos(tw_ang), -np.sin(tw_ang)], axis=1)
    tw = np.repeat(tw, rc, axis=0).astype(np.float32)
    return split(fwd) + split(inv) + (jnp.asarray(tw),)


HY_CB = 32


def _fft_rows(n1, cb):
    return SUBLANES if n1 >= 16 else cb


def _dft_mm(x, hi_ref, lo_ref):
    xh = x.astype(BF16)
    acc = jnp.dot(xh, hi_ref[...], preferred_element_type=F32)
    if FFT_PASSES >= 2:
        acc = acc + jnp.dot(xh, lo_ref[...], preferred_element_type=F32)
    if FFT_PASSES >= 3:
        xl = (x - xh.astype(F32)).astype(BF16)
        acc = acc + jnp.dot(xl, hi_ref[...], preferred_element_type=F32)
    return acc


def _filter_kernel(feat_ref, w1_ref, b1_ref, w2_ref, b2_ref, fr_ref, w3_ref, fh_ref, fl_ref, tw_ref, kf_ref,
                   h2h_s, h2l_s, k_s, s_s, t_s, *, seq, n1, cb, rc):
    n = 2 * seq
    hi = lax.Precision.HIGHEST

    @pl.when(pl.program_id(0) == 0)
    def _():
        h = jnp.dot(w1_ref[...], feat_ref[...], precision=hi, preferred_element_type=F32) + b1_ref[...]
        h = jnp.sin(fr_ref[:, 0:1] * h)
        h = jnp.dot(w2_ref[...], h, precision=hi, preferred_element_type=F32) + b2_ref[...]
        h2 = jnp.sin(fr_ref[:, 1:2] * h)
        h2h = h2.astype(BF16)
        h2h_s[...] = h2h
        h2l_s[...] = (h2 - h2h.astype(F32)).astype(BF16)

    lane = lax.broadcasted_iota(jnp.int32, (cb, n), 1)
    pos = jnp.where(lane < seq, lane, n - lane).astype(F32)
    t = pos / float(max(seq - 1, 1))
    d_idx = (lax.broadcasted_iota(jnp.int32, (cb, n), 0) + pl.program_id(0) * cb).astype(F32)
    min_decay = math.log(1e-2) / 1.5
    max_decay = math.log(1e-2) / 0.3
    delta = jnp.abs(min_decay + d_idx * ((max_decay - min_decay) / (D_HY - 1)))
    decay = jnp.exp(-t * delta)
    w3 = w3_ref[...].reshape(4 * cb, HY_FH)
    w3h = w3.astype(BF16)
    w3l = (w3 - w3h.astype(F32)).astype(BF16)
    hfb = (jnp.dot(w3h, h2h_s[...], preferred_element_type=F32) + jnp.dot(w3h, h2l_s[...], preferred_element_type=F32)
           + jnp.dot(w3l, h2h_s[...], preferred_element_type=F32))
    for o in range(2):
        hf = hfb[(2 * o) * cb:(2 * o + 1) * cb]
        hb = hfb[(2 * o + 1) * cb:(2 * o + 2) * cb]
        k_s[o] = jnp.where(lane < seq, hf, jnp.where(lane == seq, 0.0, hb)) * decay

    nrc = cb // rc
    for o in range(2):
        def rows(i, _, o=o):
            r0 = pl.multiple_of(i * rc, rc)
            base = pl.multiple_of(i * (n1 * rc), rc)

            def get(j):
                return (k_s[o, pl.ds(r0, rc), j * LANES:(j + 1) * LANES], None)

            def put(k1, z):
                re, im = z
                if k1 > 0:
                    twr = tw_ref[k1 * rc:(k1 + 1) * rc, 0:LANES]
                    twi = tw_ref[k1 * rc:(k1 + 1) * rc, LANES:2 * LANES]
                    re, im = (re * twr, re * twi) if im is None else (re * twr - im * twi, re * twi + im * twr)
                s_s[pl.ds(base + k1 * rc, rc), 0:LANES] = re
                s_s[pl.ds(base + k1 * rc, rc), LANES:2 * LANES] = im if im is not None else jnp.zeros_like(re)

            if n1 == 64:
                _fft64(get, put, -1, t_s, rc, n1, n1)
            else:
                zs = _cfft([get(j) for j in range(n1)], -1)
                for k1 in range(n1):
                    put(k1, zs[k1])
            return 0
        lax.fori_loop(0, nrc, rows, 0)
        x = s_s[...]
        xh = x.astype(BF16)
        xl = (x - xh.astype(F32)).astype(BF16)
        z = (jnp.dot(xh, fh_ref[...], preferred_element_type=F32) + jnp.dot(xh, fl_ref[...], preferred_element_type=F32)
             + jnp.dot(xl, fh_ref[...], preferred_element_type=F32))
        for i in range(nrc):
            kf_ref[o, i] = z[i * n1 * rc:(i + 1) * n1 * rc, :]


def _hyena_filters(seq, w1, b1, w2, b2, w3, freq, tables):
    n1 = 2 * seq // DFT2
    n = 2 * seq
    cb = HY_CB
    rc = _fft_rows(n1, cb)
    fh, fl, _, _, tw = tables
    pos = np.arange(n, dtype=np.float64)
    pos = np.where(pos < seq, pos, n - pos)
    tt = pos / max(seq - 1, 1)
    omega = 2.0 * math.pi * pos / seq
    bands = np.linspace(1e-4, HY_BANDS - 1, HY_BANDS)
    ang = omega[None, :] * bands[:, None]
    feats = np.concatenate([tt[None, :], np.cos(ang), np.sin(ang)], axis=0)
    nfeat = LANES
    feats = np.pad(feats, ((0, nfeat - feats.shape[0]), (0, 0))).astype(np.float32)
    w1 = jnp.pad(w1, ((0, nfeat - w1.shape[0]), (0, 0)))
    w3t = w3.T.reshape(4, D_HY, HY_FH)
    kern = functools.partial(_filter_kernel, seq=seq, n1=n1, cb=cb, rc=rc)
    full = lambda *shape: pl.BlockSpec(shape, lambda i: (0,) * len(shape))
    return pl.pallas_call(
        kern,
        grid=(D_HY // cb,),
        in_specs=[full(nfeat, n), full(HY_FH, nfeat), full(HY_FH, 1), full(HY_FH, HY_FH), full(HY_FH, 1),
                  full(HY_FH, 2),
                  pl.BlockSpec((4, cb, HY_FH), lambda i: (0, i, 0)),
                  full(2 * DFT2, 2 * DFT2), full(2 * DFT2, 2 * DFT2), full(n1 * rc, 2 * DFT2)],
        out_specs=pl.BlockSpec((2, cb // rc, n1 * rc, 2 * DFT2), lambda i: (0, i, 0, 0)),
        out_shape=jax.ShapeDtypeStruct((2, D_HY // rc, n1 * rc, 2 * DFT2), F32),
        scratch_shapes=[pltpu.VMEM((HY_FH, n), BF16), pltpu.VMEM((HY_FH, n), BF16), pltpu.VMEM((2, cb, n), F32),
                        pltpu.VMEM((n1 * cb, 2 * DFT2), F32), pltpu.VMEM((n1 * rc, 2 * DFT2), F32)],
        compiler_params=_cparams(("arbitrary",)),
        name="hyena_filters",
    )(jnp.asarray(feats), w1.T, b1.reshape(HY_FH, 1), w2.T, b2.reshape(HY_FH, 1), freq.T, w3t, fh, fl, tw)


HY_UNIT = 512


def _hyena_kernel(hy_ref, cw_ref, cb_ref, bias_ref, kf_ref, fh_ref, fl_ref, ih_ref, il_ref, tw_ref, o_ref,
                  buf, sa, sb, ta, tc, *, seq, n1, cb, nb, rc, pu):
    npairs = nb // 2
    nin = n1 // 2
    nrc = cb // rc
    upo = (npairs // pu) * nrc
    nitems = 2 * upo
    assert upo >= 3, "an item's second-order stage A must come after its first-order stage C"
    lane = lax.broadcasted_iota(jnp.int32, (rc, seq), 1)

    def conv3(h, part, r0):
        w0 = cw_ref[0, part, pl.ds(r0, rc), :]
        w1 = cw_ref[1, part, pl.ds(r0, rc), :]
        w2 = cw_ref[2, part, pl.ds(r0, rc), :]
        bb = cb_ref[part, pl.ds(r0, rc), :]
        hm = jnp.where(lane == 0, 0.0, pltpu.roll(h, 1, axis=1))
        hp = jnp.where(lane == seq - 1, 0.0, pltpu.roll(h, seq - 1, axis=1))
        return w0 * hm + w1 * h + w2 * hp + bb

    def prep(i, _):
        b = i // nrc
        r0 = pl.multiple_of((i % nrc) * rc, rc)
        buf[0, b, pl.ds(r0, rc), :] = conv3(hy_ref[b, 0, pl.ds(r0, rc), :], 0, r0)
        return 0
    lax.fori_loop(0, nb * nrc, prep, 0, unroll=2)

    def item(j):
        o = j // upo
        u = j % upo
        return o, u // nrc, u % nrc

    def stage_a(j, slot):
        o, pg, rg = item(j)
        r0 = pl.multiple_of(rg * rc, rc)
        for pp in range(pu):
            p = pg * pu + pp
            def get(jb, p=p):
                return (buf[o, 2 * p, pl.ds(r0, rc), jb * LANES:(jb + 1) * LANES],
                        buf[o, 2 * p + 1, pl.ds(r0, rc), jb * LANES:(jb + 1) * LANES])

            def put(k1, z, pp=pp):
                re, im = z
                if k1 > 0:
                    twr = tw_ref[k1 * rc:(k1 + 1) * rc, 0:LANES]
                    twi = tw_ref[k1 * rc:(k1 + 1) * rc, LANES:2 * LANES]
                    re, im = re * twr - im * twi, re * twi + im * twr
                rows = pl.ds((pp * n1 + k1) * rc, rc)
                sa[slot, rows, 0:LANES] = re
                sa[slot, rows, LANES:2 * LANES] = im

            if n1 == 64:
                _fft64(get, put, -1, ta, rc, nin, n1)
            else:
                zs = _cfft([get(jb) if jb < nin else None for jb in range(n1)], -1)
                for k1 in range(n1):
                    put(k1, zs[k1])

    def stage_b(j, slot):
        o, _, rg = item(j)
        z = _dft_mm(sa[slot], fh_ref, fl_ref)
        kk = kf_ref[o, rg]
        if pu > 1:
            kk = jnp.concatenate([kk] * pu, axis=0)
        zr, zi = z[:, :LANES], z[:, LANES:]
        kr, ki = kk[:, :LANES], kk[:, LANES:]
        w = jnp.concatenate([zr * kr - zi * ki, zr * ki + zi * kr], axis=1)
        sb[slot] = _dft_mm(w, ih_ref, il_ref)

    def stage_c(j, slot):
        o, pg, rg = item(j)
        r0 = pl.multiple_of(rg * rc, rc)
        bias = bias_ref[o, pl.ds(r0, rc), :]
        for pp in range(pu):
            p = pg * pu + pp
            xg = [conv3(hy_ref[2 * p + q, 1 + o, pl.ds(r0, rc), :], 1 + o, r0) for q in range(2)]

            def get(k1, pp=pp):
                re = sb[slot, pl.ds((pp * n1 + k1) * rc, rc), 0:LANES]
                im = sb[slot, pl.ds((pp * n1 + k1) * rc, rc), LANES:2 * LANES]
                if k1 > 0:
                    twr = tw_ref[k1 * rc:(k1 + 1) * rc, 0:LANES]
                    twi = tw_ref[k1 * rc:(k1 + 1) * rc, LANES:2 * LANES]
                    re, im = re * twr + im * twi, im * twr - re * twi
                return (re, im)

            def put(jb, y, p=p, xg=xg):
                lanes = slice(jb * LANES, (jb + 1) * LANES)
                for q in range(2):
                    u = buf[o, 2 * p + q, pl.ds(r0, rc), lanes]
                    buf[o + 1, 2 * p + q, pl.ds(r0, rc), lanes] = xg[q][:, lanes] * (y[q] + u * bias)

            if n1 == 64:
                _fft64(get, put, +1, tc, rc, n1, nin)
            else:
                ys = _cfft([get(k1) for k1 in range(n1)], +1)
                for jb in range(nin):
                    put(jb, ys[jb])

    stage_a(0, 0)
    stage_a(1, 1)
    stage_b(0, 0)

    assert nitems % 2 == 0

    def steady(i, _):
        t = 2 + 2 * i
        stage_b(t - 1, 1)
        stage_a(t, 0)
        stage_c(t - 2, 0)
        stage_b(t, 0)
        stage_a(t + 1, 1)
        stage_c(t - 1, 1)
        return 0
    lax.fori_loop(0, (nitems - 2) // 2, steady, 0)
    stage_b(nitems - 1, (nitems - 1) % 2)
    stage_c(nitems - 2, nitems % 2)
    stage_c(nitems - 1, (nitems - 1) % 2)
    for b in range(nb):
        o_ref[b] = buf[2, b]


def _hyena(hyt, cw, cbias, bias, kf, tables):
    nb, _, seq = hyt.shape
    n1 = 2 * seq // DFT2
    cb = HY_CB
    rc = _fft_rows(n1, cb)
    pu = HY_UNIT // (n1 * rc)
    nblk = D_HY // cb
    fh, fl, ih, il, tw = tables
    kern = functools.partial(_hyena_kernel, seq=seq, n1=n1, cb=cb, nb=nb, rc=rc, pu=pu)
    full = lambda *shape: pl.BlockSpec(shape, lambda i: (0,) * len(shape))
    return pl.pallas_call(
        kern,
        grid=(nblk,),
        in_specs=[pl.BlockSpec((nb, 3, cb, seq), lambda i: (0, 0, i, 0)),
                  pl.BlockSpec((3, 3, cb, 1), lambda i: (0, 0, i, 0)),
                  pl.BlockSpec((3, cb, 1), lambda i: (0, i, 0)),
                  pl.BlockSpec((2, cb, 1), lambda i: (0, i, 0)),
                  pl.BlockSpec((2, cb // rc, n1 * rc, 2 * DFT2), lambda i: (0, i, 0, 0)),
                  full(2 * DFT2, 2 * DFT2), full(2 * DFT2, 2 * DFT2),
                  full(2 * DFT2, 2 * DFT2), full(2 * DFT2, 2 * DFT2),
                  full(n1 * rc, 2 * DFT2)],
        out_specs=pl.BlockSpec((nb, cb, seq), lambda i: (0, i, 0)),
        out_shape=jax.ShapeDtypeStruct((nb, D_HY, seq), F32),
        scratch_shapes=[pltpu.VMEM((3, nb, cb, seq), F32),
                        pltpu.VMEM((2, HY_UNIT, 2 * DFT2), F32), pltpu.VMEM((2, HY_UNIT, 2 * DFT2), F32),
                        pltpu.VMEM((n1 * rc, 2 * DFT2), F32), pltpu.VMEM((n1 * rc, 2 * DFT2), F32)],
        compiler_params=_cparams(("parallel",)),
        name="hyena",
    )(hyt.reshape(nb, 3, D_HY, seq), cw, cbias, bias, kf, fh, fl, ih, il, tw)


def _out_proj_kernel(x_ref, or_ref, oht_ref, mod_ref, gr_ref, gh_ref, wr_ref, wh_ref, g2_ref, x1_ref, xn_ref):
    orn = (_rms(or_ref[0]) * gr_ref[...]).astype(BF16)
    oh = oht_ref[0]
    ohn = oh * lax.rsqrt(jnp.mean(oh * oh, axis=0, keepdims=True) + EPS) * gh_ref[...]
    o = jnp.dot(orn, wr_ref[...], preferred_element_type=F32)
    o = o + lax.dot_general(ohn.astype(BF16), wh_ref[...], (((0,), (0,)), ((), ())), preferred_element_type=F32)
    x1 = x_ref[0] + mod_ref[0, 2:3, :] * o
    x1_ref[0] = x1
    xn = _rms(x1) * g2_ref[...]
    xn_ref[0] = (xn * (1.0 + mod_ref[0, 4:5, :]) + mod_ref[0, 3:4, :]).astype(BF16)


def _out_proj(x, o_r, oht, mod, gr, gh, wr, wh, g2, tm):
    b, l, _ = x.shape
    mod_map = (lambda i, j: (i, 0, 0)) if mod.shape[0] > 1 else (lambda i, j: (0, 0, 0))
    return pl.pallas_call(
        _out_proj_kernel,
        grid=(b, l // tm),
        in_specs=[pl.BlockSpec((1, tm, D_MODEL), lambda i, j: (i, j, 0)),
                  pl.BlockSpec((1, tm, D_RNN), lambda i, j: (i, j, 0)),
                  pl.BlockSpec((1, D_HY, tm), lambda i, j: (i, 0, j)),
                  pl.BlockSpec((1, N_MOD, D_MODEL), mod_map),
                  pl.BlockSpec((1, D_RNN), lambda i, j: (0, 0)),
                  pl.BlockSpec((D_HY, 1), lambda i, j: (0, 0)),
                  pl.BlockSpec((D_RNN, D_MODEL), lambda i, j: (0, 0)),
                  pl.BlockSpec((D_HY, D_MODEL), lambda i, j: (0, 0)),
                  pl.BlockSpec((1, D_MODEL), lambda i, j: (0, 0))],
        out_specs=[pl.BlockSpec((1, tm, D_MODEL), lambda i, j: (i, j, 0)),
                   pl.BlockSpec((1, tm, D_MODEL), lambda i, j: (i, j, 0))],
        out_shape=[jax.ShapeDtypeStruct((b, l, D_MODEL), F32),
                   jax.ShapeDtypeStruct((b, l, D_MODEL), BF16)],
        compiler_params=_cparams(("parallel", "parallel")),
        name="out_proj",
    )(x, o_r, oht, mod, gr, gh, wr, wh, g2)


FF_TM = 512
FF_SUB = 2 * LANES


def _ffn_kernel(x1_ref, xn_ref, xp_ref, xq_ref, mod_ref, wu_ref, cw_ref, cb_ref, wd_ref, gf_ref, o_ref, xe_s, h_s,
                *, seg, halo, on_grid, tiles_per_seq, final_norm):
    t = pl.program_id(0)
    tm = x1_ref.shape[1]
    rows = tm + 2 * halo
    if halo:
        first = (t % tiles_per_seq) == 0
        last = (t % tiles_per_seq) == tiles_per_seq - 1
        xe_s[0:halo, :] = jnp.where(first, jnp.zeros_like(xp_ref[0]), xp_ref[0])
        xe_s[halo + tm:rows, :] = jnp.where(last, jnp.zeros_like(xq_ref[0]), xq_ref[0])
    xe_s[halo:halo + tm, :] = xn_ref[0]

    pos = lax.broadcasted_iota(jnp.int32, (rows, 1), 0) % seg
    for c0 in range(0, D_FF, FF_SUB):
        w = min(FF_SUB, D_FF - c0)
        g = jnp.dot(xe_s[...], wu_ref[:, D_FF + c0:D_FF + c0 + w], preferred_element_type=F32)
        a = jnp.dot(xe_s[halo:halo + tm, :], wu_ref[:, c0:c0 + w], preferred_element_type=F32)
        gls = pltpu.roll(jnp.where(pos == seg - 1, 0.0, g), 1, axis=0)
        grs = pltpu.roll(jnp.where(pos == 0, 0.0, g), rows - 1, axis=0)
        acc = cb_ref[:, c0:c0 + w]
        for dr in ((-1, 0, 1) if on_grid else (0,)):
            lo = halo + dr * seg
            acc = (acc + cw_ref[dr + 1, 0:1, c0:c0 + w] * gls[lo:lo + tm]
                   + cw_ref[dr + 1, 1:2, c0:c0 + w] * g[lo:lo + tm]
                   + cw_ref[dr + 1, 2:3, c0:c0 + w] * grs[lo:lo + tm])
        h_s[:, c0:c0 + w] = (jax.nn.gelu(acc) * a).astype(BF16)
    y = jnp.dot(h_s[...], wd_ref[...], preferred_element_type=F32)
    x2 = x1_ref[0] + mod_ref[0, 5:6, :] * y
    if final_norm:
        x2 = _rms(x2) * gf_ref[...]
    o_ref[0] = x2


def _ffn(x1, xn2, mod, wu, cw, cb, wd, gf, on_grid, final_norm):
    b, l, _ = x1.shape
    tm = FF_TM
    if on_grid:
        seg, halo = GRID_W, GRID_W
        tps = l // tm
    else:
        seg, halo = l, 0
        tps = 1
    nt = b * l // tm
    hb = GRID_W
    nhb = tm // hb
    last_hb = b * l // hb - 1
    mod_map = (lambda t: (t // tps, 0, 0)) if mod.shape[0] > 1 else (lambda t: (0, 0, 0))
    kern = functools.partial(_ffn_kernel, seg=seg, halo=halo, on_grid=on_grid, tiles_per_seq=tps, final_norm=final_norm)
    const = lambda *shape: pl.BlockSpec(shape, lambda t: (0,) * len(shape), pipeline_mode=pl.Buffered(1))
    xh = xn2.reshape(b * l // hb, hb, D_MODEL)
    out = pl.pallas_call(
        kern,
        grid=(nt,),
        in_specs=[pl.BlockSpec((1, tm, D_MODEL), lambda t: (t, 0, 0)),
                  pl.BlockSpec((1, tm, D_MODEL), lambda t: (t, 0, 0)),
                  pl.BlockSpec((1, hb, D_MODEL), lambda t: (jnp.maximum(t * nhb - 1, 0), 0, 0)),
                  pl.BlockSpec((1, hb, D_MODEL), lambda t: (jnp.minimum((t + 1) * nhb, last_hb), 0, 0)),
                  pl.BlockSpec((1, N_MOD, D_MODEL), mod_map),
                  const(D_MODEL, 2 * D_FF), const(3, 3, D_FF), const(1, D_FF), const(D_FF, D_MODEL), const(1, D_MODEL)],
        out_specs=pl.BlockSpec((1, tm, D_MODEL), lambda t: (t, 0, 0)),
        out_shape=jax.ShapeDtypeStruct((nt, tm, D_MODEL), F32),
        scratch_shapes=[pltpu.VMEM((tm + 2 * halo, D_MODEL), BF16), pltpu.VMEM((tm, D_FF), BF16)],
        compiler_params=_cparams(("parallel",)),
        name="ffn",
    )(x1.reshape(nt, tm, D_MODEL), xn2.reshape(nt, tm, D_MODEL), xh, xh, mod, wu, cw, cb, wd, gf)
    return out.reshape(b, l, D_MODEL)


def _trunk_layer(x, mod, h0, p, kf, tables, on_grid, final_norm, g_final):
    b, l, _ = x.shape
    tm = min(512, l)
    xy, hyt = _in_proj(x, mod, p['g_norm1'], p['wxy'], p['whyt'], tm)
    o_r, states = _rglru(xy, p['rg_conv_w'], p['rg_conv_b'], p['wg'], p['bg'], p['ap'], h0)
    oht = _hyena(hyt, p['hy_cw'], p['hy_cb'], p['hy_bias'], kf, tables)
    x1, xn2 = _out_proj(x, o_r, oht, mod, p['g_rnn_out'], p['g_hy_out'], p['w_out_r'], p['w_out_h'], p['g_norm2'], tm)
    x2 = _ffn(x1, xn2, mod, p['w_up'], p['ffn_conv_w'], p['ffn_conv_b'], p['w_down'], g_final, on_grid, final_norm)
    return x2, states


def kernel(x_prompt, x_sample, state_rglru, c, c_ctx, w_ada, b_ada, g_norm1, g_norm2, w_in, rg_conv_w, rg_conv_b, rg_gate_w, rg_gate_b, rg_a, hy_conv_w, hy_conv_b, hf_w1, hf_b1, hf_w2, hf_b2, hf_w3, hf_freq, hy_bias, g_rnn_out, g_hy_out, w_out, w_up, ffn_conv_w, ffn_conv_b, w_down, g_final):
    depth = w_in.shape[0]
    nb_ctx, l_ctx, _ = x_prompt.shape
    nb_lat, l_lat, _ = x_sample.shape

    cc = jnp.zeros((SUBLANES, D_MODEL), F32).at[0].set(c_ctx).at[1:1 + nb_lat].set(c)
    mods = _modulation(cc, w_ada, b_ada).reshape(depth, SUBLANES, N_MOD, D_MODEL)

    tab_ctx = _dft_tables(2 * l_ctx // DFT2, _fft_rows(2 * l_ctx // DFT2, HY_CB))
    tab_lat = _dft_tables(2 * l_lat // DFT2, _fft_rows(2 * l_lat // DFT2, HY_CB))
    gf = g_final.reshape(1, D_MODEL)
    zero_h = jnp.zeros((nb_ctx, 2, D_RNN), F32)

    xp, xs = x_prompt, x_sample
    new_states = []
    for l in range(depth):
        wg, bg, ap = _gate_weights(rg_gate_w[l], rg_gate_b[l], rg_a[l])
        p = {
            'g_norm1': g_norm1[l].reshape(1, D_MODEL), 'g_norm2': g_norm2[l].reshape(1, D_MODEL),
            'wxy': w_in[l, :, :2 * D_RNN].astype(BF16), 'whyt': w_in[l, :, 2 * D_RNN:].T.astype(BF16),
            'rg_conv_w': rg_conv_w[l], 'rg_conv_b': rg_conv_b[l].reshape(1, D_RNN),
            'wg': wg, 'bg': bg, 'ap': ap,
            'hy_cw': hy_conv_w[l].reshape(3, 3, D_HY, 1), 'hy_cb': hy_conv_b[l].reshape(3, D_HY, 1),
            'hy_bias': hy_bias[l].reshape(2, D_HY, 1),
            'g_rnn_out': g_rnn_out[l].reshape(1, D_RNN), 'g_hy_out': g_hy_out[l].reshape(D_HY, 1),
            'w_out_r': w_out[l, :D_RNN].astype(BF16), 'w_out_h': w_out[l, D_RNN:].astype(BF16),
            'w_up': w_up[l].astype(BF16), 'ffn_conv_w': ffn_conv_w[l], 'ffn_conv_b': ffn_conv_b[l].reshape(1, D_FF),
            'w_down': w_down[l].astype(BF16),
        }
        fargs = (hf_w1[l], hf_b1[l], hf_w2[l], hf_b2[l], hf_w3[l], hf_freq[l])
        kf_ctx = _hyena_filters(l_ctx, *fargs, tab_ctx)
        kf_lat = _hyena_filters(l_lat, *fargs, tab_lat)
        final = l == depth - 1
        xp, st = _trunk_layer(xp, mods[l, 0:1], zero_h, p, kf_ctx, tab_ctx, False, final, gf)
        new_states.append(st)
        xs, _ = _trunk_layer(xs, mods[l, 1:1 + nb_lat], state_rglru[:, l], p, kf_lat, tab_lat, True, final, gf)
    return (xp, xs, jnp.stack(new_states, axis=1))
```

```python
import functools
import math

import jax
import jax.numpy as jnp
import ml_dtypes
import numpy as np
from jax import lax
from jax.experimental import pallas as pl
from jax.experimental.pallas import tpu as pltpu

F32 = jnp.float32
BF16 = jnp.bfloat16

D_MODEL = 1024
D_RNN = 512
D_HY = 512
N_HEADS = 8
HEAD_DIM = D_RNN // N_HEADS
RG_C = 8.0
GRID_W = 64
HY_BANDS = 16
HY_FH = 64
D_FF = 2816
N_MOD = 6
EPS = 1e-6

SUBLANES = 8
LANES = 128
VMEM_LIMIT = 56 * 1024 * 1024

DFT2 = LANES
FFT_PASSES = 2


def _cparams(sem):
    return pltpu.CompilerParams(dimension_semantics=sem, vmem_limit_bytes=VMEM_LIMIT)


def _rms(x):
    return x * lax.rsqrt(jnp.mean(x * x, axis=-1, keepdims=True) + EPS)


def _mod_kernel(c_ref, w_ref, b_ref, o_ref):
    c = c_ref[...]
    s = c * jax.nn.sigmoid(c)
    o_ref[0] = jnp.dot(s, w_ref[0], precision=lax.Precision.HIGHEST, preferred_element_type=F32) + b_ref[0]


def _modulation(cc, w_ada, b_ada):
    depth, _, n = w_ada.shape
    tn = 1536
    return pl.pallas_call(
        _mod_kernel,
        grid=(depth, n // tn),
        in_specs=[pl.BlockSpec((SUBLANES, D_MODEL), lambda l, j: (0, 0)),
                  pl.BlockSpec((1, D_MODEL, tn), lambda l, j: (l, 0, j)),
                  pl.BlockSpec((1, 1, tn), lambda l, j: (l, 0, j))],
        out_specs=pl.BlockSpec((1, SUBLANES, tn), lambda l, j: (l, 0, j)),
        out_shape=jax.ShapeDtypeStruct((depth, SUBLANES, n), F32),
        compiler_params=_cparams(("parallel", "parallel")),
        name="adaln_mod",
    )(cc, w_ada, b_ada.reshape(depth, 1, n))


RG_HALF = D_RNN // 2
RG_TILES = RG_HALF // LANES


def _rglru_kernel(xr_ref, yr_ref, cw_ref, cb_ref, wg_ref, bg_ref, ap_ref, h0_ref, o_ref, st_ref,
                  ext, xc_s, hf, hb, a_s, b_s, *, seq, tc):
    nchunks = seq // tc
    nblk = tc // SUBLANES
    ext[0:SUBLANES, :] = jnp.zeros((SUBLANES, RG_HALF), F32)
    ext[SUBLANES:SUBLANES + seq, :] = xr_ref[0]
    ext[SUBLANES + seq:2 * SUBLANES + seq, :] = jnp.zeros((SUBLANES, RG_HALF), F32)
    for c in range(nchunks):
        xc = cb_ref[...] + cw_ref[0:1, :] * ext[pl.ds(c * tc + SUBLANES - 2, tc), :]
        for k in range(1, 4):
            xc = xc + cw_ref[k:k + 1, :] * ext[pl.ds(c * tc + SUBLANES - 2 + k, tc), :]
        xc_s[pl.ds(c * tc, tc), :] = xc

    row = lax.broadcasted_iota(jnp.int32, (SUBLANES, LANES), 0)
    half_neg_c_sp = [(-0.5 * RG_C) * jax.nn.softplus(-ap_ref[d, 0]) for d in range(2)]

    def gates(c0, d):
        xc = xc_s[pl.ds(c0, tc), :]
        g = jnp.dot(xc.astype(BF16), wg_ref[d, 0], preferred_element_type=F32) + bg_ref[d, 0]
        i = 0.5 * jnp.tanh(g[:, RG_HALF:]) + 0.5
        log_a = half_neg_c_sp[d] * jnp.tanh(g[:, :RG_HALF]) + half_neg_c_sp[d]
        a = jnp.exp(log_a)
        y = jnp.tanh(-log_a) * (a * a + 1.0)
        mult = jnp.where(y > 0.0, y * lax.rsqrt(y), 0.0)
        a_s[d] = a
        b_s[d] = xc * i * mult

    def local_scan(a, b, reverse):
        for s in (1, 2, 4):
            if reverse:
                keep = row < SUBLANES - s
                shift = SUBLANES - s
            else:
                keep = row >= s
                shift = s
            a_sh = jnp.where(keep, pltpu.roll(a, shift, axis=0), 1.0)
            b_sh = jnp.where(keep, pltpu.roll(b, shift, axis=0), 0.0)
            b = a * b_sh + b
            a = a * a_sh
        return a, b

    def block_body(cf0, cb0):
        def body(j, carry):
            rf = pl.multiple_of(j * SUBLANES, SUBLANES)
            rb = pl.multiple_of(tc - SUBLANES - j * SUBLANES, SUBLANES)
            out = []
            for t in range(RG_TILES):
                lanes = slice(t * LANES, (t + 1) * LANES)
                pa, pb = local_scan(a_s[0, pl.ds(rf, SUBLANES), lanes], b_s[0, pl.ds(rf, SUBLANES), lanes], False)
                h = pa * carry[2 * t] + pb
                hf[pl.ds(cf0 + rf, SUBLANES), lanes] = h
                out.append(jnp.broadcast_to(h[SUBLANES - 1:SUBLANES, :], (SUBLANES, LANES)))
                pa, pb = local_scan(a_s[1, pl.ds(rb, SUBLANES), lanes], b_s[1, pl.ds(rb, SUBLANES), lanes], True)
                h = pa * carry[2 * t + 1] + pb
                hb[pl.ds(cb0 + rb, SUBLANES), lanes] = h
                out.append(jnp.broadcast_to(h[0:1, :], (SUBLANES, LANES)))
            return tuple(out)
        return body

    carry = []
    for t in range(RG_TILES):
        carry.append(jnp.broadcast_to(h0_ref[0, 0:1, t * LANES:(t + 1) * LANES], (SUBLANES, LANES)))
        carry.append(jnp.broadcast_to(h0_ref[0, 1:2, t * LANES:(t + 1) * LANES], (SUBLANES, LANES)))
    carry = tuple(carry)
    for c in range(nchunks):
        cf0 = c * tc
        cb0 = (nchunks - 1 - c) * tc
        gates(cf0, 0)
        gates(cb0, 1)
        carry = lax.fori_loop(0, nblk, block_body(cf0, cb0), carry, unroll=2)
    for t in range(RG_TILES):
        st_ref[0, 0:1, t * LANES:(t + 1) * LANES] = carry[2 * t][0:1, :]
        st_ref[0, 1:2, t * LANES:(t + 1) * LANES] = carry[2 * t + 1][0:1, :]
    for c in range(nchunks):
        rows = pl.ds(c * tc, tc)
        o_ref[0, rows, :] = (hf[rows, :] + hb[rows, :]) * jax.nn.gelu(yr_ref[0, rows, :])


def _rglru(xy, cw, cb, wg, bg, ap, h0):
    b, seq, _ = xy.shape
    tc = min(512, seq)
    nh = D_RNN // RG_HALF
    kern = functools.partial(_rglru_kernel, seq=seq, tc=tc)
    return pl.pallas_call(
        kern,
        grid=(b, nh),
        in_specs=[pl.BlockSpec((1, seq, RG_HALF), lambda i, h: (i, 0, h)),
                  pl.BlockSpec((1, seq, RG_HALF), lambda i, h: (i, 0, nh + h)),
                  pl.BlockSpec((4, RG_HALF), lambda i, h: (0, h)),
                  pl.BlockSpec((1, RG_HALF), lambda i, h: (0, h)),
                  pl.BlockSpec((2, 1, RG_HALF, 2 * RG_HALF), lambda i, h: (0, h, 0, 0)),
                  pl.BlockSpec((2, 1, 1, 2 * RG_HALF), lambda i, h: (0, h, 0, 0)),
                  pl.BlockSpec((2, 1, 1, RG_HALF), lambda i, h: (0, h, 0, 0)),
                  pl.BlockSpec((1, 2, RG_HALF), lambda i, h: (i, 0, h))],
        out_specs=[pl.BlockSpec((1, seq, RG_HALF), lambda i, h: (i, 0, h)),
                   pl.BlockSpec((1, 2, RG_HALF), lambda i, h: (i, 0, h))],
        out_shape=[jax.ShapeDtypeStruct((b, seq, D_RNN), F32),
                   jax.ShapeDtypeStruct((b, 2, D_RNN), F32)],
        scratch_shapes=[pltpu.VMEM((seq + 2 * SUBLANES, RG_HALF), F32),
                        pltpu.VMEM((seq, RG_HALF), F32),
                        pltpu.VMEM((seq, RG_HALF), F32),
                        pltpu.VMEM((seq, RG_HALF), F32),
                        pltpu.VMEM((2, tc, RG_HALF), F32),
                        pltpu.VMEM((2, tc, RG_HALF), F32)],
        compiler_params=_cparams(("parallel", "parallel")),
        name="rglru",
    )(xy, xy, cw, cb, wg, bg, ap, h0)


def _gate_weights(gate_w, gate_b, a_param):
    nh = D_RNN // RG_HALF
    hp = N_HEADS // nh
    w = gate_w.reshape(2, 2, nh, hp, HEAD_DIM, HEAD_DIM)
    eye = jnp.eye(hp, dtype=F32)
    dense = jnp.einsum('dghpio,pq->dhpigqo', w, eye)
    dense = (0.5 * dense).reshape(2, nh, RG_HALF, 2 * RG_HALF).astype(BF16)
    bias = 0.5 * gate_b.reshape(2, 2, nh, RG_HALF).transpose(0, 2, 1, 3).reshape(2, nh, 1, 2 * RG_HALF)
    ap = a_param.reshape(2, nh, 1, RG_HALF)
    return dense, bias, ap


def _add(a, b):
    if a is None:
        return b
    if b is None:
        return a
    return a + b


def _sub(a, b):
    if b is None:
        return a
    if a is None:
        return -b
    return a - b


def _scale(a, s):
    if a is None or s == 0.0:
        return None
    if s == 1.0:
        return a
    if s == -1.0:
        return -a
    return a * s


def _cmul_const(z, wr, wi):
    if z is None:
        return None
    re, im = z
    if abs(wr) < 1e-15:
        wr = 0.0
    if abs(wi) < 1e-15:
        wi = 0.0
    if wr != 0.0 and abs(abs(wr) - abs(wi)) < 1e-15 and re is not None and im is not None:
        sr, si = math.copysign(1.0, wr), math.copysign(1.0, wi)
        return (_scale(_sub(_scale(re, sr), _scale(im, si)), abs(wr)), _scale(_add(_scale(re, si), _scale(im, sr)), abs(wr)))
    return (_sub(_scale(re, wr), _scale(im, wi)), _add(_scale(re, wi), _scale(im, wr)))


def _cadd(a, b):
    if a is None:
        return b
    if b is None:
        return a
    return (_add(a[0], b[0]), _add(a[1], b[1]))


def _csub(a, b):
    if b is None:
        return a
    if a is None:
        return (_sub(None, b[0]), _sub(None, b[1]))
    return (_sub(a[0], b[0]), _sub(a[1], b[1]))


def _cfft(xs, sign):
    n = len(xs)
    if n == 1:
        return list(xs)
    even = _cfft(xs[0::2], sign)
    odd = _cfft(xs[1::2], sign)
    out = [None] * n
    for k in range(n // 2):
        ang = sign * 2.0 * math.pi * k / n
        t = _cmul_const(odd[k], math.cos(ang), math.sin(ang))
        out[k] = _cadd(even[k], t)
        out[k + n // 2] = _csub(even[k], t)
    return out


def _fft64(get, put, sign, t_ref, rc, nin, nout):
    r8 = 8
    for p in range(r8):
        if sign < 0:
            xs = [get(r8 * a + p) if r8 * a + p < nin else None for a in range(r8)]
        else:
            xs = [get(p + r8 * d) for d in range(r8)]
        ts = _cfft(xs, sign)
        for q in range(r8):
            ang = sign * 2.0 * math.pi * p * q / 64.0
            re, im = _cmul_const(ts[q], math.cos(ang), math.sin(ang))
            rows = pl.ds((p * r8 + q) * rc, rc)
            t_ref[rows, 0:LANES] = re if re is not None else jnp.zeros_like(im)
            t_ref[rows, LANES:2 * LANES] = im if im is not None else jnp.zeros_like(re)
    for q in range(r8):
        ys = [(t_ref[pl.ds((p * r8 + q) * rc, rc), 0:LANES], t_ref[pl.ds((p * r8 + q) * rc, rc), LANES:2 * LANES])
              for p in range(r8)]
        zs = _cfft(ys, sign)
        for m in range(r8):
            k = q + r8 * m if sign < 0 else r8 * m + q
            if k < nout:
                put(k, zs[m])


def _dft_tables(n1, rc):
    n = n1 * DFT2
    k = np.arange(DFT2, dtype=np.float64)
    ang = 2.0 * np.pi * np.outer(k, k) / DFT2
    c, s = np.cos(ang), np.sin(ang)
    fwd = np.block([[c, -s], [s, c]])
    inv = np.block([[c, s], [-s, c]]) / n

    def split(m):
        hi = m.astype(np.float32).astype(ml_dtypes.bfloat16)
        lo = (m - hi.astype(np.float64)).astype(ml_dtypes.bfloat16)
        return jnp.asarray(hi), jnp.asarray(lo)

    tw_ang = 2.0 * np.pi * np.outer(np.arange(n1, dtype=np.float64), k) / n
    tw = np.concatenate([np.cos(tw_ang), -np.sin(tw_ang)], axis=1)
    tw = np.repeat(tw, rc, axis=0).astype(np.float32)
    return split(fwd) + split(inv) + (jnp.asarray(tw),)


HY_CB = 32


def _fft_rows(n1, cb):
    return SUBLANES if n1 >= 16 else cb


def _dft_mm(x, hi_ref, lo_ref):
    xh = x.astype(BF16)
    acc = jnp.dot(xh, hi_ref[...], preferred_element_type=F32)
    if FFT_PASSES >= 2:
        acc = acc + jnp.dot(xh, lo_ref[...], preferred_element_type=F32)
    if FFT_PASSES >= 3:
        xl = (x - xh.astype(F32)).astype(BF16)
        acc = acc + jnp.dot(xl, hi_ref[...], preferred_element_type=F32)
    return acc


def _filter_hidden(feat_ref, w1_ref, b1_ref, w2_ref, b2_ref, fr_ref, h2h_s, h2l_s):
    hi = lax.Precision.HIGHEST

    @pl.when(pl.program_id(0) == 0)
    def _():
        h = jnp.dot(w1_ref[...], feat_ref[...], precision=hi, preferred_element_type=F32) + b1_ref[...]
        h = jnp.sin(fr_ref[:, 0:1] * h)
        h = jnp.dot(w2_ref[...], h, precision=hi, preferred_element_type=F32) + b2_ref[...]
        h2 = jnp.sin(fr_ref[:, 1:2] * h)
        h2h = h2.astype(BF16)
        h2h_s[...] = h2h
        h2l_s[...] = (h2 - h2h.astype(F32)).astype(BF16)


def _filter_block(w3_ref, fh_ref, fl_ref, tw_ref, kf_ref, h2h_s, h2l_s, k_s, s_s, t_s, *, seq, n1, cb, rc):
    n = 2 * seq
    lane = lax.broadcasted_iota(jnp.int32, (cb, n), 1)
    pos = jnp.where(lane < seq, lane, n - lane).astype(F32)
    t = pos / float(max(seq - 1, 1))
    d_idx = (lax.broadcasted_iota(jnp.int32, (cb, n), 0) + pl.program_id(0) * cb).astype(F32)
    min_decay = math.log(1e-2) / 1.5
    max_decay = math.log(1e-2) / 0.3
    delta = jnp.abs(min_decay + d_idx * ((max_decay - min_decay) / (D_HY - 1)))
    decay = jnp.exp(-t * delta)
    w3 = w3_ref[...].reshape(4 * cb, HY_FH)
    w3h = w3.astype(BF16)
    w3l = (w3 - w3h.astype(F32)).astype(BF16)
    hfb = (jnp.dot(w3h, h2h_s[...], preferred_element_type=F32) + jnp.dot(w3h, h2l_s[...], preferred_element_type=F32)
           + jnp.dot(w3l, h2h_s[...], preferred_element_type=F32))
    for o in range(2):
        hf = hfb[(2 * o) * cb:(2 * o + 1) * cb]
        hb = hfb[(2 * o + 1) * cb:(2 * o + 2) * cb]
        k_s[o] = jnp.where(lane < seq, hf, jnp.where(lane == seq, 0.0, hb)) * decay

    nrc = cb // rc
    for o in range(2):
        for i in range(nrc):
            r0 = i * rc
            base = i * (n1 * rc)

            def get(j, o=o, r0=r0):
                return (k_s[o, pl.ds(r0, rc), j * LANES:(j + 1) * LANES], None)

            def put(k1, z, base=base):
                re, im = z
                if k1 > 0:
                    twr = tw_ref[k1 * rc:(k1 + 1) * rc, 0:LANES]
                    twi = tw_ref[k1 * rc:(k1 + 1) * rc, LANES:2 * LANES]
                    re, im = (re * twr, re * twi) if im is None else (re * twr - im * twi, re * twi + im * twr)
                s_s[pl.ds(base + k1 * rc, rc), 0:LANES] = re
                s_s[pl.ds(base + k1 * rc, rc), LANES:2 * LANES] = im if im is not None else jnp.zeros_like(re)

            if n1 == 64:
                _fft64(get, put, -1, t_s, rc, n1, n1)
            else:
                zs = _cfft([get(j) for j in range(n1)], -1)
                for k1 in range(n1):
                    put(k1, zs[k1])
        x = s_s[...]
        xh = x.astype(BF16)
        xl = (x - xh.astype(F32)).astype(BF16)
        z = (jnp.dot(xh, fh_ref[...], preferred_element_type=F32) + jnp.dot(xh, fl_ref[...], preferred_element_type=F32)
             + jnp.dot(xl, fh_ref[...], preferred_element_type=F32))
        for i in range(nrc):
            kf_ref[o, i] = z[i * n1 * rc:(i + 1) * n1 * rc, :]


FRONT_TM = 512


def _front_kernel(x_ref, mod_ref, g_ref, wxy_ref, whyt_ref, feat_ref, w1_ref, b1_ref, w2_ref, b2_ref, fr_ref, w3_ref,
                  fh_ref, fl_ref, tw_ref, xy_ref, hyt_ref, kf_ref, h2h_s, h2l_s, k_s, s_s, t_s, *, seq, n1, cb, rc):
    _filter_hidden(feat_ref, w1_ref, b1_ref, w2_ref, b2_ref, fr_ref, h2h_s, h2l_s)
    for q in range(x_ref.shape[0]):
        xn = _rms(x_ref[q]) * g_ref[...]
        xn = (xn * (1.0 + mod_ref[0, 1:2, :]) + mod_ref[0, 0:1, :]).astype(BF16)
        xy_ref[q] = jnp.dot(xn, wxy_ref[...], preferred_element_type=F32)
        hyt_ref[q] = lax.dot_general(whyt_ref[...], xn, (((1,), (1,)), ((), ())), preferred_element_type=F32)
    _filter_block(w3_ref, fh_ref, fl_ref, tw_ref, kf_ref, h2h_s, h2l_s, k_s, s_s, t_s, seq=seq, n1=n1, cb=cb, rc=rc)


def _front(x, mod, g, wxy, whyt, w1, b1, w2, b2, w3, freq, tables):
    b, seq, _ = x.shape
    n1 = 2 * seq // DFT2
    n = 2 * seq
    rc = _fft_rows(n1, HY_CB)
    tmb = min(FRONT_TM, seq)
    nbt = FRONT_TM // tmb
    steps = b * seq // FRONT_TM
    cb = D_HY // steps
    assert cb % rc == 0 and cb * steps == D_HY
    tps = seq // tmb
    fh, fl, _, _, tw = tables
    pos = np.arange(n, dtype=np.float64)
    pos = np.where(pos < seq, pos, n - pos)
    tt = pos / max(seq - 1, 1)
    omega = 2.0 * math.pi * pos / seq
    bands = np.linspace(1e-4, HY_BANDS - 1, HY_BANDS)
    ang = omega[None, :] * bands[:, None]
    feats = np.concatenate([tt[None, :], np.cos(ang), np.sin(ang)], axis=0)
    nfeat = LANES
    feats = np.pad(feats, ((0, nfeat - feats.shape[0]), (0, 0))).astype(np.float32)
    w1 = jnp.pad(w1, ((0, nfeat - w1.shape[0]), (0, 0)))
    w3t = w3.T.reshape(4, D_HY, HY_FH)
    kern = functools.partial(_front_kernel, seq=seq, n1=n1, cb=cb, rc=rc)
    const = lambda *shape: pl.BlockSpec(shape, lambda i: (0,) * len(shape), pipeline_mode=pl.Buffered(1))
    mod_map = (lambda i: (i // tps, 0, 0)) if mod.shape[0] > 1 else (lambda i: (0, 0, 0))
    xy, hyt, kf = pl.pallas_call(
        kern,
        grid=(steps,),
        in_specs=[pl.BlockSpec((nbt, tmb, D_MODEL), lambda i: (i, 0, 0)),
                  pl.BlockSpec((1, N_MOD, D_MODEL), mod_map),
                  const(1, D_MODEL), const(D_MODEL, 2 * D_RNN), const(3 * D_HY, D_MODEL),
                  const(nfeat, n), const(HY_FH, nfeat), const(HY_FH, 1), const(HY_FH, HY_FH), const(HY_FH, 1),
                  const(HY_FH, 2),
                  pl.BlockSpec((4, cb, HY_FH), lambda i: (0, i, 0)),
                  const(2 * DFT2, 2 * DFT2), const(2 * DFT2, 2 * DFT2), const(n1 * rc, 2 * DFT2)],
        out_specs=[pl.BlockSpec((nbt, tmb, 2 * D_RNN), lambda i: (i, 0, 0)),
                   pl.BlockSpec((nbt, 3 * D_HY, tmb), lambda i: (i // tps, 0, i % tps)),
                   pl.BlockSpec((2, cb // rc, n1 * rc, 2 * DFT2), lambda i: (0, i, 0, 0))],
        out_shape=[jax.ShapeDtypeStruct((b * seq // tmb, tmb, 2 * D_RNN), F32),
                   jax.ShapeDtypeStruct((b, 3 * D_HY, seq), F32),
                   jax.ShapeDtypeStruct((2, D_HY // rc, n1 * rc, 2 * DFT2), F32)],
        scratch_shapes=[pltpu.VMEM((HY_FH, n), BF16), pltpu.VMEM((HY_FH, n), BF16), pltpu.VMEM((2, cb, n), F32),
                        pltpu.VMEM((n1 * cb, 2 * DFT2), F32), pltpu.VMEM((n1 * rc, 2 * DFT2), F32)],
        compiler_params=_cparams(("arbitrary",)),
        name="front",
    )(x.reshape(b * seq // tmb, tmb, D_MODEL), mod, g, wxy, whyt,
      jnp.asarray(feats), w1.T, b1.reshape(HY_FH, 1), w2.T, b2.reshape(HY_FH, 1), freq.T, w3t, fh, fl, tw)
    return xy.reshape(b, seq, 2 * D_RNN), hyt, kf


HY_UNIT = 512


def _hyena_kernel(hy_ref, cw_ref, cb_ref, bias_ref, kf_ref, fh_ref, fl_ref, ih_ref, il_ref, tw_ref, o_ref,
                  buf, sa, sb, ta, tc, *, seq, n1, cb, nb, rc, pu):
    npairs = nb // 2
    nin = n1 // 2
    nrc = cb // rc
    upo = (npairs // pu) * nrc
    nitems = 2 * upo
    assert upo >= 3, "an item's second-order stage A must come after its first-order stage C"
    lane = lax.broadcasted_iota(jnp.int32, (rc, seq), 1)

    def conv3(h, part, r0):
        w0 = cw_ref[0, part, pl.ds(r0, rc), :]
        w1 = cw_ref[1, part, pl.ds(r0, rc), :]
        w2 = cw_ref[2, part, pl.ds(r0, rc), :]
        bb = cb_ref[part, pl.ds(r0, rc), :]
        hm = jnp.where(lane == 0, 0.0, pltpu.roll(h, 1, axis=1))
        hp = jnp.where(lane == seq - 1, 0.0, pltpu.roll(h, seq - 1, axis=1))
        return w0 * hm + w1 * h + w2 * hp + bb

    def prep(i, _):
        b = i // nrc
        r0 = pl.multiple_of((i % nrc) * rc, rc)
        buf[0, b, pl.ds(r0, rc), :] = conv3(hy_ref[b, 0, pl.ds(r0, rc), :], 0, r0)
        return 0
    lax.fori_loop(0, nb * nrc, prep, 0, unroll=2)

    def item(j):
        o = j // upo
        u = j % upo
        return o, u // nrc, u % nrc

    def stage_a(j, slot):
        o, pg, rg = item(j)
        r0 = pl.multiple_of(rg * rc, rc)
        for pp in range(pu):
            p = pg * pu + pp
            def get(jb, p=p):
                return (buf[o, 2 * p, pl.ds(r0, rc), jb * LANES:(jb + 1) * LANES],
                        buf[o, 2 * p + 1, pl.ds(r0, rc), jb * LANES:(jb + 1) * LANES])

            def put(k1, z, pp=pp):
                re, im = z
                if k1 > 0:
                    twr = tw_ref[k1 * rc:(k1 + 1) * rc, 0:LANES]
                    twi = tw_ref[k1 * rc:(k1 + 1) * rc, LANES:2 * LANES]
                    re, im = re * twr - im * twi, re * twi + im * twr
                rows = pl.ds((pp * n1 + k1) * rc, rc)
                sa[slot, rows, 0:LANES] = re
                sa[slot, rows, LANES:2 * LANES] = im

            if n1 == 64:
                _fft64(get, put, -1, ta, rc, nin, n1)
            else:
                zs = _cfft([get(jb) if jb < nin else None for jb in range(n1)], -1)
                for k1 in range(n1):
                    put(k1, zs[k1])

    def stage_b(j, slot):
        o, _, rg = item(j)
        z = _dft_mm(sa[slot], fh_ref, fl_ref)
        kk = kf_ref[o, rg]
        if pu > 1:
            kk = jnp.concatenate([kk] * pu, axis=0)
        zr, zi = z[:, :LANES], z[:, LANES:]
        kr, ki = kk[:, :LANES], kk[:, LANES:]
        w = jnp.concatenate([zr * kr - zi * ki, zr * ki + zi * kr], axis=1)
        sb[slot] = _dft_mm(w, ih_ref, il_ref)

    def stage_c(j, slot):
        o, pg, rg = item(j)
        r0 = pl.multiple_of(rg * rc, rc)
        bias = bias_ref[o, pl.ds(r0, rc), :]
        for pp in range(pu):
            p = pg * pu + pp
            xg = [conv3(hy_ref[2 * p + q, 1 + o, pl.ds(r0, rc), :], 1 + o, r0) for q in range(2)]

            def get(k1, pp=pp):
                re = sb[slot, pl.ds((pp * n1 + k1) * rc, rc), 0:LANES]
                im = sb[slot, pl.ds((pp * n1 + k1) * rc, rc), LANES:2 * LANES]
                if k1 > 0:
                    twr = tw_ref[k1 * rc:(k1 + 1) * rc, 0:LANES]
                    twi = tw_ref[k1 * rc:(k1 + 1) * rc, LANES:2 * LANES]
                    re, im = re * twr + im * twi, im * twr - re * twi
                return (re, im)

            def put(jb, y, p=p, xg=xg):
                lanes = slice(jb * LANES, (jb + 1) * LANES)
                for q in range(2):
                    u = buf[o, 2 * p + q, pl.ds(r0, rc), lanes]
                    buf[o + 1, 2 * p + q, pl.ds(r0, rc), lanes] = xg[q][:, lanes] * (y[q] + u * bias)

            if n1 == 64:
                _fft64(get, put, +1, tc, rc, n1, nin)
            else:
                ys = _cfft([get(k1) for k1 in range(n1)], +1)
                for jb in range(nin):
                    put(jb, ys[jb])

    stage_a(0, 0)
    stage_a(1, 1)
    stage_b(0, 0)

    assert nitems % 2 == 0

    def steady(i, _):
        t = 2 + 2 * i
        stage_b(t - 1, 1)
        stage_a(t, 0)
        stage_c(t - 2, 0)
        stage_b(t, 0)
        stage_a(t + 1, 1)
        stage_c(t - 1, 1)
        return 0
    lax.fori_loop(0, (nitems - 2) // 2, steady, 0)
    stage_b(nitems - 1, (nitems - 1) % 2)
    stage_c(nitems - 2, nitems % 2)
    stage_c(nitems - 1, (nitems - 1) % 2)
    for b in range(nb):
        o_ref[b] = buf[2, b]


def _hyena(hyt, cw, cbias, bias, kf, tables):
    nb, _, seq = hyt.shape
    n1 = 2 * seq // DFT2
    cb = HY_CB
    rc = _fft_rows(n1, cb)
    pu = HY_UNIT // (n1 * rc)
    nblk = D_HY // cb
    fh, fl, ih, il, tw = tables
    kern = functools.partial(_hyena_kernel, seq=seq, n1=n1, cb=cb, nb=nb, rc=rc, pu=pu)
    full = lambda *shape: pl.BlockSpec(shape, lambda i: (0,) * len(shape))
    return pl.pallas_call(
        kern,
        grid=(nblk,),
        in_specs=[pl.BlockSpec((nb, 3, cb, seq), lambda i: (0, 0, i, 0)),
                  pl.BlockSpec((3, 3, cb, 1), lambda i: (0, 0, i, 0)),
                  pl.BlockSpec((3, cb, 1), lambda i: (0, i, 0)),
                  pl.BlockSpec((2, cb, 1), lambda i: (0, i, 0)),
                  pl.BlockSpec((2, cb // rc, n1 * rc, 2 * DFT2), lambda i: (0, i, 0, 0)),
                  full(2 * DFT2, 2 * DFT2), full(2 * DFT2, 2 * DFT2),
                  full(2 * DFT2, 2 * DFT2), full(2 * DFT2, 2 * DFT2),
                  full(n1 * rc, 2 * DFT2)],
        out_specs=pl.BlockSpec((nb, cb, seq), lambda i: (0, i, 0)),
        out_shape=jax.ShapeDtypeStruct((nb, D_HY, seq), F32),
        scratch_shapes=[pltpu.VMEM((3, nb, cb, seq), F32),
                        pltpu.VMEM((2, HY_UNIT, 2 * DFT2), F32), pltpu.VMEM((2, HY_UNIT, 2 * DFT2), F32),
                        pltpu.VMEM((n1 * rc, 2 * DFT2), F32), pltpu.VMEM((n1 * rc, 2 * DFT2), F32)],
        compiler_params=_cparams(("parallel",)),
        name="hyena",
    )(hyt.reshape(nb, 3, D_HY, seq), cw, cbias, bias, kf, fh, fl, ih, il, tw)


def _out_proj_kernel(x_ref, or_ref, oht_ref, mod_ref, gr_ref, gh_ref, wr_ref, wh_ref, g2_ref, x1_ref, xn_ref):
    orn = (_rms(or_ref[0]) * gr_ref[...]).astype(BF16)
    oh = oht_ref[0]
    ohn = oh * lax.rsqrt(jnp.mean(oh * oh, axis=0, keepdims=True) + EPS) * gh_ref[...]
    o = jnp.dot(orn, wr_ref[...], preferred_element_type=F32)
    o = o + lax.dot_general(ohn.astype(BF16), wh_ref[...], (((0,), (0,)), ((), ())), preferred_element_type=F32)
    x1 = x_ref[0] + mod_ref[0, 2:3, :] * o
    x1_ref[0] = x1
    xn = _rms(x1) * g2_ref[...]
    xn_ref[0] = (xn * (1.0 + mod_ref[0, 4:5, :]) + mod_ref[0, 3:4, :]).astype(BF16)


def _out_proj(x, o_r, oht, mod, gr, gh, wr, wh, g2, tm):
    b, l, _ = x.shape
    mod_map = (lambda i, j: (i, 0, 0)) if mod.shape[0] > 1 else (lambda i, j: (0, 0, 0))
    return pl.pallas_call(
        _out_proj_kernel,
        grid=(b, l // tm),
        in_specs=[pl.BlockSpec((1, tm, D_MODEL), lambda i, j: (i, j, 0)),
                  pl.BlockSpec((1, tm, D_RNN), lambda i, j: (i, j, 0)),
                  pl.BlockSpec((1, D_HY, tm), lambda i, j: (i, 0, j)),
                  pl.BlockSpec((1, N_MOD, D_MODEL), mod_map),
                  pl.BlockSpec((1, D_RNN), lambda i, j: (0, 0)),
                  pl.BlockSpec((D_HY, 1), lambda i, j: (0, 0)),
                  pl.BlockSpec((D_RNN, D_MODEL), lambda i, j: (0, 0)),
                  pl.BlockSpec((D_HY, D_MODEL), lambda i, j: (0, 0)),
                  pl.BlockSpec((1, D_MODEL), lambda i, j: (0, 0))],
        out_specs=[pl.BlockSpec((1, tm, D_MODEL), lambda i, j: (i, j, 0)),
                   pl.BlockSpec((1, tm, D_MODEL), lambda i, j: (i, j, 0))],
        out_shape=[jax.ShapeDtypeStruct((b, l, D_MODEL), F32),
                   jax.ShapeDtypeStruct((b, l, D_MODEL), BF16)],
        compiler_params=_cparams(("parallel", "parallel")),
        name="out_proj",
    )(x, o_r, oht, mod, gr, gh, wr, wh, g2)


FF_TM = 512
FF_SUB = 2 * LANES


def _ffn_kernel(x1_ref, xn_ref, xp_ref, xq_ref, mod_ref, wu_ref, cw_ref, cb_ref, wd_ref, gf_ref, o_ref, xe_s, h_s,
                *, seg, halo, on_grid, tiles_per_seq, final_norm):
    t = pl.program_id(0)
    tm = x1_ref.shape[1]
    rows = tm + 2 * halo
    if halo:
        first = (t % tiles_per_seq) == 0
        last = (t % tiles_per_seq) == tiles_per_seq - 1
        xe_s[0:halo, :] = jnp.where(first, jnp.zeros_like(xp_ref[0]), xp_ref[0])
        xe_s[halo + tm:rows, :] = jnp.where(last, jnp.zeros_like(xq_ref[0]), xq_ref[0])
    xe_s[halo:halo + tm, :] = xn_ref[0]

    pos = lax.broadcasted_iota(jnp.int32, (rows, 1), 0) % seg
    for c0 in range(0, D_FF, FF_SUB):
        w = min(FF_SUB, D_FF - c0)
        g = jnp.dot(xe_s[...], wu_ref[:, D_FF + c0:D_FF + c0 + w], preferred_element_type=F32)
        a = jnp.dot(xe_s[halo:halo + tm, :], wu_ref[:, c0:c0 + w], preferred_element_type=F32)
        gls = pltpu.roll(jnp.where(pos == seg - 1, 0.0, g), 1, axis=0)
        grs = pltpu.roll(jnp.where(pos == 0, 0.0, g), rows - 1, axis=0)
        acc = cb_ref[:, c0:c0 + w]
        for dr in ((-1, 0, 1) if on_grid else (0,)):
            lo = halo + dr * seg
            acc = (acc + cw_ref[dr + 1, 0:1, c0:c0 + w] * gls[lo:lo + tm]
                   + cw_ref[dr + 1, 1:2, c0:c0 + w] * g[lo:lo + tm]
                   + cw_ref[dr + 1, 2:3, c0:c0 + w] * grs[lo:lo + tm])
        h_s[:, c0:c0 + w] = (jax.nn.gelu(acc) * a).astype(BF16)
    y = jnp.dot(h_s[...], wd_ref[...], preferred_element_type=F32)
    x2 = x1_ref[0] + mod_ref[0, 5:6, :] * y
    if final_norm:
        x2 = _rms(x2) * gf_ref[...]
    o_ref[0] = x2


def _ffn(x1, xn2, mod, wu, cw, cb, wd, gf, on_grid, final_norm):
    b, l, _ = x1.shape
    tm = FF_TM
    if on_grid:
        seg, halo = GRID_W, GRID_W
        tps = l // tm
    else:
        seg, halo = l, 0
        tps = 1
    nt = b * l // tm
    hb = GRID_W
    nhb = tm // hb
    last_hb = b * l // hb - 1
    mod_map = (lambda t: (t // tps, 0, 0)) if mod.shape[0] > 1 else (lambda t: (0, 0, 0))
    kern = functools.partial(_ffn_kernel, seg=seg, halo=halo, on_grid=on_grid, tiles_per_seq=tps, final_norm=final_norm)
    const = lambda *shape: pl.BlockSpec(shape, lambda t: (0,) * len(shape), pipeline_mode=pl.Buffered(1))
    xh = xn2.reshape(b * l // hb, hb, D_MODEL)
    out = pl.pallas_call(
        kern,
        grid=(nt,),
        in_specs=[pl.BlockSpec((1, tm, D_MODEL), lambda t: (t, 0, 0)),
                  pl.BlockSpec((1, tm, D_MODEL), lambda t: (t, 0, 0)),
                  pl.BlockSpec((1, hb, D_MODEL), lambda t: (jnp.maximum(t * nhb - 1, 0), 0, 0)),
                  pl.BlockSpec((1, hb, D_MODEL), lambda t: (jnp.minimum((t + 1) * nhb, last_hb), 0, 0)),
                  pl.BlockSpec((1, N_MOD, D_MODEL), mod_map),
                  const(D_MODEL, 2 * D_FF), const(3, 3, D_FF), const(1, D_FF), const(D_FF, D_MODEL), const(1, D_MODEL)],
        out_specs=pl.BlockSpec((1, tm, D_MODEL), lambda t: (t, 0, 0)),
        out_shape=jax.ShapeDtypeStruct((nt, tm, D_MODEL), F32),
        scratch_shapes=[pltpu.VMEM((tm + 2 * halo, D_MODEL), BF16), pltpu.VMEM((tm, D_FF), BF16)],
        compiler_params=_cparams(("parallel",)),
        name="ffn",
    )(x1.reshape(nt, tm, D_MODEL), xn2.reshape(nt, tm, D_MODEL), xh, xh, mod, wu, cw, cb, wd, gf)
    return out.reshape(b, l, D_MODEL)


def _trunk_layer(x, mod, h0, p, fargs, tables, on_grid, final_norm, g_final):
    b, l, _ = x.shape
    tm = min(512, l)
    xy, hyt, kf = _front(x, mod, p['g_norm1'], p['wxy'], p['whyt'], *fargs, tables)
    o_r, states = _rglru(xy, p['rg_conv_w'], p['rg_conv_b'], p['wg'], p['bg'], p['ap'], h0)
    oht = _hyena(hyt, p['hy_cw'], p['hy_cb'], p['hy_bias'], kf, tables)
    x1, xn2 = _out_proj(x, o_r, oht, mod, p['g_rnn_out'], p['g_hy_out'], p['w_out_r'], p['w_out_h'], p['g_norm2'], tm)
    x2 = _ffn(x1, xn2, mod, p['w_up'], p['ffn_conv_w'], p['ffn_conv_b'], p['w_down'], g_final, on_grid, final_norm)
    return x2, states


def kernel(x_prompt, x_sample, state_rglru, c, c_ctx, w_ada, b_ada, g_norm1, g_norm2, w_in, rg_conv_w, rg_conv_b, rg_gate_w, rg_gate_b, rg_a, hy_conv_w, hy_conv_b, hf_w1, hf_b1, hf_w2, hf_b2, hf_w3, hf_freq, hy_bias, g_rnn_out, g_hy_out, w_out, w_up, ffn_conv_w, ffn_conv_b, w_down, g_final):
    depth = w_in.shape[0]
    nb_ctx, l_ctx, _ = x_prompt.shape
    nb_lat, l_lat, _ = x_sample.shape

    cc = jnp.zeros((SUBLANES, D_MODEL), F32).at[0].set(c_ctx).at[1:1 + nb_lat].set(c)
    mods = _modulation(cc, w_ada, b_ada).reshape(depth, SUBLANES, N_MOD, D_MODEL)

    tab_ctx = _dft_tables(2 * l_ctx // DFT2, _fft_rows(2 * l_ctx // DFT2, HY_CB))
    tab_lat = _dft_tables(2 * l_lat // DFT2, _fft_rows(2 * l_lat // DFT2, HY_CB))
    gf = g_final.reshape(1, D_MODEL)
    zero_h = jnp.zeros((nb_ctx, 2, D_RNN), F32)

    xp, xs = x_prompt, x_sample
    new_states = []
    for l in range(depth):
        wg, bg, ap = _gate_weights(rg_gate_w[l], rg_gate_b[l], rg_a[l])
        p = {
            'g_norm1': g_norm1[l].reshape(1, D_MODEL), 'g_norm2': g_norm2[l].reshape(1, D_MODEL),
            'wxy': w_in[l, :, :2 * D_RNN].astype(BF16), 'whyt': w_in[l, :, 2 * D_RNN:].T.astype(BF16),
            'rg_conv_w': rg_conv_w[l], 'rg_conv_b': rg_conv_b[l].reshape(1, D_RNN),
            'wg': wg, 'bg': bg, 'ap': ap,
            'hy_cw': hy_conv_w[l].reshape(3, 3, D_HY, 1), 'hy_cb': hy_conv_b[l].reshape(3, D_HY, 1),
            'hy_bias': hy_bias[l].reshape(2, D_HY, 1),
            'g_rnn_out': g_rnn_out[l].reshape(1, D_RNN), 'g_hy_out': g_hy_out[l].reshape(D_HY, 1),
            'w_out_r': w_out[l, :D_RNN].astype(BF16), 'w_out_h': w_out[l, D_RNN:].astype(BF16),
            'w_up': w_up[l].astype(BF16), 'ffn_conv_w': ffn_conv_w[l], 'ffn_conv_b': ffn_conv_b[l].reshape(1, D_FF),
            'w_down': w_down[l].astype(BF16),
        }
        fargs = (hf_w1[l], hf_b1[l], hf_w2[l], hf_b2[l], hf_w3[l], hf_freq[l])
        final = l == depth - 1
        xp, st = _trunk_layer(xp, mods[l, 0:1], zero_h, p, fargs, tab_ctx, False, final, gf)
        new_states.append(st)
        xs, _ = _trunk_layer(xs, mods[l, 1:1 + nb_lat], state_rglru[:, l], p, fargs, tab_lat, True, final, gf)
    return (xp, xs, jnp.stack(new_states, axis=1))
```

```python
import functools
import math

import jax
import jax.numpy as jnp
import ml_dtypes
import numpy as np
from jax import lax
from jax.experimental import pallas as pl
from jax.experimental.pallas import tpu as pltpu

F32 = jnp.float32
BF16 = jnp.bfloat16

D_MODEL = 1024
D_RNN = 512
D_HY = 512
N_HEADS = 8
HEAD_DIM = D_RNN // N_HEADS
RG_C = 8.0
GRID_W = 64
HY_BANDS = 16
HY_FH = 64
D_FF = 2816
N_MOD = 6
EPS = 1e-6

SUBLANES = 8
LANES = 128
VMEM_LIMIT = 56 * 1024 * 1024

DFT2 = LANES
FFT_PASSES = 2


def _cparams(sem):
    return pltpu.CompilerParams(dimension_semantics=sem, vmem_limit_bytes=VMEM_LIMIT)


def _rms(x):
    return x * lax.rsqrt(jnp.mean(x * x, axis=-1, keepdims=True) + EPS)


def _mod_kernel(c_ref, w_ref, b_ref, o_ref):
    c = c_ref[...]
    s = c * jax.nn.sigmoid(c)
    o_ref[0] = jnp.dot(s, w_ref[0], precision=lax.Precision.HIGHEST, preferred_element_type=F32) + b_ref[0]


def _modulation(cc, w_ada, b_ada):
    depth, _, n = w_ada.shape
    tn = 1536
    return pl.pallas_call(
        _mod_kernel,
        grid=(depth, n // tn),
        in_specs=[pl.BlockSpec((SUBLANES, D_MODEL), lambda l, j: (0, 0)),
                  pl.BlockSpec((1, D_MODEL, tn), lambda l, j: (l, 0, j)),
                  pl.BlockSpec((1, 1, tn), lambda l, j: (l, 0, j))],
        out_specs=pl.BlockSpec((1, SUBLANES, tn), lambda l, j: (l, 0, j)),
        out_shape=jax.ShapeDtypeStruct((depth, SUBLANES, n), F32),
        compiler_params=_cparams(("parallel", "parallel")),
        name="adaln_mod",
    )(cc, w_ada, b_ada.reshape(depth, 1, n))


RG_HALF = D_RNN // 2
RG_TILES = RG_HALF // LANES


def _rglru_kernel(*refs, seq, tc, nbk):
    for bb in range(nbk):
        _rglru_sequence(bb, *refs, seq=seq, tc=tc)


def _rglru_sequence(bb, xr_ref, yr_ref, cw_ref, cb_ref, wg_ref, bg_ref, ap_ref, h0_ref, o_ref, st_ref,
                    ext, xc_s, hf, hb, a_s, b_s, *, seq, tc):
    nchunks = seq // tc
    nblk = tc // SUBLANES
    ext[0:SUBLANES, :] = jnp.zeros((SUBLANES, RG_HALF), F32)
    ext[SUBLANES:SUBLANES + seq, :] = xr_ref[bb]
    ext[SUBLANES + seq:2 * SUBLANES + seq, :] = jnp.zeros((SUBLANES, RG_HALF), F32)
    for c in range(nchunks):
        xc = cb_ref[...] + cw_ref[0:1, :] * ext[pl.ds(c * tc + SUBLANES - 2, tc), :]
        for k in range(1, 4):
            xc = xc + cw_ref[k:k + 1, :] * ext[pl.ds(c * tc + SUBLANES - 2 + k, tc), :]
        xc_s[pl.ds(c * tc, tc), :] = xc

    row = lax.broadcasted_iota(jnp.int32, (SUBLANES, LANES), 0)
    half_neg_c_sp = [(-0.5 * RG_C) * jax.nn.softplus(-ap_ref[d, 0]) for d in range(2)]

    def gates(c0, d):
        xc = xc_s[pl.ds(c0, tc), :]
        g = jnp.dot(xc.astype(BF16), wg_ref[d, 0], preferred_element_type=F32) + bg_ref[d, 0]
        i = 0.5 * jnp.tanh(g[:, RG_HALF:]) + 0.5
        log_a = half_neg_c_sp[d] * jnp.tanh(g[:, :RG_HALF]) + half_neg_c_sp[d]
        a = jnp.exp(log_a)
        y = jnp.tanh(-log_a) * (a * a + 1.0)
        mult = jnp.where(y > 0.0, y * lax.rsqrt(y), 0.0)
        a_s[d] = a
        b_s[d] = xc * i * mult

    def local_scan(a, b, reverse):
        for s in (1, 2, 4):
            if reverse:
                keep = row < SUBLANES - s
                shift = SUBLANES - s
            else:
                keep = row >= s
                shift = s
            a_sh = jnp.where(keep, pltpu.roll(a, shift, axis=0), 1.0)
            b_sh = jnp.where(keep, pltpu.roll(b, shift, axis=0), 0.0)
            b = a * b_sh + b
            a = a * a_sh
        return a, b

    def block_body(cf0, cb0):
        def body(j, carry):
            rf = pl.multiple_of(j * SUBLANES, SUBLANES)
            rb = pl.multiple_of(tc - SUBLANES - j * SUBLANES, SUBLANES)
            out = []
            for t in range(RG_TILES):
                lanes = slice(t * LANES, (t + 1) * LANES)
                pa, pb = local_scan(a_s[0, pl.ds(rf, SUBLANES), lanes], b_s[0, pl.ds(rf, SUBLANES), lanes], False)
                h = pa * carry[2 * t] + pb
                hf[pl.ds(cf0 + rf, SUBLANES), lanes] = h
                out.append(jnp.broadcast_to(h[SUBLANES - 1:SUBLANES, :], (SUBLANES, LANES)))
                pa, pb = local_scan(a_s[1, pl.ds(rb, SUBLANES), lanes], b_s[1, pl.ds(rb, SUBLANES), lanes], True)
                h = pa * carry[2 * t + 1] + pb
                hb[pl.ds(cb0 + rb, SUBLANES), lanes] = h
                out.append(jnp.broadcast_to(h[0:1, :], (SUBLANES, LANES)))
            return tuple(out)
        return body

    carry = []
    for t in range(RG_TILES):
        carry.append(jnp.broadcast_to(h0_ref[bb, 0:1, t * LANES:(t + 1) * LANES], (SUBLANES, LANES)))
        carry.append(jnp.broadcast_to(h0_ref[bb, 1:2, t * LANES:(t + 1) * LANES], (SUBLANES, LANES)))
    carry = tuple(carry)
    for c in range(nchunks):
        cf0 = c * tc
        cb0 = (nchunks - 1 - c) * tc
        gates(cf0, 0)
        gates(cb0, 1)
        carry = lax.fori_loop(0, nblk, block_body(cf0, cb0), carry, unroll=2)
    for t in range(RG_TILES):
        st_ref[bb, 0:1, t * LANES:(t + 1) * LANES] = carry[2 * t][0:1, :]
        st_ref[bb, 1:2, t * LANES:(t + 1) * LANES] = carry[2 * t + 1][0:1, :]
    for c in range(nchunks):
        rows = pl.ds(c * tc, tc)
        o_ref[bb, rows, :] = (hf[rows, :] + hb[rows, :]) * jax.nn.gelu(yr_ref[bb, rows, :])


def _rglru(xy, cw, cb, wg, bg, ap, h0):
    b, seq, _ = xy.shape
    tc = min(512, seq)
    nh = D_RNN // RG_HALF
    nbk = max(1, min(b, 1024 // seq))
    assert b % nbk == 0
    kern = functools.partial(_rglru_kernel, seq=seq, tc=tc, nbk=nbk)
    return pl.pallas_call(
        kern,
        grid=(b // nbk, nh),
        in_specs=[pl.BlockSpec((nbk, seq, RG_HALF), lambda i, h: (i, 0, h)),
                  pl.BlockSpec((nbk, seq, RG_HALF), lambda i, h: (i, 0, nh + h)),
                  pl.BlockSpec((4, RG_HALF), lambda i, h: (0, h)),
                  pl.BlockSpec((1, RG_HALF), lambda i, h: (0, h)),
                  pl.BlockSpec((2, 1, RG_HALF, 2 * RG_HALF), lambda i, h: (0, h, 0, 0)),
                  pl.BlockSpec((2, 1, 1, 2 * RG_HALF), lambda i, h: (0, h, 0, 0)),
                  pl.BlockSpec((2, 1, 1, RG_HALF), lambda i, h: (0, h, 0, 0)),
                  pl.BlockSpec((nbk, 2, RG_HALF), lambda i, h: (i, 0, h))],
        out_specs=[pl.BlockSpec((nbk, seq, RG_HALF), lambda i, h: (i, 0, h)),
                   pl.BlockSpec((nbk, 2, RG_HALF), lambda i, h: (i, 0, h))],
        out_shape=[jax.ShapeDtypeStruct((b, seq, D_RNN), F32),
                   jax.ShapeDtypeStruct((b, 2, D_RNN), F32)],
        scratch_shapes=[pltpu.VMEM((seq + 2 * SUBLANES, RG_HALF), F32),
                        pltpu.VMEM((seq, RG_HALF), F32),
                        pltpu.VMEM((seq, RG_HALF), F32),
                        pltpu.VMEM((seq, RG_HALF), F32),
                        pltpu.VMEM((2, tc, RG_HALF), F32),
                        pltpu.VMEM((2, tc, RG_HALF), F32)],
        compiler_params=_cparams(("parallel", "parallel")),
        name="rglru",
    )(xy, xy, cw, cb, wg, bg, ap, h0)


def _gate_weights(gate_w, gate_b, a_param):
    nh = D_RNN // RG_HALF
    hp = N_HEADS // nh
    w = gate_w.reshape(2, 2, nh, hp, HEAD_DIM, HEAD_DIM)
    eye = jnp.eye(hp, dtype=F32)
    dense = jnp.einsum('dghpio,pq->dhpigqo', w, eye)
    dense = (0.5 * dense).reshape(2, nh, RG_HALF, 2 * RG_HALF).astype(BF16)
    bias = 0.5 * gate_b.reshape(2, 2, nh, RG_HALF).transpose(0, 2, 1, 3).reshape(2, nh, 1, 2 * RG_HALF)
    ap = a_param.reshape(2, nh, 1, RG_HALF)
    return dense, bias, ap


def _add(a, b):
    if a is None:
        return b
    if b is None:
        return a
    return a + b


def _sub(a, b):
    if b is None:
        return a
    if a is None:
        return -b
    return a - b


def _scale(a, s):
    if a is None or s == 0.0:
        return None
    if s == 1.0:
        return a
    if s == -1.0:
        return -a
    return a * s


def _cmul_const(z, wr, wi):
    if z is None:
        return None
    re, im = z
    if abs(wr) < 1e-15:
        wr = 0.0
    if abs(wi) < 1e-15:
        wi = 0.0
    if wr != 0.0 and abs(abs(wr) - abs(wi)) < 1e-15 and re is not None and im is not None:
        sr, si = math.copysign(1.0, wr), math.copysign(1.0, wi)
        return (_scale(_sub(_scale(re, sr), _scale(im, si)), abs(wr)), _scale(_add(_scale(re, si), _scale(im, sr)), abs(wr)))
    return (_sub(_scale(re, wr), _scale(im, wi)), _add(_scale(re, wi), _scale(im, wr)))


def _cadd(a, b):
    if a is None:
        return b
    if b is None:
        return a
    return (_add(a[0], b[0]), _add(a[1], b[1]))


def _csub(a, b):
    if b is None:
        return a
    if a is None:
        return (_sub(None, b[0]), _sub(None, b[1]))
    return (_sub(a[0], b[0]), _sub(a[1], b[1]))


def _cfft(xs, sign):
    n = len(xs)
    if n == 1:
        return list(xs)
    even = _cfft(xs[0::2], sign)
    odd = _cfft(xs[1::2], sign)
    out = [None] * n
    for k in range(n // 2):
        ang = sign * 2.0 * math.pi * k / n
        t = _cmul_const(odd[k], math.cos(ang), math.sin(ang))
        out[k] = _cadd(even[k], t)
        out[k + n // 2] = _csub(even[k], t)
    return out


def _fft64(get, put, sign, t_ref, rc, nin, nout):
    r8 = 8
    for p in range(r8):
        if sign < 0:
            xs = [get(r8 * a + p) if r8 * a + p < nin else None for a in range(r8)]
        else:
            xs = [get(p + r8 * d) for d in range(r8)]
        ts = _cfft(xs, sign)
        for q in range(r8):
            ang = sign * 2.0 * math.pi * p * q / 64.0
            re, im = _cmul_const(ts[q], math.cos(ang), math.sin(ang))
            rows = pl.ds((p * r8 + q) * rc, rc)
            t_ref[rows, 0:LANES] = re if re is not None else jnp.zeros_like(im)
            t_ref[rows, LANES:2 * LANES] = im if im is not None else jnp.zeros_like(re)
    for q in range(r8):
        ys = [(t_ref[pl.ds((p * r8 + q) * rc, rc), 0:LANES], t_ref[pl.ds((p * r8 + q) * rc, rc), LANES:2 * LANES])
              for p in range(r8)]
        zs = _cfft(ys, sign)
        for m in range(r8):
            k = q + r8 * m if sign < 0 else r8 * m + q
            if k < nout:
                put(k, zs[m])


def _dft_tables(n1, rc):
    n = n1 * DFT2
    k = np.arange(DFT2, dtype=np.float64)
    ang = 2.0 * np.pi * np.outer(k, k) / DFT2
    c, s = np.cos(ang), np.sin(ang)
    fwd = np.block([[c, -s], [s, c]])
    inv = np.block([[c, s], [-s, c]]) / n

    def split(m):
        hi = m.astype(np.float32).astype(ml_dtypes.bfloat16)
        lo = (m - hi.astype(np.float64)).astype(ml_dtypes.bfloat16)
        return jnp.asarray(hi), jnp.asarray(lo)

    tw_ang = 2.0 * np.pi * np.outer(np.arange(n1, dtype=np.float64), k) / n
    tw = np.concatenate([np.cos(tw_ang), -np.sin(tw_ang)], axis=1)
    tw = np.repeat(tw, rc, axis=0).astype(np.float32)
    return split(fwd) + split(inv) + (jnp.asarray(tw),)


HY_CB = 32


def _fft_rows(n1, cb):
    return SUBLANES if n1 >= 16 else cb


def _dft_mm(x, hi_ref, lo_ref):
    xh = x.astype(BF16)
    acc = jnp.dot(xh, hi_ref[...], preferred_element_type=F32)
    if FFT_PASSES >= 2:
        acc = acc + jnp.dot(xh, lo_ref[...], preferred_element_type=F32)
    if FFT_PASSES >= 3:
        xl = (x - xh.astype(F32)).astype(BF16)
        acc = acc + jnp.dot(xl, hi_ref[...], preferred_element_type=F32)
    return acc


def _filter_hidden(feat_ref, w1_ref, b1_ref, w2_ref, b2_ref, fr_ref, h2h_s, h2l_s):
    hi = lax.Precision.HIGHEST

    @pl.when(pl.program_id(0) == 0)
    def _():
        h = jnp.dot(w1_ref[...], feat_ref[...], precision=hi, preferred_element_type=F32) + b1_ref[...]
        h = jnp.sin(fr_ref[:, 0:1] * h)
        h = jnp.dot(w2_ref[...], h, precision=hi, preferred_element_type=F32) + b2_ref[...]
        h2 = jnp.sin(fr_ref[:, 1:2] * h)
        h2h = h2.astype(BF16)
        h2h_s[...] = h2h
        h2l_s[...] = (h2 - h2h.astype(F32)).astype(BF16)


def _filter_block(w3_ref, fh_ref, fl_ref, tw_ref, kf_ref, h2h_s, h2l_s, k_s, s_s, t_s, *, seq, n1, cb, rc):
    n = 2 * seq
    lane = lax.broadcasted_iota(jnp.int32, (cb, n), 1)
    pos = jnp.where(lane < seq, lane, n - lane).astype(F32)
    t = pos / float(max(seq - 1, 1))
    d_idx = (lax.broadcasted_iota(jnp.int32, (cb, n), 0) + pl.program_id(0) * cb).astype(F32)
    min_decay = math.log(1e-2) / 1.5
    max_decay = math.log(1e-2) / 0.3
    delta = jnp.abs(min_decay + d_idx * ((max_decay - min_decay) / (D_HY - 1)))
    decay = jnp.exp(-t * delta)
    w3 = w3_ref[...].reshape(4 * cb, HY_FH)
    w3h = w3.astype(BF16)
    w3l = (w3 - w3h.astype(F32)).astype(BF16)
    hfb = (jnp.dot(w3h, h2h_s[...], preferred_element_type=F32) + jnp.dot(w3h, h2l_s[...], preferred_element_type=F32)
           + jnp.dot(w3l, h2h_s[...], preferred_element_type=F32))
    for o in range(2):
        hf = hfb[(2 * o) * cb:(2 * o + 1) * cb]
        hb = hfb[(2 * o + 1) * cb:(2 * o + 2) * cb]
        k_s[o] = jnp.where(lane < seq, hf, jnp.where(lane == seq, 0.0, hb)) * decay

    nrc = cb // rc
    for o in range(2):
        for i in range(nrc):
            r0 = i * rc
            base = i * (n1 * rc)

            def get(j, o=o, r0=r0):
                return (k_s[o, pl.ds(r0, rc), j * LANES:(j + 1) * LANES], None)

            def put(k1, z, base=base):
                re, im = z
                if k1 > 0:
                    twr = tw_ref[k1 * rc:(k1 + 1) * rc, 0:LANES]
                    twi = tw_ref[k1 * rc:(k1 + 1) * rc, LANES:2 * LANES]
                    re, im = (re * twr, re * twi) if im is None else (re * twr - im * twi, re * twi + im * twr)
                s_s[pl.ds(base + k1 * rc, rc), 0:LANES] = re
                s_s[pl.ds(base + k1 * rc, rc), LANES:2 * LANES] = im if im is not None else jnp.zeros_like(re)

            if n1 == 64:
                _fft64(get, put, -1, t_s, rc, n1, n1)
            else:
                zs = _cfft([get(j) for j in range(n1)], -1)
                for k1 in range(n1):
                    put(k1, zs[k1])
        x = s_s[...]
        xh = x.astype(BF16)
        xl = (x - xh.astype(F32)).astype(BF16)
        z = (jnp.dot(xh, fh_ref[...], preferred_element_type=F32) + jnp.dot(xh, fl_ref[...], preferred_element_type=F32)
             + jnp.dot(xl, fh_ref[...], preferred_element_type=F32))
        for i in range(nrc):
            kf_ref[o, i] = z[i * n1 * rc:(i + 1) * n1 * rc, :]


FRONT_TM = 512


def _front_kernel(*refs, seq, n1, cb, rc, ffn):
    refs = list(refs)
    (x_ref, mod_ref, g_ref, wxy_ref, whyt_ref, feat_ref, w1_ref, b1_ref, w2_ref, b2_ref, fr_ref, w3_ref,
     fh_ref, fl_ref, tw_ref) = refs[:15]
    del refs[:15]
    if ffn is not None:
        ffn_in = refs[:8]
        del refs[:8]
    xy_ref, hyt_ref, kf_ref = refs[:3]
    del refs[:3]
    if ffn is not None:
        ffn_out = refs.pop(0)
    h2h_s, h2l_s, k_s, s_s, t_s = refs[:5]
    _filter_hidden(feat_ref, w1_ref, b1_ref, w2_ref, b2_ref, fr_ref, h2h_s, h2l_s)
    for q in range(x_ref.shape[0]):
        xn = _rms(x_ref[q]) * g_ref[...]
        xn = (xn * (1.0 + mod_ref[0, 1:2, :]) + mod_ref[0, 0:1, :]).astype(BF16)
        xy_ref[q] = jnp.dot(xn, wxy_ref[...], preferred_element_type=F32)
        hyt_ref[q] = lax.dot_general(whyt_ref[...], xn, (((1,), (1,)), ((), ())), preferred_element_type=F32)
    _filter_block(w3_ref, fh_ref, fl_ref, tw_ref, kf_ref, h2h_s, h2l_s, k_s, s_s, t_s, seq=seq, n1=n1, cb=cb, rc=rc)
    if ffn is not None:
        x1_ref, xn_ref = ffn_in[:2]
        _ffn_kernel(x1_ref, xn_ref, xn_ref, xn_ref, *ffn_in[2:], ffn_out, refs[5], refs[6], seg=ffn['seg'], halo=0,
                    on_grid=False, tiles_per_seq=1, final_norm=ffn['final_norm'])


def _front(x, mod, g, wxy, whyt, w1, b1, w2, b2, w3, freq, tables, ffn_args=None):
    b, seq, _ = x.shape
    n1 = 2 * seq // DFT2
    n = 2 * seq
    rc = _fft_rows(n1, HY_CB)
    tmb = min(FRONT_TM, seq)
    nbt = FRONT_TM // tmb
    steps = b * seq // FRONT_TM
    cb = D_HY // steps
    assert cb % rc == 0 and cb * steps == D_HY
    tps = seq // tmb
    fh, fl, _, _, tw = tables
    pos = np.arange(n, dtype=np.float64)
    pos = np.where(pos < seq, pos, n - pos)
    tt = pos / max(seq - 1, 1)
    omega = 2.0 * math.pi * pos / seq
    bands = np.linspace(1e-4, HY_BANDS - 1, HY_BANDS)
    ang = omega[None, :] * bands[:, None]
    feats = np.concatenate([tt[None, :], np.cos(ang), np.sin(ang)], axis=0)
    nfeat = 5 * SUBLANES
    feats = np.pad(feats, ((0, nfeat - feats.shape[0]), (0, 0))).astype(np.float32)
    w1 = jnp.pad(w1, ((0, nfeat - w1.shape[0]), (0, 0)))
    w3t = w3.T.reshape(4, D_HY, HY_FH)
    const = lambda *shape: pl.BlockSpec(shape, lambda i: (0,) * len(shape), pipeline_mode=pl.Buffered(1))
    mod_map = (lambda i: (i // tps, 0, 0)) if mod.shape[0] > 1 else (lambda i: (0, 0, 0))
    ffn, ffn_in, ffn_specs, ffn_out_specs, ffn_out_shape, ffn_scratch = None, [], [], [], [], []
    if ffn_args is not None:
        x1f, xnf, modf, wu, cw, cbf, wd, gf, final_norm = ffn_args
        bf_, lf, _ = x1f.shape
        tmf = bf_ * lf // steps
        assert tmf % lf == 0 and modf.shape[0] == 1, "piggy-backed FFN tiles must hold whole sequences"
        ffn = {'seg': lf, 'final_norm': final_norm}
        ffn_in = [x1f.reshape(steps, tmf, D_MODEL), xnf.reshape(steps, tmf, D_MODEL), modf, wu, cw, cbf, wd, gf]
        ffn_specs = [pl.BlockSpec((1, tmf, D_MODEL), lambda i: (i, 0, 0)), pl.BlockSpec((1, tmf, D_MODEL), lambda i: (i, 0, 0)),
                     const(1, N_MOD, D_MODEL), const(D_MODEL, 2 * D_FF), const(3, 3, D_FF), const(1, D_FF),
                     const(D_FF, D_MODEL), const(1, D_MODEL)]
        ffn_out_specs = [pl.BlockSpec((1, tmf, D_MODEL), lambda i: (i, 0, 0))]
        ffn_out_shape = [jax.ShapeDtypeStruct((steps, tmf, D_MODEL), F32)]
        ffn_scratch = [pltpu.VMEM((tmf, D_MODEL), BF16), pltpu.VMEM((tmf, D_FF), BF16)]
    kern = functools.partial(_front_kernel, seq=seq, n1=n1, cb=cb, rc=rc, ffn=ffn)
    outs = pl.pallas_call(
        kern,
        grid=(steps,),
        in_specs=[pl.BlockSpec((nbt, tmb, D_MODEL), lambda i: (i, 0, 0)),
                  pl.BlockSpec((1, N_MOD, D_MODEL), mod_map),
                  const(1, D_MODEL), const(D_MODEL, 2 * D_RNN), const(3 * D_HY, D_MODEL),
                  const(nfeat, n), const(HY_FH, nfeat), const(HY_FH, 1), const(HY_FH, HY_FH), const(HY_FH, 1),
                  const(HY_FH, 2),
                  pl.BlockSpec((4, cb, HY_FH), lambda i: (0, i, 0)),
                  const(2 * DFT2, 2 * DFT2), const(2 * DFT2, 2 * DFT2), const(n1 * rc, 2 * DFT2)] + ffn_specs,
        out_specs=[pl.BlockSpec((nbt, tmb, 2 * D_RNN), lambda i: (i, 0, 0)),
                   pl.BlockSpec((nbt, 3 * D_HY, tmb), lambda i: (i // tps, 0, i % tps)),
                   pl.BlockSpec((2, cb // rc, n1 * rc, 2 * DFT2), lambda i: (0, i, 0, 0))] + ffn_out_specs,
        out_shape=[jax.ShapeDtypeStruct((b * seq // tmb, tmb, 2 * D_RNN), F32),
                   jax.ShapeDtypeStruct((b, 3 * D_HY, seq), F32),
                   jax.ShapeDtypeStruct((2, D_HY // rc, n1 * rc, 2 * DFT2), F32)] + ffn_out_shape,
        scratch_shapes=[pltpu.VMEM((HY_FH, n), BF16), pltpu.VMEM((HY_FH, n), BF16), pltpu.VMEM((2, cb, n), F32),
                        pltpu.VMEM((n1 * cb, 2 * DFT2), F32), pltpu.VMEM((n1 * rc, 2 * DFT2), F32)] + ffn_scratch,
        compiler_params=_cparams(("arbitrary",)),
        name="front" if ffn is None else "front_ffn",
    )(x.reshape(b * seq // tmb, tmb, D_MODEL), mod, g, wxy, whyt,
      jnp.asarray(feats), w1.T, b1.reshape(HY_FH, 1), w2.T, b2.reshape(HY_FH, 1), freq.T, w3t, fh, fl, tw, *ffn_in)
    xy, hyt, kf = outs[:3]
    res = (xy.reshape(b, seq, 2 * D_RNN), hyt, kf)
    if ffn is not None:
        res = res + (outs[3].reshape(x1f.shape),)
    return res


HY_UNIT = 512


def _hyena_kernel(hy_ref, cw_ref, cb_ref, bias_ref, kf_ref, fh_ref, fl_ref, ih_ref, il_ref, tw_ref, o_ref,
                  buf, sa, sb, ta, tc, *, seq, n1, cb, nb, rc, pu):
    npairs = nb // 2
    nin = n1 // 2
    nrc = cb // rc
    upo = (npairs // pu) * nrc
    nitems = 2 * upo
    assert upo >= 3, "an item's second-order stage A must come after its first-order stage C"
    lane = lax.broadcasted_iota(jnp.int32, (rc, seq), 1)

    def conv3(h, part, r0):
        w0 = cw_ref[0, part, pl.ds(r0, rc), :]
        w1 = cw_ref[1, part, pl.ds(r0, rc), :]
        w2 = cw_ref[2, part, pl.ds(r0, rc), :]
        bb = cb_ref[part, pl.ds(r0, rc), :]
        hm = jnp.where(lane == 0, 0.0, pltpu.roll(h, 1, axis=1))
        hp = jnp.where(lane == seq - 1, 0.0, pltpu.roll(h, seq - 1, axis=1))
        return w0 * hm + w1 * h + w2 * hp + bb

    def prep(i, _):
        b = i // nrc
        r0 = pl.multiple_of((i % nrc) * rc, rc)
        buf[0, b, pl.ds(r0, rc), :] = conv3(hy_ref[b, 0, pl.ds(r0, rc), :], 0, r0)
        return 0
    lax.fori_loop(0, nb * nrc, prep, 0, unroll=2)

    def item(j):
        o = j // upo
        u = j % upo
        return o, u // nrc, u % nrc

    def stage_a(j, slot):
        o, pg, rg = item(j)
        r0 = pl.multiple_of(rg * rc, rc)
        for pp in range(pu):
            p = pg * pu + pp
            def get(jb, p=p):
                return (buf[o, 2 * p, pl.ds(r0, rc), jb * LANES:(jb + 1) * LANES],
                        buf[o, 2 * p + 1, pl.ds(r0, rc), jb * LANES:(jb + 1) * LANES])

            def put(k1, z, pp=pp):
                re, im = z
                if k1 > 0:
                    twr = tw_ref[k1 * rc:(k1 + 1) * rc, 0:LANES]
                    twi = tw_ref[k1 * rc:(k1 + 1) * rc, LANES:2 * LANES]
                    re, im = re * twr - im * twi, re * twi + im * twr
                rows = pl.ds((pp * n1 + k1) * rc, rc)
                sa[slot, rows, 0:LANES] = re
                sa[slot, rows, LANES:2 * LANES] = im

            if n1 == 64:
                _fft64(get, put, -1, ta, rc, nin, n1)
            else:
                zs = _cfft([get(jb) if jb < nin else None for jb in range(n1)], -1)
                for k1 in range(n1):
                    put(k1, zs[k1])

    def stage_b(j, slot):
        o, _, rg = item(j)
        z = _dft_mm(sa[slot], fh_ref, fl_ref)
        kk = kf_ref[o, rg]
        if pu > 1:
            kk = jnp.concatenate([kk] * pu, axis=0)
        zr, zi = z[:, :LANES], z[:, LANES:]
        kr, ki = kk[:, :LANES], kk[:, LANES:]
        w = jnp.concatenate([zr * kr - zi * ki, zr * ki + zi * kr], axis=1)
        sb[slot] = _dft_mm(w, ih_ref, il_ref)

    def stage_c(j, slot):
        o, pg, rg = item(j)
        r0 = pl.multiple_of(rg * rc, rc)
        bias = bias_ref[o, pl.ds(r0, rc), :]
        for pp in range(pu):
            p = pg * pu + pp
            xg = [conv3(hy_ref[2 * p + q, 1 + o, pl.ds(r0, rc), :], 1 + o, r0) for q in range(2)]

            def get(k1, pp=pp):
                re = sb[slot, pl.ds((pp * n1 + k1) * rc, rc), 0:LANES]
                im = sb[slot, pl.ds((pp * n1 + k1) * rc, rc), LANES:2 * LANES]
                if k1 > 0:
                    twr = tw_ref[k1 * rc:(k1 + 1) * rc, 0:LANES]
                    twi = tw_ref[k1 * rc:(k1 + 1) * rc, LANES:2 * LANES]
                    re, im = re * twr + im * twi, im * twr - re * twi
                return (re, im)

            def put(jb, y, p=p, xg=xg):
                lanes = slice(jb * LANES, (jb + 1) * LANES)
                for q in range(2):
                    u = buf[o, 2 * p + q, pl.ds(r0, rc), lanes]
                    buf[o + 1, 2 * p + q, pl.ds(r0, rc), lanes] = xg[q][:, lanes] * (y[q] + u * bias)

            if n1 == 64:
                _fft64(get, put, +1, tc, rc, n1, nin)
            else:
                ys = _cfft([get(k1) for k1 in range(n1)], +1)
                for jb in range(nin):
                    put(jb, ys[jb])

    stage_a(0, 0)
    stage_a(1, 1)
    stage_b(0, 0)

    assert nitems % 2 == 0

    def steady(i, _):
        t = 2 + 2 * i
        stage_b(t - 1, 1)
        stage_a(t, 0)
        stage_c(t - 2, 0)
        stage_b(t, 0)
        stage_a(t + 1, 1)
        stage_c(t - 1, 1)
        return 0
    lax.fori_loop(0, (nitems - 2) // 2, steady, 0)
    stage_b(nitems - 1, (nitems - 1) % 2)
    stage_c(nitems - 2, nitems % 2)
    stage_c(nitems - 1, (nitems - 1) % 2)
    for b in range(nb):
        o_ref[b] = buf[2, b]


def _hyena(hyt, cw, cbias, bias, kf, tables):
    nb, _, seq = hyt.shape
    n1 = 2 * seq // DFT2
    cb = HY_CB
    rc = _fft_rows(n1, cb)
    pu = HY_UNIT // (n1 * rc)
    nblk = D_HY // cb
    fh, fl, ih, il, tw = tables
    kern = functools.partial(_hyena_kernel, seq=seq, n1=n1, cb=cb, nb=nb, rc=rc, pu=pu)
    full = lambda *shape: pl.BlockSpec(shape, lambda i: (0,) * len(shape))
    return pl.pallas_call(
        kern,
        grid=(nblk,),
        in_specs=[pl.BlockSpec((nb, 3, cb, seq), lambda i: (0, 0, i, 0)),
                  pl.BlockSpec((3, 3, cb, 1), lambda i: (0, 0, i, 0)),
                  pl.BlockSpec((3, cb, 1), lambda i: (0, i, 0)),
                  pl.BlockSpec((2, cb, 1), lambda i: (0, i, 0)),
                  pl.BlockSpec((2, cb // rc, n1 * rc, 2 * DFT2), lambda i: (0, i, 0, 0)),
                  full(2 * DFT2, 2 * DFT2), full(2 * DFT2, 2 * DFT2),
                  full(2 * DFT2, 2 * DFT2), full(2 * DFT2, 2 * DFT2),
                  full(n1 * rc, 2 * DFT2)],
        out_specs=pl.BlockSpec((nb, cb, seq), lambda i: (0, i, 0)),
        out_shape=jax.ShapeDtypeStruct((nb, D_HY, seq), F32),
        scratch_shapes=[pltpu.VMEM((3, nb, cb, seq), F32),
                        pltpu.VMEM((2, HY_UNIT, 2 * DFT2), F32), pltpu.VMEM((2, HY_UNIT, 2 * DFT2), F32),
                        pltpu.VMEM((n1 * rc, 2 * DFT2), F32), pltpu.VMEM((n1 * rc, 2 * DFT2), F32)],
        compiler_params=_cparams(("parallel",)),
        name="hyena",
    )(hyt.reshape(nb, 3, D_HY, seq), cw, cbias, bias, kf, fh, fl, ih, il, tw)


def _out_proj_kernel(x_ref, or_ref, oht_ref, mod_ref, gr_ref, gh_ref, wr_ref, wh_ref, g2_ref, x1_ref, xn_ref):
    orn = (_rms(or_ref[0]) * gr_ref[...]).astype(BF16)
    oh = oht_ref[0]
    ohn = oh * lax.rsqrt(jnp.mean(oh * oh, axis=0, keepdims=True) + EPS) * gh_ref[...]
    o = jnp.dot(orn, wr_ref[...], preferred_element_type=F32)
    o = o + lax.dot_general(ohn.astype(BF16), wh_ref[...], (((0,), (0,)), ((), ())), preferred_element_type=F32)
    x1 = x_ref[0] + mod_ref[0, 2:3, :] * o
    x1_ref[0] = x1
    xn = _rms(x1) * g2_ref[...]
    xn_ref[0] = (xn * (1.0 + mod_ref[0, 4:5, :]) + mod_ref[0, 3:4, :]).astype(BF16)


def _out_proj(x, o_r, oht, mod, gr, gh, wr, wh, g2, tm):
    b, l, _ = x.shape
    mod_map = (lambda i, j: (i, 0, 0)) if mod.shape[0] > 1 else (lambda i, j: (0, 0, 0))
    return pl.pallas_call(
        _out_proj_kernel,
        grid=(b, l // tm),
        in_specs=[pl.BlockSpec((1, tm, D_MODEL), lambda i, j: (i, j, 0)),
                  pl.BlockSpec((1, tm, D_RNN), lambda i, j: (i, j, 0)),
                  pl.BlockSpec((1, D_HY, tm), lambda i, j: (i, 0, j)),
                  pl.BlockSpec((1, N_MOD, D_MODEL), mod_map),
                  pl.BlockSpec((1, D_RNN), lambda i, j: (0, 0)),
                  pl.BlockSpec((D_HY, 1), lambda i, j: (0, 0)),
                  pl.BlockSpec((D_RNN, D_MODEL), lambda i, j: (0, 0)),
                  pl.BlockSpec((D_HY, D_MODEL), lambda i, j: (0, 0)),
                  pl.BlockSpec((1, D_MODEL), lambda i, j: (0, 0))],
        out_specs=[pl.BlockSpec((1, tm, D_MODEL), lambda i, j: (i, j, 0)),
                   pl.BlockSpec((1, tm, D_MODEL), lambda i, j: (i, j, 0))],
        out_shape=[jax.ShapeDtypeStruct((b, l, D_MODEL), F32),
                   jax.ShapeDtypeStruct((b, l, D_MODEL), BF16)],
        compiler_params=_cparams(("parallel", "parallel")),
        name="out_proj",
    )(x, o_r, oht, mod, gr, gh, wr, wh, g2)


FF_TM = 512
FF_SUB = 2 * LANES


def _ffn_kernel(x1_ref, xn_ref, xp_ref, xq_ref, mod_ref, wu_ref, cw_ref, cb_ref, wd_ref, gf_ref, o_ref, xe_s, h_s,
                *, seg, halo, on_grid, tiles_per_seq, final_norm):
    t = pl.program_id(0)
    tm = x1_ref.shape[1]
    rows = tm + 2 * halo
    if halo:
        first = (t % tiles_per_seq) == 0
        last = (t % tiles_per_seq) == tiles_per_seq - 1
        xe_s[0:halo, :] = jnp.where(first, jnp.zeros_like(xp_ref[0]), xp_ref[0])
        xe_s[halo + tm:rows, :] = jnp.where(last, jnp.zeros_like(xq_ref[0]), xq_ref[0])
    xe_s[halo:halo + tm, :] = xn_ref[0]

    pos = lax.broadcasted_iota(jnp.int32, (rows, 1), 0) % seg
    for c0 in range(0, D_FF, FF_SUB):
        w = min(FF_SUB, D_FF - c0)
        g = jnp.dot(xe_s[...], wu_ref[:, D_FF + c0:D_FF + c0 + w], preferred_element_type=F32)
        a = jnp.dot(xe_s[halo:halo + tm, :], wu_ref[:, c0:c0 + w], preferred_element_type=F32)
        gls = pltpu.roll(jnp.where(pos == seg - 1, 0.0, g), 1, axis=0)
        grs = pltpu.roll(jnp.where(pos == 0, 0.0, g), rows - 1, axis=0)
        acc = cb_ref[:, c0:c0 + w]
        for dr in ((-1, 0, 1) if on_grid else (0,)):
            lo = halo + dr * seg
            acc = (acc + cw_ref[dr + 1, 0:1, c0:c0 + w] * gls[lo:lo + tm]
                   + cw_ref[dr + 1, 1:2, c0:c0 + w] * g[lo:lo + tm]
                   + cw_ref[dr + 1, 2:3, c0:c0 + w] * grs[lo:lo + tm])
        h_s[:, c0:c0 + w] = (jax.nn.gelu(acc) * a).astype(BF16)
    y = jnp.dot(h_s[...], wd_ref[...], preferred_element_type=F32)
    x2 = x1_ref[0] + mod_ref[0, 5:6, :] * y
    if final_norm:
        x2 = _rms(x2) * gf_ref[...]
    o_ref[0] = x2


def _ffn(x1, xn2, mod, wu, cw, cb, wd, gf, on_grid, final_norm):
    b, l, _ = x1.shape
    tm = FF_TM
    if on_grid:
        seg, halo = GRID_W, GRID_W
        tps = l // tm
    else:
        seg, halo = l, 0
        tps = 1
    nt = b * l // tm
    hb = GRID_W
    nhb = tm // hb
    last_hb = b * l // hb - 1
    mod_map = (lambda t: (t // tps, 0, 0)) if mod.shape[0] > 1 else (lambda t: (0, 0, 0))
    kern = functools.partial(_ffn_kernel, seg=seg, halo=halo, on_grid=on_grid, tiles_per_seq=tps, final_norm=final_norm)
    const = lambda *shape: pl.BlockSpec(shape, lambda t: (0,) * len(shape), pipeline_mode=pl.Buffered(1))
    xh = xn2.reshape(b * l // hb, hb, D_MODEL)
    out = pl.pallas_call(
        kern,
        grid=(nt,),
        in_specs=[pl.BlockSpec((1, tm, D_MODEL), lambda t: (t, 0, 0)),
                  pl.BlockSpec((1, tm, D_MODEL), lambda t: (t, 0, 0)),
                  pl.BlockSpec((1, hb, D_MODEL), lambda t: (jnp.maximum(t * nhb - 1, 0), 0, 0)),
                  pl.BlockSpec((1, hb, D_MODEL), lambda t: (jnp.minimum((t + 1) * nhb, last_hb), 0, 0)),
                  pl.BlockSpec((1, N_MOD, D_MODEL), mod_map),
                  const(D_MODEL, 2 * D_FF), const(3, 3, D_FF), const(1, D_FF), const(D_FF, D_MODEL), const(1, D_MODEL)],
        out_specs=pl.BlockSpec((1, tm, D_MODEL), lambda t: (t, 0, 0)),
        out_shape=jax.ShapeDtypeStruct((nt, tm, D_MODEL), F32),
        scratch_shapes=[pltpu.VMEM((tm + 2 * halo, D_MODEL), BF16), pltpu.VMEM((tm, D_FF), BF16)],
        compiler_params=_cparams(("parallel",)),
        name="ffn",
    )(x1.reshape(nt, tm, D_MODEL), xn2.reshape(nt, tm, D_MODEL), xh, xh, mod, wu, cw, cb, wd, gf)
    return out.reshape(b, l, D_MODEL)


def _mixers(front_out, x, mod, h0, p, tables):
    xy, hyt, kf = front_out
    tm = min(512, x.shape[1])
    o_r, states = _rglru(xy, p['rg_conv_w'], p['rg_conv_b'], p['wg'], p['bg'], p['ap'], h0)
    oht = _hyena(hyt, p['hy_cw'], p['hy_cb'], p['hy_bias'], kf, tables)
    x1, xn2 = _out_proj(x, o_r, oht, mod, p['g_rnn_out'], p['g_hy_out'], p['w_out_r'], p['w_out_h'], p['g_norm2'], tm)
    return x1, xn2, states


def kernel(x_prompt, x_sample, state_rglru, c, c_ctx, w_ada, b_ada, g_norm1, g_norm2, w_in, rg_conv_w, rg_conv_b, rg_gate_w, rg_gate_b, rg_a, hy_conv_w, hy_conv_b, hf_w1, hf_b1, hf_w2, hf_b2, hf_w3, hf_freq, hy_bias, g_rnn_out, g_hy_out, w_out, w_up, ffn_conv_w, ffn_conv_b, w_down, g_final):
    depth = w_in.shape[0]
    nb_ctx, l_ctx, _ = x_prompt.shape
    nb_lat, l_lat, _ = x_sample.shape

    cc = jnp.zeros((SUBLANES, D_MODEL), F32).at[0].set(c_ctx).at[1:1 + nb_lat].set(c)
    mods = _modulation(cc, w_ada, b_ada).reshape(depth, SUBLANES, N_MOD, D_MODEL)

    tab_ctx = _dft_tables(2 * l_ctx // DFT2, _fft_rows(2 * l_ctx // DFT2, HY_CB))
    tab_lat = _dft_tables(2 * l_lat // DFT2, _fft_rows(2 * l_lat // DFT2, HY_CB))
    gf = g_final.reshape(1, D_MODEL)
    zero_h = jnp.zeros((nb_ctx, 2, D_RNN), F32)

    xp, xs = x_prompt, x_sample
    new_states = []
    for l in range(depth):
        wg, bg, ap = _gate_weights(rg_gate_w[l], rg_gate_b[l], rg_a[l])
        p = {
            'g_norm1': g_norm1[l].reshape(1, D_MODEL), 'g_norm2': g_norm2[l].reshape(1, D_MODEL),
            'wxy': w_in[l, :, :2 * D_RNN].astype(BF16), 'whyt': w_in[l, :, 2 * D_RNN:].T.astype(BF16),
            'rg_conv_w': rg_conv_w[l], 'rg_conv_b': rg_conv_b[l].reshape(1, D_RNN),
            'wg': wg, 'bg': bg, 'ap': ap,
            'hy_cw': hy_conv_w[l].reshape(3, 3, D_HY, 1), 'hy_cb': hy_conv_b[l].reshape(3, D_HY, 1),
            'hy_bias': hy_bias[l].reshape(2, D_HY, 1),
            'g_rnn_out': g_rnn_out[l].reshape(1, D_RNN), 'g_hy_out': g_hy_out[l].reshape(D_HY, 1),
            'w_out_r': w_out[l, :D_RNN].astype(BF16), 'w_out_h': w_out[l, D_RNN:].astype(BF16),
            'w_up': w_up[l].astype(BF16), 'ffn_conv_w': ffn_conv_w[l], 'ffn_conv_b': ffn_conv_b[l].reshape(1, D_FF),
            'w_down': w_down[l].astype(BF16),
        }
        fargs = (hf_w1[l], hf_b1[l], hf_w2[l], hf_b2[l], hf_w3[l], hf_freq[l])
        final = l == depth - 1
        mod_ctx, mod_lat = mods[l, 0:1], mods[l, 1:1 + nb_lat]
        ffn_w = (p['w_up'], p['ffn_conv_w'], p['ffn_conv_b'], p['w_down'], gf)
        front_ctx = _front(xp, mod_ctx, p['g_norm1'], p['wxy'], p['whyt'], *fargs, tab_ctx)
        x1c, xn2c, st = _mixers(front_ctx, xp, mod_ctx, zero_h, p, tab_ctx)
        new_states.append(st)
        *front_lat, xp = _front(xs, mod_lat, p['g_norm1'], p['wxy'], p['whyt'], *fargs, tab_lat,
                                ffn_args=(x1c, xn2c, mod_ctx) + ffn_w + (final,))
        x1s, xn2s, _ = _mixers(front_lat, xs, mod_lat, state_rglru[:, l], p, tab_lat)
        xs = _ffn(x1s, xn2s, mod_lat, *ffn_w, True, final)
    return (xp, xs, jnp.stack(new_states, axis=1))
```

```python
import functools
import math

import jax
import jax.numpy as jnp
import ml_dtypes
import numpy as np
from jax import lax
from jax.experimental import pallas as pl
from jax.experimental.pallas import tpu as pltpu

F32 = jnp.float32
BF16 = jnp.bfloat16

D_MODEL = 1024
D_RNN = 512
D_HY = 512
N_HEADS = 8
HEAD_DIM = D_RNN // N_HEADS
RG_C = 8.0
GRID_W = 64
HY_BANDS = 16
HY_FH = 64
D_FF = 2816
N_MOD = 6
EPS = 1e-6

SUBLANES = 8
LANES = 128
VMEM_LIMIT = 56 * 1024 * 1024

DFT2 = LANES
FFT_PASSES = 2


def _cparams(sem):
    return pltpu.CompilerParams(dimension_semantics=sem, vmem_limit_bytes=VMEM_LIMIT)


def _rms(x):
    return x * lax.rsqrt(jnp.mean(x * x, axis=-1, keepdims=True) + EPS)


def _mod_kernel(c_ref, w_ref, b_ref, o_ref):
    c = c_ref[...]
    s = c * jax.nn.sigmoid(c)
    o_ref[0] = jnp.dot(s, w_ref[0], precision=lax.Precision.HIGHEST, preferred_element_type=F32) + b_ref[0]


def _modulation(cc, w_ada, b_ada):
    depth, _, n = w_ada.shape
    tn = 1536
    return pl.pallas_call(
        _mod_kernel,
        grid=(depth, n // tn),
        in_specs=[pl.BlockSpec((SUBLANES, D_MODEL), lambda l, j: (0, 0)),
                  pl.BlockSpec((1, D_MODEL, tn), lambda l, j: (l, 0, j)),
                  pl.BlockSpec((1, 1, tn), lambda l, j: (l, 0, j))],
        out_specs=pl.BlockSpec((1, SUBLANES, tn), lambda l, j: (l, 0, j)),
        out_shape=jax.ShapeDtypeStruct((depth, SUBLANES, n), F32),
        compiler_params=_cparams(("parallel", "parallel")),
        name="adaln_mod",
    )(cc, w_ada, b_ada.reshape(depth, 1, n))


RG_HALF = D_RNN // 2
RG_TILES = RG_HALF // LANES


def _rglru_kernel(*refs, seq, tc, nbk):
    for bb in range(nbk):
        _rglru_sequence(bb, *refs, seq=seq, tc=tc)


def _rglru_sequence(bb, xr_ref, yr_ref, cw_ref, cb_ref, wg_ref, bg_ref, ap_ref, h0_ref, o_ref, st_ref,
                    ext, xc_s, hf, hb, a_s, b_s, *, seq, tc):
    nchunks = seq // tc
    nblk = tc // SUBLANES
    ext[0:SUBLANES, :] = jnp.zeros((SUBLANES, RG_HALF), F32)
    ext[SUBLANES:SUBLANES + seq, :] = xr_ref[bb]
    ext[SUBLANES + seq:2 * SUBLANES + seq, :] = jnp.zeros((SUBLANES, RG_HALF), F32)
    for c in range(nchunks):
        xc = cb_ref[...] + cw_ref[0:1, :] * ext[pl.ds(c * tc + SUBLANES - 2, tc), :]
        for k in range(1, 4):
            xc = xc + cw_ref[k:k + 1, :] * ext[pl.ds(c * tc + SUBLANES - 2 + k, tc), :]
        xc_s[pl.ds(c * tc, tc), :] = xc

    row = lax.broadcasted_iota(jnp.int32, (SUBLANES, LANES), 0)
    half_neg_c_sp = [(-0.5 * RG_C) * jax.nn.softplus(-ap_ref[d, 0]) for d in range(2)]

    def gates(c0, d):
        xc = xc_s[pl.ds(c0, tc), :]
        g = jnp.dot(xc.astype(BF16), wg_ref[d, 0], preferred_element_type=F32) + bg_ref[d, 0]
        i = 0.5 * jnp.tanh(g[:, RG_HALF:]) + 0.5
        log_a = half_neg_c_sp[d] * jnp.tanh(g[:, :RG_HALF]) + half_neg_c_sp[d]
        a = jnp.exp(log_a)
        y = jnp.tanh(-log_a) * (a * a + 1.0)
        mult = jnp.where(y > 0.0, y * lax.rsqrt(y), 0.0)
        a_s[d] = a
        b_s[d] = xc * i * mult

    def local_scan(a, b, reverse):
        for s in (1, 2, 4):
            if reverse:
                keep = row < SUBLANES - s
                shift = SUBLANES - s
            else:
                keep = row >= s
                shift = s
            a_sh = jnp.where(keep, pltpu.roll(a, shift, axis=0), 1.0)
            b_sh = jnp.where(keep, pltpu.roll(b, shift, axis=0), 0.0)
            b = a * b_sh + b
            a = a * a_sh
        return a, b

    def block_body(cf0, cb0):
        def body(j, carry):
            rf = pl.multiple_of(j * SUBLANES, SUBLANES)
            rb = pl.multiple_of(tc - SUBLANES - j * SUBLANES, SUBLANES)
            out = []
            for t in range(RG_TILES):
                lanes = slice(t * LANES, (t + 1) * LANES)
                pa, pb = local_scan(a_s[0, pl.ds(rf, SUBLANES), lanes], b_s[0, pl.ds(rf, SUBLANES), lanes], False)
                h = pa * carry[2 * t] + pb
                hf[pl.ds(cf0 + rf, SUBLANES), lanes] = h
                out.append(jnp.broadcast_to(h[SUBLANES - 1:SUBLANES, :], (SUBLANES, LANES)))
                pa, pb = local_scan(a_s[1, pl.ds(rb, SUBLANES), lanes], b_s[1, pl.ds(rb, SUBLANES), lanes], True)
                h = pa * carry[2 * t + 1] + pb
                hb[pl.ds(cb0 + rb, SUBLANES), lanes] = h
                out.append(jnp.broadcast_to(h[0:1, :], (SUBLANES, LANES)))
            return tuple(out)
        return body

    carry = []
    for t in range(RG_TILES):
        carry.append(jnp.broadcast_to(h0_ref[bb, 0:1, t * LANES:(t + 1) * LANES], (SUBLANES, LANES)))
        carry.append(jnp.broadcast_to(h0_ref[bb, 1:2, t * LANES:(t + 1) * LANES], (SUBLANES, LANES)))
    carry = tuple(carry)
    for c in range(nchunks):
        cf0 = c * tc
        cb0 = (nchunks - 1 - c) * tc
        gates(cf0, 0)
        gates(cb0, 1)
        carry = lax.fori_loop(0, nblk, block_body(cf0, cb0), carry, unroll=2)
    for t in range(RG_TILES):
        st_ref[bb, 0:1, t * LANES:(t + 1) * LANES] = carry[2 * t][0:1, :]
        st_ref[bb, 1:2, t * LANES:(t + 1) * LANES] = carry[2 * t + 1][0:1, :]
    for c in range(nchunks):
        rows = pl.ds(c * tc, tc)
        o_ref[bb, rows, :] = (hf[rows, :] + hb[rows, :]) * jax.nn.gelu(yr_ref[bb, rows, :])


def _rglru(xy, cw, cb, wg, bg, ap, h0):
    b, seq, _ = xy.shape
    tc = min(512, seq)
    nh = D_RNN // RG_HALF
    nbk = max(1, min(b, 1024 // seq))
    assert b % nbk == 0
    kern = functools.partial(_rglru_kernel, seq=seq, tc=tc, nbk=nbk)
    return pl.pallas_call(
        kern,
        grid=(b // nbk, nh),
        in_specs=[pl.BlockSpec((nbk, seq, RG_HALF), lambda i, h: (i, 0, h)),
                  pl.BlockSpec((nbk, seq, RG_HALF), lambda i, h: (i, 0, nh + h)),
                  pl.BlockSpec((4, RG_HALF), lambda i, h: (0, h)),
                  pl.BlockSpec((1, RG_HALF), lambda i, h: (0, h)),
                  pl.BlockSpec((2, 1, RG_HALF, 2 * RG_HALF), lambda i, h: (0, h, 0, 0)),
                  pl.BlockSpec((2, 1, 1, 2 * RG_HALF), lambda i, h: (0, h, 0, 0)),
                  pl.BlockSpec((2, 1, 1, RG_HALF), lambda i, h: (0, h, 0, 0)),
                  pl.BlockSpec((nbk, 2, RG_HALF), lambda i, h: (i, 0, h))],
        out_specs=[pl.BlockSpec((nbk, seq, RG_HALF), lambda i, h: (i, 0, h)),
                   pl.BlockSpec((nbk, 2, RG_HALF), lambda i, h: (i, 0, h))],
        out_shape=[jax.ShapeDtypeStruct((b, seq, D_RNN), F32),
                   jax.ShapeDtypeStruct((b, 2, D_RNN), F32)],
        scratch_shapes=[pltpu.VMEM((seq + 2 * SUBLANES, RG_HALF), F32),
                        pltpu.VMEM((seq, RG_HALF), F32),
                        pltpu.VMEM((seq, RG_HALF), F32),
                        pltpu.VMEM((seq, RG_HALF), F32),
                        pltpu.VMEM((2, tc, RG_HALF), F32),
                        pltpu.VMEM((2, tc, RG_HALF), F32)],
        compiler_params=_cparams(("parallel", "parallel")),
        name="rglru",
    )(xy, xy, cw, cb, wg, bg, ap, h0)


def _gate_weights(gate_w, gate_b, a_param):
    nh = D_RNN // RG_HALF
    hp = N_HEADS // nh
    w = gate_w.reshape(2, 2, nh, hp, HEAD_DIM, HEAD_DIM)
    eye = jnp.eye(hp, dtype=F32)
    dense = jnp.einsum('dghpio,pq->dhpigqo', w, eye)
    dense = (0.5 * dense).reshape(2, nh, RG_HALF, 2 * RG_HALF).astype(BF16)
    bias = 0.5 * gate_b.reshape(2, 2, nh, RG_HALF).transpose(0, 2, 1, 3).reshape(2, nh, 1, 2 * RG_HALF)
    ap = a_param.reshape(2, nh, 1, RG_HALF)
    return dense, bias, ap


def _add(a, b):
    if a is None:
        return b
    if b is None:
        return a
    return a + b


def _sub(a, b):
    if b is None:
        return a
    if a is None:
        return -b
    return a - b


def _scale(a, s):
    if a is None or s == 0.0:
        return None
    if s == 1.0:
        return a
    if s == -1.0:
        return -a
    return a * s


def _cmul_const(z, wr, wi):
    if z is None:
        return None
    re, im = z
    if abs(wr) < 1e-15:
        wr = 0.0
    if abs(wi) < 1e-15:
        wi = 0.0
    if wr != 0.0 and abs(abs(wr) - abs(wi)) < 1e-15 and re is not None and im is not None:
        sr, si = math.copysign(1.0, wr), math.copysign(1.0, wi)
        return (_scale(_sub(_scale(re, sr), _scale(im, si)), abs(wr)), _scale(_add(_scale(re, si), _scale(im, sr)), abs(wr)))
    return (_sub(_scale(re, wr), _scale(im, wi)), _add(_scale(re, wi), _scale(im, wr)))


def _cadd(a, b):
    if a is None:
        return b
    if b is None:
        return a
    return (_add(a[0], b[0]), _add(a[1], b[1]))


def _csub(a, b):
    if b is None:
        return a
    if a is None:
        return (_sub(None, b[0]), _sub(None, b[1]))
    return (_sub(a[0], b[0]), _sub(a[1], b[1]))


def _cfft(xs, sign):
    n = len(xs)
    if n == 1:
        return list(xs)
    even = _cfft(xs[0::2], sign)
    odd = _cfft(xs[1::2], sign)
    out = [None] * n
    for k in range(n // 2):
        ang = sign * 2.0 * math.pi * k / n
        t = _cmul_const(odd[k], math.cos(ang), math.sin(ang))
        out[k] = _cadd(even[k], t)
        out[k + n // 2] = _csub(even[k], t)
    return out


def _fft64(get, put, sign, t_ref, rc, nin, nout):
    r8 = 8
    for p in range(r8):
        if sign < 0:
            xs = [get(r8 * a + p) if r8 * a + p < nin else None for a in range(r8)]
        else:
            xs = [get(p + r8 * d) for d in range(r8)]
        ts = _cfft(xs, sign)
        for q in range(r8):
            ang = sign * 2.0 * math.pi * p * q / 64.0
            re, im = _cmul_const(ts[q], math.cos(ang), math.sin(ang))
            rows = pl.ds((p * r8 + q) * rc, rc)
            t_ref[rows, 0:LANES] = re if re is not None else jnp.zeros_like(im)
            t_ref[rows, LANES:2 * LANES] = im if im is not None else jnp.zeros_like(re)
    for q in range(r8):
        ys = [(t_ref[pl.ds((p * r8 + q) * rc, rc), 0:LANES], t_ref[pl.ds((p * r8 + q) * rc, rc), LANES:2 * LANES])
              for p in range(r8)]
        zs = _cfft(ys, sign)
        for m in range(r8):
            k = q + r8 * m if sign < 0 else r8 * m + q
            if k < nout:
                put(k, zs[m])


def _dft_tables(n1, rc):
    n = n1 * DFT2
    k = np.arange(DFT2, dtype=np.float64)
    ang = 2.0 * np.pi * np.outer(k, k) / DFT2
    c, s = np.cos(ang), np.sin(ang)
    fwd = np.block([[c, -s], [s, c]])
    inv = np.block([[c, s], [-s, c]]) / n

    def split(m):
        hi = m.astype(np.float32).astype(ml_dtypes.bfloat16)
        lo = (m - hi.astype(np.float64)).astype(ml_dtypes.bfloat16)
        return jnp.asarray(hi), jnp.asarray(lo)

    tw_ang = 2.0 * np.pi * np.outer(np.arange(n1, dtype=np.float64), k) / n
    tw = np.concatenate([np.cos(tw_ang), -np.sin(tw_ang)], axis=1)
    tw = np.repeat(tw, rc, axis=0).astype(np.float32)
    return split(fwd) + split(inv) + (jnp.asarray(tw),)


HY_CB = 32


def _fft_rows(n1, cb):
    return SUBLANES if n1 >= 16 else cb


def _dft_mm(x, hi_ref, lo_ref):
    xh = x.astype(BF16)
    acc = jnp.dot(xh, hi_ref[...], preferred_element_type=F32)
    if FFT_PASSES >= 2:
        acc = acc + jnp.dot(xh, lo_ref[...], preferred_element_type=F32)
    if FFT_PASSES >= 3:
        xl = (x - xh.astype(F32)).astype(BF16)
        acc = acc + jnp.dot(xl, hi_ref[...], preferred_element_type=F32)
    return acc


def _filter_hidden(feat_ref, w1_ref, b1_ref, w2_ref, b2_ref, fr_ref, h2h_s, h2l_s):
    hi = lax.Precision.HIGHEST

    @pl.when(pl.program_id(0) == 0)
    def _():
        h = jnp.dot(w1_ref[...], feat_ref[...], precision=hi, preferred_element_type=F32) + b1_ref[...]
        h = jnp.sin(fr_ref[:, 0:1] * h)
        h = jnp.dot(w2_ref[...], h, precision=hi, preferred_element_type=F32) + b2_ref[...]
        h2 = jnp.sin(fr_ref[:, 1:2] * h)
        h2h = h2.astype(BF16)
        h2h_s[...] = h2h
        h2l_s[...] = (h2 - h2h.astype(F32)).astype(BF16)


def _filter_block(w3_ref, fh_ref, fl_ref, tw_ref, kf_ref, h2h_s, h2l_s, k_s, s_s, t_s, *, seq, n1, cb, rc):
    n = 2 * seq
    lane = lax.broadcasted_iota(jnp.int32, (cb, n), 1)
    pos = jnp.where(lane < seq, lane, n - lane).astype(F32)
    t = pos / float(max(seq - 1, 1))
    d_idx = (lax.broadcasted_iota(jnp.int32, (cb, n), 0) + pl.program_id(0) * cb).astype(F32)
    min_decay = math.log(1e-2) / 1.5
    max_decay = math.log(1e-2) / 0.3
    delta = jnp.abs(min_decay + d_idx * ((max_decay - min_decay) / (D_HY - 1)))
    decay = jnp.exp(-t * delta)
    w3 = w3_ref[...].reshape(4 * cb, HY_FH)
    w3h = w3.astype(BF16)
    w3l = (w3 - w3h.astype(F32)).astype(BF16)
    hfb = (jnp.dot(w3h, h2h_s[...], preferred_element_type=F32) + jnp.dot(w3h, h2l_s[...], preferred_element_type=F32)
           + jnp.dot(w3l, h2h_s[...], preferred_element_type=F32))
    for o in range(2):
        hf = hfb[(2 * o) * cb:(2 * o + 1) * cb]
        hb = hfb[(2 * o + 1) * cb:(2 * o + 2) * cb]
        k_s[o] = jnp.where(lane < seq, hf, jnp.where(lane == seq, 0.0, hb)) * decay

    nrc = cb // rc
    for o in range(2):
        for i in range(nrc):
            r0 = i * rc
            base = i * (n1 * rc)

            def get(j, o=o, r0=r0):
                return (k_s[o, pl.ds(r0, rc), j * LANES:(j + 1) * LANES], None)

            def put(k1, z, base=base):
                re, im = z
                if k1 > 0:
                    twr = tw_ref[k1 * rc:(k1 + 1) * rc, 0:LANES]
                    twi = tw_ref[k1 * rc:(k1 + 1) * rc, LANES:2 * LANES]
                    re, im = (re * twr, re * twi) if im is None else (re * twr - im * twi, re * twi + im * twr)
                s_s[pl.ds(base + k1 * rc, rc), 0:LANES] = re
                s_s[pl.ds(base + k1 * rc, rc), LANES:2 * LANES] = im if im is not None else jnp.zeros_like(re)

            if n1 == 64:
                _fft64(get, put, -1, t_s, rc, n1, n1)
            else:
                zs = _cfft([get(j) for j in range(n1)], -1)
                for k1 in range(n1):
                    put(k1, zs[k1])
        x = s_s[...]
        xh = x.astype(BF16)
        xl = (x - xh.astype(F32)).astype(BF16)
        z = (jnp.dot(xh, fh_ref[...], preferred_element_type=F32) + jnp.dot(xh, fl_ref[...], preferred_element_type=F32)
             + jnp.dot(xl, fh_ref[...], preferred_element_type=F32))
        for i in range(nrc):
            kf_ref[o, i] = z[i * n1 * rc:(i + 1) * n1 * rc, :]


FRONT_TM = 512


def _front_kernel(x_ref, mod_ref, g_ref, wxy_ref, whyt_ref, feat_ref, w1_ref, b1_ref, w2_ref, b2_ref, fr_ref, w3_ref,
                  fh_ref, fl_ref, tw_ref, xy_ref, hyt_ref, kf_ref, h2h_s, h2l_s, k_s, s_s, t_s, *, seq, n1, cb, rc):
    _filter_hidden(feat_ref, w1_ref, b1_ref, w2_ref, b2_ref, fr_ref, h2h_s, h2l_s)
    for q in range(x_ref.shape[0]):
        xn = _rms(x_ref[q]) * g_ref[...]
        xn = (xn * (1.0 + mod_ref[0, 1:2, :]) + mod_ref[0, 0:1, :]).astype(BF16)
        xy_ref[q] = jnp.dot(xn, wxy_ref[...], preferred_element_type=F32)
        hyt_ref[q] = lax.dot_general(whyt_ref[...], xn, (((1,), (1,)), ((), ())), preferred_element_type=F32)
    _filter_block(w3_ref, fh_ref, fl_ref, tw_ref, kf_ref, h2h_s, h2l_s, k_s, s_s, t_s, seq=seq, n1=n1, cb=cb, rc=rc)


def _front(x, mod, g, wxy, whyt, w1, b1, w2, b2, w3, freq, tables):
    b, seq, _ = x.shape
    n1 = 2 * seq // DFT2
    n = 2 * seq
    rc = _fft_rows(n1, HY_CB)
    tmb = min(FRONT_TM, seq)
    nbt = FRONT_TM // tmb
    steps = b * seq // FRONT_TM
    cb = D_HY // steps
    assert cb % rc == 0 and cb * steps == D_HY
    tps = seq // tmb
    fh, fl, _, _, tw = tables
    pos = np.arange(n, dtype=np.float64)
    pos = np.where(pos < seq, pos, n - pos)
    tt = pos / max(seq - 1, 1)
    omega = 2.0 * math.pi * pos / seq
    bands = np.linspace(1e-4, HY_BANDS - 1, HY_BANDS)
    ang = omega[None, :] * bands[:, None]
    feats = np.concatenate([tt[None, :], np.cos(ang), np.sin(ang)], axis=0)
    nfeat = 5 * SUBLANES
    feats = np.pad(feats, ((0, nfeat - feats.shape[0]), (0, 0))).astype(np.float32)
    w1 = jnp.pad(w1, ((0, nfeat - w1.shape[0]), (0, 0)))
    w3t = w3.T.reshape(4, D_HY, HY_FH)
    kern = functools.partial(_front_kernel, seq=seq, n1=n1, cb=cb, rc=rc)
    const = lambda *shape: pl.BlockSpec(shape, lambda i: (0,) * len(shape), pipeline_mode=pl.Buffered(1))
    mod_map = (lambda i: (i // tps, 0, 0)) if mod.shape[0] > 1 else (lambda i: (0, 0, 0))
    xy, hyt, kf = pl.pallas_call(
        kern,
        grid=(steps,),
        in_specs=[pl.BlockSpec((nbt, tmb, D_MODEL), lambda i: (i, 0, 0)),
                  pl.BlockSpec((1, N_MOD, D_MODEL), mod_map),
                  const(1, D_MODEL), const(D_MODEL, 2 * D_RNN), const(3 * D_HY, D_MODEL),
                  const(nfeat, n), const(HY_FH, nfeat), const(HY_FH, 1), const(HY_FH, HY_FH), const(HY_FH, 1),
                  const(HY_FH, 2),
                  pl.BlockSpec((4, cb, HY_FH), lambda i: (0, i, 0)),
                  const(2 * DFT2, 2 * DFT2), const(2 * DFT2, 2 * DFT2), const(n1 * rc, 2 * DFT2)],
        out_specs=[pl.BlockSpec((nbt, tmb, 2 * D_RNN), lambda i: (i, 0, 0)),
                   pl.BlockSpec((nbt, 3 * D_HY, tmb), lambda i: (i // tps, 0, i % tps)),
                   pl.BlockSpec((2, cb // rc, n1 * rc, 2 * DFT2), lambda i: (0, i, 0, 0))],
        out_shape=[jax.ShapeDtypeStruct((b * seq // tmb, tmb, 2 * D_RNN), F32),
                   jax.ShapeDtypeStruct((b, 3 * D_HY, seq), F32),
                   jax.ShapeDtypeStruct((2, D_HY // rc, n1 * rc, 2 * DFT2), F32)],
        scratch_shapes=[pltpu.VMEM((HY_FH, n), BF16), pltpu.VMEM((HY_FH, n), BF16), pltpu.VMEM((2, cb, n), F32),
                        pltpu.VMEM((n1 * cb, 2 * DFT2), F32), pltpu.VMEM((n1 * rc, 2 * DFT2), F32)],
        compiler_params=_cparams(("arbitrary",)),
        name="front",
    )(x.reshape(b * seq // tmb, tmb, D_MODEL), mod, g, wxy, whyt,
      jnp.asarray(feats), w1.T, b1.reshape(HY_FH, 1), w2.T, b2.reshape(HY_FH, 1), freq.T, w3t, fh, fl, tw)
    return xy.reshape(b, seq, 2 * D_RNN), hyt, kf


HY_UNIT = 512


def _hyena_kernel(hy_ref, cw_ref, cb_ref, bias_ref, kf_ref, fh_ref, fl_ref, ih_ref, il_ref, tw_ref, o_ref,
                  buf, sa, sb, ta, tc, *, seq, n1, cb, nb, rc, pu):
    npairs = nb // 2
    nin = n1 // 2
    nrc = cb // rc
    upo = (npairs // pu) * nrc
    nitems = 2 * upo
    assert upo >= 3, "an item's second-order stage A must come after its first-order stage C"
    lane = lax.broadcasted_iota(jnp.int32, (rc, seq), 1)

    def conv3(h, part, r0):
        w0 = cw_ref[0, part, pl.ds(r0, rc), :]
        w1 = cw_ref[1, part, pl.ds(r0, rc), :]
        w2 = cw_ref[2, part, pl.ds(r0, rc), :]
        bb = cb_ref[part, pl.ds(r0, rc), :]
        hm = jnp.where(lane == 0, 0.0, pltpu.roll(h, 1, axis=1))
        hp = jnp.where(lane == seq - 1, 0.0, pltpu.roll(h, seq - 1, axis=1))
        return w0 * hm + w1 * h + w2 * hp + bb

    def prep(i, _):
        b = i // nrc
        r0 = pl.multiple_of((i % nrc) * rc, rc)
        buf[0, b, pl.ds(r0, rc), :] = conv3(hy_ref[b, 0, pl.ds(r0, rc), :], 0, r0)
        return 0
    lax.fori_loop(0, nb * nrc, prep, 0, unroll=2)

    def item(j):
        o = j // upo
        u = j % upo
        return o, u // nrc, u % nrc

    def stage_a(j, slot):
        o, pg, rg = item(j)
        r0 = pl.multiple_of(rg * rc, rc)
        for pp in range(pu):
            p = pg * pu + pp
            def get(jb, p=p):
                return (buf[o, 2 * p, pl.ds(r0, rc), jb * LANES:(jb + 1) * LANES],
                        buf[o, 2 * p + 1, pl.ds(r0, rc), jb * LANES:(jb + 1) * LANES])

            def put(k1, z, pp=pp):
                re, im = z
                if k1 > 0:
                    twr = tw_ref[k1 * rc:(k1 + 1) * rc, 0:LANES]
                    twi = tw_ref[k1 * rc:(k1 + 1) * rc, LANES:2 * LANES]
                    re, im = re * twr - im * twi, re * twi + im * twr
                rows = pl.ds((pp * n1 + k1) * rc, rc)
                sa[slot, rows, 0:LANES] = re
                sa[slot, rows, LANES:2 * LANES] = im

            if n1 == 64:
                _fft64(get, put, -1, ta, rc, nin, n1)
            else:
                zs = _cfft([get(jb) if jb < nin else None for jb in range(n1)], -1)
                for k1 in range(n1):
                    put(k1, zs[k1])

    def stage_b(j, slot):
        o, _, rg = item(j)
        z = _dft_mm(sa[slot], fh_ref, fl_ref)
        kk = kf_ref[o, rg]
        if pu > 1:
            kk = jnp.concatenate([kk] * pu, axis=0)
        zr, zi = z[:, :LANES], z[:, LANES:]
        kr, ki = kk[:, :LANES], kk[:, LANES:]
        w = jnp.concatenate([zr * kr - zi * ki, zr * ki + zi * kr], axis=1)
        sb[slot] = _dft_mm(w, ih_ref, il_ref)

    def stage_c(j, slot):
        o, pg, rg = item(j)
        r0 = pl.multiple_of(rg * rc, rc)
        bias = bias_ref[o, pl.ds(r0, rc), :]
        for pp in range(pu):
            p = pg * pu + pp
            xg = [conv3(hy_ref[2 * p + q, 1 + o, pl.ds(r0, rc), :], 1 + o, r0) for q in range(2)]

            def get(k1, pp=pp):
                re = sb[slot, pl.ds((pp * n1 + k1) * rc, rc), 0:LANES]
                im = sb[slot, pl.ds((pp * n1 + k1) * rc, rc), LANES:2 * LANES]
                if k1 > 0:
                    twr = tw_ref[k1 * rc:(k1 + 1) * rc, 0:LANES]
                    twi = tw_ref[k1 * rc:(k1 + 1) * rc, LANES:2 * LANES]
                    re, im = re * twr + im * twi, im * twr - re * twi
                return (re, im)

            def put(jb, y, p=p, xg=xg):
                lanes = slice(jb * LANES, (jb + 1) * LANES)
                for q in range(2):
                    u = buf[o, 2 * p + q, pl.ds(r0, rc), lanes]
                    buf[o + 1, 2 * p + q, pl.ds(r0, rc), lanes] = xg[q][:, lanes] * (y[q] + u * bias)

            if n1 == 64:
                _fft64(get, put, +1, tc, rc, n1, nin)
            else:
                ys = _cfft([get(k1) for k1 in range(n1)], +1)
                for jb in range(nin):
                    put(jb, ys[jb])

    stage_a(0, 0)
    stage_a(1, 1)
    stage_b(0, 0)

    assert nitems % 2 == 0

    def steady(i, _):
        t = 2 + 2 * i
        stage_b(t - 1, 1)
        stage_a(t, 0)
        stage_c(t - 2, 0)
        stage_b(t, 0)
        stage_a(t + 1, 1)
        stage_c(t - 1, 1)
        return 0
    lax.fori_loop(0, (nitems - 2) // 2, steady, 0)
    stage_b(nitems - 1, (nitems - 1) % 2)
    stage_c(nitems - 2, nitems % 2)
    stage_c(nitems - 1, (nitems - 1) % 2)
    for b in range(nb):
        o_ref[b] = buf[2, b]


def _hyena(hyt, cw, cbias, bias, kf, tables):
    nb, _, seq = hyt.shape
    n1 = 2 * seq // DFT2
    cb = HY_CB
    rc = _fft_rows(n1, cb)
    pu = HY_UNIT // (n1 * rc)
    nblk = D_HY // cb
    fh, fl, ih, il, tw = tables
    kern = functools.partial(_hyena_kernel, seq=seq, n1=n1, cb=cb, nb=nb, rc=rc, pu=pu)
    full = lambda *shape: pl.BlockSpec(shape, lambda i: (0,) * len(shape))
    return pl.pallas_call(
        kern,
        grid=(nblk,),
        in_specs=[pl.BlockSpec((nb, 3, cb, seq), lambda i: (0, 0, i, 0)),
                  pl.BlockSpec((3, 3, cb, 1), lambda i: (0, 0, i, 0)),
                  pl.BlockSpec((3, cb, 1), lambda i: (0, i, 0)),
                  pl.BlockSpec((2, cb, 1), lambda i: (0, i, 0)),
                  pl.BlockSpec((2, cb // rc, n1 * rc, 2 * DFT2), lambda i: (0, i, 0, 0)),
                  full(2 * DFT2, 2 * DFT2), full(2 * DFT2, 2 * DFT2),
                  full(2 * DFT2, 2 * DFT2), full(2 * DFT2, 2 * DFT2),
                  full(n1 * rc, 2 * DFT2)],
        out_specs=pl.BlockSpec((nb, cb, seq), lambda i: (0, i, 0)),
        out_shape=jax.ShapeDtypeStruct((nb, D_HY, seq), F32),
        scratch_shapes=[pltpu.VMEM((3, nb, cb, seq), F32),
                        pltpu.VMEM((2, HY_UNIT, 2 * DFT2), F32), pltpu.VMEM((2, HY_UNIT, 2 * DFT2), F32),
                        pltpu.VMEM((n1 * rc, 2 * DFT2), F32), pltpu.VMEM((n1 * rc, 2 * DFT2), F32)],
        compiler_params=_cparams(("parallel",)),
        name="hyena",
    )(hyt.reshape(nb, 3, D_HY, seq), cw, cbias, bias, kf, fh, fl, ih, il, tw)


def _out_proj_kernel(x_ref, or_ref, oht_ref, mod_ref, gr_ref, gh_ref, wr_ref, wh_ref, g2_ref, x1_ref, xn_ref):
    orn = (_rms(or_ref[0]) * gr_ref[...]).astype(BF16)
    oh = oht_ref[0]
    ohn = oh * lax.rsqrt(jnp.mean(oh * oh, axis=0, keepdims=True) + EPS) * gh_ref[...]
    o = jnp.dot(orn, wr_ref[...], preferred_element_type=F32)
    o = o + lax.dot_general(ohn.astype(BF16), wh_ref[...], (((0,), (0,)), ((), ())), preferred_element_type=F32)
    x1 = x_ref[0] + mod_ref[0, 2:3, :] * o
    x1_ref[0] = x1
    xn = _rms(x1) * g2_ref[...]
    xn_ref[0] = (xn * (1.0 + mod_ref[0, 4:5, :]) + mod_ref[0, 3:4, :]).astype(BF16)


def _out_proj(x, o_r, oht, mod, gr, gh, wr, wh, g2, tm):
    b, l, _ = x.shape
    mod_map = (lambda i, j: (i, 0, 0)) if mod.shape[0] > 1 else (lambda i, j: (0, 0, 0))
    return pl.pallas_call(
        _out_proj_kernel,
        grid=(b, l // tm),
        in_specs=[pl.BlockSpec((1, tm, D_MODEL), lambda i, j: (i, j, 0)),
                  pl.BlockSpec((1, tm, D_RNN), lambda i, j: (i, j, 0)),
                  pl.BlockSpec((1, D_HY, tm), lambda i, j: (i, 0, j)),
                  pl.BlockSpec((1, N_MOD, D_MODEL), mod_map),
                  pl.BlockSpec((1, D_RNN), lambda i, j: (0, 0)),
                  pl.BlockSpec((D_HY, 1), lambda i, j: (0, 0)),
                  pl.BlockSpec((D_RNN, D_MODEL), lambda i, j: (0, 0)),
                  pl.BlockSpec((D_HY, D_MODEL), lambda i, j: (0, 0)),
                  pl.BlockSpec((1, D_MODEL), lambda i, j: (0, 0))],
        out_specs=[pl.BlockSpec((1, tm, D_MODEL), lambda i, j: (i, j, 0)),
                   pl.BlockSpec((1, tm, D_MODEL), lambda i, j: (i, j, 0))],
        out_shape=[jax.ShapeDtypeStruct((b, l, D_MODEL), F32),
                   jax.ShapeDtypeStruct((b, l, D_MODEL), BF16)],
        compiler_params=_cparams(("parallel", "parallel")),
        name="out_proj",
    )(x, o_r, oht, mod, gr, gh, wr, wh, g2)


FF_TM = 512
FF_SUB = 2 * LANES


def _ffn_kernel(x1_ref, xn_ref, xp_ref, xq_ref, mod_ref, wu_ref, cw_ref, cb_ref, wd_ref, gf_ref, o_ref, xe_s, h_s,
                *, seg, halo, on_grid, tiles_per_seq, final_norm):
    t = pl.program_id(0)
    tm = x1_ref.shape[1]
    rows = tm + 2 * halo
    if halo:
        first = (t % tiles_per_seq) == 0
        last = (t % tiles_per_seq) == tiles_per_seq - 1
        xe_s[0:halo, :] = jnp.where(first, jnp.zeros_like(xp_ref[0]), xp_ref[0])
        xe_s[halo + tm:rows, :] = jnp.where(last, jnp.zeros_like(xq_ref[0]), xq_ref[0])
    xe_s[halo:halo + tm, :] = xn_ref[0]

    pos = lax.broadcasted_iota(jnp.int32, (rows, 1), 0) % seg
    for c0 in range(0, D_FF, FF_SUB):
        w = min(FF_SUB, D_FF - c0)
        g = jnp.dot(xe_s[...], wu_ref[:, D_FF + c0:D_FF + c0 + w], preferred_element_type=F32)
        a = jnp.dot(xe_s[halo:halo + tm, :], wu_ref[:, c0:c0 + w], preferred_element_type=F32)
        gls = pltpu.roll(jnp.where(pos == seg - 1, 0.0, g), 1, axis=0)
        grs = pltpu.roll(jnp.where(pos == 0, 0.0, g), rows - 1, axis=0)
        acc = cb_ref[:, c0:c0 + w]
        for dr in ((-1, 0, 1) if on_grid else (0,)):
            lo = halo + dr * seg
            acc = (acc + cw_ref[dr + 1, 0:1, c0:c0 + w] * gls[lo:lo + tm]
                   + cw_ref[dr + 1, 1:2, c0:c0 + w] * g[lo:lo + tm]
                   + cw_ref[dr + 1, 2:3, c0:c0 + w] * grs[lo:lo + tm])
        h_s[:, c0:c0 + w] = (jax.nn.gelu(acc) * a).astype(BF16)
    y = jnp.dot(h_s[...], wd_ref[...], preferred_element_type=F32)
    x2 = x1_ref[0] + mod_ref[0, 5:6, :] * y
    if final_norm:
        x2 = _rms(x2) * gf_ref[...]
    o_ref[0] = x2


def _ffn(x1, xn2, mod, wu, cw, cb, wd, gf, on_grid, final_norm):
    b, l, _ = x1.shape
    tm = FF_TM
    if on_grid:
        seg, halo = GRID_W, GRID_W
        tps = l // tm
    else:
        seg, halo = l, 0
        tps = 1
    nt = b * l // tm
    hb = GRID_W
    nhb = tm // hb
    last_hb = b * l // hb - 1
    mod_map = (lambda t: (t // tps, 0, 0)) if mod.shape[0] > 1 else (lambda t: (0, 0, 0))
    kern = functools.partial(_ffn_kernel, seg=seg, halo=halo, on_grid=on_grid, tiles_per_seq=tps, final_norm=final_norm)
    const = lambda *shape: pl.BlockSpec(shape, lambda t: (0,) * len(shape), pipeline_mode=pl.Buffered(1))
    xh = xn2.reshape(b * l // hb, hb, D_MODEL)
    out = pl.pallas_call(
        kern,
        grid=(nt,),
        in_specs=[pl.BlockSpec((1, tm, D_MODEL), lambda t: (t, 0, 0)),
                  pl.BlockSpec((1, tm, D_MODEL), lambda t: (t, 0, 0)),
                  pl.BlockSpec((1, hb, D_MODEL), lambda t: (jnp.maximum(t * nhb - 1, 0), 0, 0)),
                  pl.BlockSpec((1, hb, D_MODEL), lambda t: (jnp.minimum((t + 1) * nhb, last_hb), 0, 0)),
                  pl.BlockSpec((1, N_MOD, D_MODEL), mod_map),
                  const(D_MODEL, 2 * D_FF), const(3, 3, D_FF), const(1, D_FF), const(D_FF, D_MODEL), const(1, D_MODEL)],
        out_specs=pl.BlockSpec((1, tm, D_MODEL), lambda t: (t, 0, 0)),
        out_shape=jax.ShapeDtypeStruct((nt, tm, D_MODEL), F32),
        scratch_shapes=[pltpu.VMEM((tm + 2 * halo, D_MODEL), BF16), pltpu.VMEM((tm, D_FF), BF16)],
        compiler_params=_cparams(("parallel",)),
        name="ffn",
    )(x1.reshape(nt, tm, D_MODEL), xn2.reshape(nt, tm, D_MODEL), xh, xh, mod, wu, cw, cb, wd, gf)
    return out.reshape(b, l, D_MODEL)


def _trunk_layer(x, mod, h0, p, fargs, tables, on_grid, final_norm, g_final):
    b, l, _ = x.shape
    tm = min(512, l)
    xy, hyt, kf = _front(x, mod, p['g_norm1'], p['wxy'], p['whyt'], *fargs, tables)
    o_r, states = _rglru(xy, p['rg_conv_w'], p['rg_conv_b'], p['wg'], p['bg'], p['ap'], h0)
    oht = _hyena(hyt, p['hy_cw'], p['hy_cb'], p['hy_bias'], kf, tables)
    x1, xn2 = _out_proj(x, o_r, oht, mod, p['g_rnn_out'], p['g_hy_out'], p['w_out_r'], p['w_out_h'], p['g_norm2'], tm)
    x2 = _ffn(x1, xn2, mod, p['w_up'], p['ffn_conv_w'], p['ffn_conv_b'], p['w_down'], g_final, on_grid, final_norm)
    return x2, states


def kernel(x_prompt, x_sample, state_rglru, c, c_ctx, w_ada, b_ada, g_norm1, g_norm2, w_in, rg_conv_w, rg_conv_b, rg_gate_w, rg_gate_b, rg_a, hy_conv_w, hy_conv_b, hf_w1, hf_b1, hf_w2, hf_b2, hf_w3, hf_freq, hy_bias, g_rnn_out, g_hy_out, w_out, w_up, ffn_conv_w, ffn_conv_b, w_down, g_final):
    depth = w_in.shape[0]
    nb_ctx, l_ctx, _ = x_prompt.shape
    nb_lat, l_lat, _ = x_sample.shape

    cc = jnp.zeros((SUBLANES, D_MODEL), F32).at[0].set(c_ctx).at[1:1 + nb_lat].set(c)
    mods = _modulation(cc, w_ada, b_ada).reshape(depth, SUBLANES, N_MOD, D_MODEL)

    tab_ctx = _dft_tables(2 * l_ctx // DFT2, _fft_rows(2 * l_ctx // DFT2, HY_CB))
    tab_lat = _dft_tables(2 * l_lat // DFT2, _fft_rows(2 * l_lat // DFT2, HY_CB))
    gf = g_final.reshape(1, D_MODEL)
    zero_h = jnp.zeros((nb_ctx, 2, D_RNN), F32)

    xp, xs = x_prompt, x_sample
    new_states = []
    for l in range(depth):
        wg, bg, ap = _gate_weights(rg_gate_w[l], rg_gate_b[l], rg_a[l])
        p = {
            'g_norm1': g_norm1[l].reshape(1, D_MODEL), 'g_norm2': g_norm2[l].reshape(1, D_MODEL),
            'wxy': w_in[l, :, :2 * D_RNN].astype(BF16), 'whyt': w_in[l, :, 2 * D_RNN:].T.astype(BF16),
            'rg_conv_w': rg_conv_w[l], 'rg_conv_b': rg_conv_b[l].reshape(1, D_RNN),
            'wg': wg, 'bg': bg, 'ap': ap,
            'hy_cw': hy_conv_w[l].reshape(3, 3, D_HY, 1), 'hy_cb': hy_conv_b[l].reshape(3, D_HY, 1),
            'hy_bias': hy_bias[l].reshape(2, D_HY, 1),
            'g_rnn_out': g_rnn_out[l].reshape(1, D_RNN), 'g_hy_out': g_hy_out[l].reshape(D_HY, 1),
            'w_out_r': w_out[l, :D_RNN].astype(BF16), 'w_out_h': w_out[l, D_RNN:].astype(BF16),
            'w_up': w_up[l].astype(BF16), 'ffn_conv_w': ffn_conv_w[l], 'ffn_conv_b': ffn_conv_b[l].reshape(1, D_FF),
            'w_down': w_down[l].astype(BF16),
        }
        fargs = (hf_w1[l], hf_b1[l], hf_w2[l], hf_b2[l], hf_w3[l], hf_freq[l])
        final = l == depth - 1
        xp, st = _trunk_layer(xp, mods[l, 0:1], zero_h, p, fargs, tab_ctx, False, final, gf)
        new_states.append(st)
        xs, _ = _trunk_layer(xs, mods[l, 1:1 + nb_lat], state_rglru[:, l], p, fargs, tab_lat, True, final, gf)
    return (xp, xs, jnp.stack(new_states, axis=1))
```

```python
import functools
import math

import jax
import jax.numpy as jnp
import ml_dtypes
import numpy as np
from jax import lax
from jax.experimental import pallas as pl
from jax.experimental.pallas import tpu as pltpu

F32 = jnp.float32
BF16 = jnp.bfloat16

D_MODEL = 1024
D_RNN = 512
D_HY = 512
N_HEADS = 8
HEAD_DIM = D_RNN // N_HEADS
RG_C = 8.0
GRID_W = 64
HY_BANDS = 16
HY_FH = 64
D_FF = 2816
N_MOD = 6
EPS = 1e-6

SUBLANES = 8
LANES = 128
VMEM_LIMIT = 56 * 1024 * 1024

DFT2 = LANES
FFT_PASSES = 2


def _cparams(sem):
    return pltpu.CompilerParams(dimension_semantics=sem, vmem_limit_bytes=VMEM_LIMIT)


def _rms(x):
    return x * lax.rsqrt(jnp.mean(x * x, axis=-1, keepdims=True) + EPS)


def _mod_kernel(c_ref, w_ref, b_ref, o_ref):
    c = c_ref[...]
    s = c * jax.nn.sigmoid(c)
    o_ref[0] = jnp.dot(s, w_ref[0], precision=lax.Precision.HIGHEST, preferred_element_type=F32) + b_ref[0]


def _modulation(cc, w_ada, b_ada):
    depth, _, n = w_ada.shape
    tn = 1536
    return pl.pallas_call(
        _mod_kernel,
        grid=(depth, n // tn),
        in_specs=[pl.BlockSpec((SUBLANES, D_MODEL), lambda l, j: (0, 0)),
                  pl.BlockSpec((1, D_MODEL, tn), lambda l, j: (l, 0, j)),
                  pl.BlockSpec((1, 1, tn), lambda l, j: (l, 0, j))],
        out_specs=pl.BlockSpec((1, SUBLANES, tn), lambda l, j: (l, 0, j)),
        out_shape=jax.ShapeDtypeStruct((depth, SUBLANES, n), F32),
        compiler_params=_cparams(("parallel", "parallel")),
        name="adaln_mod",
    )(cc, w_ada, b_ada.reshape(depth, 1, n))


RG_HALF = D_RNN // 2
RG_TILES = RG_HALF // LANES


def _rglru_kernel(*refs, seq, tc, nbk):
    for bb in range(nbk):
        _rglru_sequence(bb, *refs, seq=seq, tc=tc)


def _rglru_sequence(bb, xr_ref, yr_ref, cw_ref, cb_ref, wg_ref, bg_ref, ap_ref, h0_ref, o_ref, st_ref,
                    ext, xc_s, hf, hb, a_s, b_s, *, seq, tc):
    nchunks = seq // tc
    nblk = tc // SUBLANES
    ext[0:SUBLANES, :] = jnp.zeros((SUBLANES, RG_HALF), F32)
    ext[SUBLANES:SUBLANES + seq, :] = xr_ref[bb]
    ext[SUBLANES + seq:2 * SUBLANES + seq, :] = jnp.zeros((SUBLANES, RG_HALF), F32)
    for c in range(nchunks):
        xc = cb_ref[...] + cw_ref[0:1, :] * ext[pl.ds(c * tc + SUBLANES - 2, tc), :]
        for k in range(1, 4):
            xc = xc + cw_ref[k:k + 1, :] * ext[pl.ds(c * tc + SUBLANES - 2 + k, tc), :]
        xc_s[pl.ds(c * tc, tc), :] = xc

    row = lax.broadcasted_iota(jnp.int32, (SUBLANES, LANES), 0)
    half_neg_c_sp = [(-0.5 * RG_C) * jax.nn.softplus(-ap_ref[d, 0]) for d in range(2)]

    def gates(c0, d):
        xc = xc_s[pl.ds(c0, tc), :]
        g = jnp.dot(xc.astype(BF16), wg_ref[d, 0], preferred_element_type=F32) + bg_ref[d, 0]
        i = 0.5 * jnp.tanh(g[:, RG_HALF:]) + 0.5
        log_a = half_neg_c_sp[d] * jnp.tanh(g[:, :RG_HALF]) + half_neg_c_sp[d]
        a = jnp.exp(log_a)
        y = jnp.tanh(-log_a) * (a * a + 1.0)
        mult = jnp.where(y > 0.0, y * lax.rsqrt(y), 0.0)
        a_s[d] = a
        b_s[d] = xc * i * mult

    def local_scan(a, b, reverse):
        for s in (1, 2, 4):
            if reverse:
                keep = row < SUBLANES - s
                shift = SUBLANES - s
            else:
                keep = row >= s
                shift = s
            a_sh = jnp.where(keep, pltpu.roll(a, shift, axis=0), 1.0)
            b_sh = jnp.where(keep, pltpu.roll(b, shift, axis=0), 0.0)
            b = a * b_sh + b
            a = a * a_sh
        return a, b

    def block_body(cf0, cb0):
        def body(j, carry):
            rf = pl.multiple_of(j * SUBLANES, SUBLANES)
            rb = pl.multiple_of(tc - SUBLANES - j * SUBLANES, SUBLANES)
            out = []
            for t in range(RG_TILES):
                lanes = slice(t * LANES, (t + 1) * LANES)
                pa, pb = local_scan(a_s[0, pl.ds(rf, SUBLANES), lanes], b_s[0, pl.ds(rf, SUBLANES), lanes], False)
                h = pa * carry[2 * t] + pb
                hf[pl.ds(cf0 + rf, SUBLANES), lanes] = h
                out.append(jnp.broadcast_to(h[SUBLANES - 1:SUBLANES, :], (SUBLANES, LANES)))
                pa, pb = local_scan(a_s[1, pl.ds(rb, SUBLANES), lanes], b_s[1, pl.ds(rb, SUBLANES), lanes], True)
                h = pa * carry[2 * t + 1] + pb
                hb[pl.ds(cb0 + rb, SUBLANES), lanes] = h
                out.append(jnp.broadcast_to(h[0:1, :], (SUBLANES, LANES)))
            return tuple(out)
        return body

    carry = []
    for t in range(RG_TILES):
        carry.append(jnp.broadcast_to(h0_ref[bb, 0:1, t * LANES:(t + 1) * LANES], (SUBLANES, LANES)))
        carry.append(jnp.broadcast_to(h0_ref[bb, 1:2, t * LANES:(t + 1) * LANES], (SUBLANES, LANES)))
    carry = tuple(carry)
    for c in range(nchunks):
        cf0 = c * tc
        cb0 = (nchunks - 1 - c) * tc
        gates(cf0, 0)
        gates(cb0, 1)
        carry = lax.fori_loop(0, nblk, block_body(cf0, cb0), carry, unroll=2)
    for t in range(RG_TILES):
        st_ref[bb, 0:1, t * LANES:(t + 1) * LANES] = carry[2 * t][0:1, :]
        st_ref[bb, 1:2, t * LANES:(t + 1) * LANES] = carry[2 * t + 1][0:1, :]
    for c in range(nchunks):
        rows = pl.ds(c * tc, tc)
        o_ref[bb, rows, :] = (hf[rows, :] + hb[rows, :]) * jax.nn.gelu(yr_ref[bb, rows, :])


def _rglru(xy, cw, cb, wg, bg, ap, h0):
    b, seq, _ = xy.shape
    tc = min(512, seq)
    nh = D_RNN // RG_HALF
    nbk = max(1, min(b, 1024 // seq))
    assert b % nbk == 0
    kern = functools.partial(_rglru_kernel, seq=seq, tc=tc, nbk=nbk)
    return pl.pallas_call(
        kern,
        grid=(b // nbk, nh),
        in_specs=[pl.BlockSpec((nbk, seq, RG_HALF), lambda i, h: (i, 0, h)),
                  pl.BlockSpec((nbk, seq, RG_HALF), lambda i, h: (i, 0, nh + h)),
                  pl.BlockSpec((4, RG_HALF), lambda i, h: (0, h)),
                  pl.BlockSpec((1, RG_HALF), lambda i, h: (0, h)),
                  pl.BlockSpec((2, 1, RG_HALF, 2 * RG_HALF), lambda i, h: (0, h, 0, 0)),
                  pl.BlockSpec((2, 1, 1, 2 * RG_HALF), lambda i, h: (0, h, 0, 0)),
                  pl.BlockSpec((2, 1, 1, RG_HALF), lambda i, h: (0, h, 0, 0)),
                  pl.BlockSpec((nbk, 2, RG_HALF), lambda i, h: (i, 0, h))],
        out_specs=[pl.BlockSpec((nbk, seq, RG_HALF), lambda i, h: (i, 0, h)),
                   pl.BlockSpec((nbk, 2, RG_HALF), lambda i, h: (i, 0, h))],
        out_shape=[jax.ShapeDtypeStruct((b, seq, D_RNN), F32),
                   jax.ShapeDtypeStruct((b, 2, D_RNN), F32)],
        scratch_shapes=[pltpu.VMEM((seq + 2 * SUBLANES, RG_HALF), F32),
                        pltpu.VMEM((seq, RG_HALF), F32),
                        pltpu.VMEM((seq, RG_HALF), F32),
                        pltpu.VMEM((seq, RG_HALF), F32),
                        pltpu.VMEM((2, tc, RG_HALF), F32),
                        pltpu.VMEM((2, tc, RG_HALF), F32)],
        compiler_params=_cparams(("parallel", "parallel")),
        name="rglru",
    )(xy, xy, cw, cb, wg, bg, ap, h0)


def _gate_weights(gate_w, gate_b, a_param):
    nh = D_RNN // RG_HALF
    hp = N_HEADS // nh
    w = gate_w.reshape(2, 2, nh, hp, HEAD_DIM, HEAD_DIM)
    eye = jnp.eye(hp, dtype=F32)
    dense = jnp.einsum('dghpio,pq->dhpigqo', w, eye)
    dense = (0.5 * dense).reshape(2, nh, RG_HALF, 2 * RG_HALF).astype(BF16)
    bias = 0.5 * gate_b.reshape(2, 2, nh, RG_HALF).transpose(0, 2, 1, 3).reshape(2, nh, 1, 2 * RG_HALF)
    ap = a_param.reshape(2, nh, 1, RG_HALF)
    return dense, bias, ap


def _add(a, b):
    if a is None:
        return b
    if b is None:
        return a
    return a + b


def _sub(a, b):
    if b is None:
        return a
    if a is None:
        return -b
    return a - b


def _scale(a, s):
    if a is None or s == 0.0:
        return None
    if s == 1.0:
        return a
    if s == -1.0:
        return -a
    return a * s


def _cmul_const(z, wr, wi):
    if z is None:
        return None
    re, im = z
    if abs(wr) < 1e-15:
        wr = 0.0
    if abs(wi) < 1e-15:
        wi = 0.0
    if wr != 0.0 and abs(abs(wr) - abs(wi)) < 1e-15 and re is not None and im is not None:
        sr, si = math.copysign(1.0, wr), math.copysign(1.0, wi)
        return (_scale(_sub(_scale(re, sr), _scale(im, si)), abs(wr)), _scale(_add(_scale(re, si), _scale(im, sr)), abs(wr)))
    return (_sub(_scale(re, wr), _scale(im, wi)), _add(_scale(re, wi), _scale(im, wr)))


def _cadd(a, b):
    if a is None:
        return b
    if b is None:
        return a
    return (_add(a[0], b[0]), _add(a[1], b[1]))


def _csub(a, b):
    if b is None:
        return a
    if a is None:
        return (_sub(None, b[0]), _sub(None, b[1]))
    return (_sub(a[0], b[0]), _sub(a[1], b[1]))


def _cfft(xs, sign):
    n = len(xs)
    if n == 1:
        return list(xs)
    even = _cfft(xs[0::2], sign)
    odd = _cfft(xs[1::2], sign)
    out = [None] * n
    for k in range(n // 2):
        ang = sign * 2.0 * math.pi * k / n
        t = _cmul_const(odd[k], math.cos(ang), math.sin(ang))
        out[k] = _cadd(even[k], t)
        out[k + n // 2] = _csub(even[k], t)
    return out


def _fft64(get, put, sign, t_ref, rc, nin, nout):
    r8 = 8
    for p in range(r8):
        if sign < 0:
            xs = [get(r8 * a + p) if r8 * a + p < nin else None for a in range(r8)]
        else:
            xs = [get(p + r8 * d) for d in range(r8)]
        ts = _cfft(xs, sign)
        for q in range(r8):
            ang = sign * 2.0 * math.pi * p * q / 64.0
            re, im = _cmul_const(ts[q], math.cos(ang), math.sin(ang))
            rows = pl.ds((p * r8 + q) * rc, rc)
            t_ref[rows, 0:LANES] = re if re is not None else jnp.zeros_like(im)
            t_ref[rows, LANES:2 * LANES] = im if im is not None else jnp.zeros_like(re)
    for q in range(r8):
        ys = [(t_ref[pl.ds((p * r8 + q) * rc, rc), 0:LANES], t_ref[pl.ds((p * r8 + q) * rc, rc), LANES:2 * LANES])
              for p in range(r8)]
        zs = _cfft(ys, sign)
        for m in range(r8):
            k = q + r8 * m if sign < 0 else r8 * m + q
            if k < nout:
                put(k, zs[m])


def _dft_tables(n1, rc):
    n = n1 * DFT2
    k = np.arange(DFT2, dtype=np.float64)
    ang = 2.0 * np.pi * np.outer(k, k) / DFT2
    c, s = np.cos(ang), np.sin(ang)
    fwd = np.block([[c, -s], [s, c]])
    inv = np.block([[c, s], [-s, c]]) / n

    def split(m):
        hi = m.astype(np.float32).astype(ml_dtypes.bfloat16)
        lo = (m - hi.astype(np.float64)).astype(ml_dtypes.bfloat16)
        return jnp.asarray(hi), jnp.asarray(lo)

    tw_ang = 2.0 * np.pi * np.outer(np.arange(n1, dtype=np.float64), k) / n
    tw = np.concatenate([np.cos(tw_ang), -np.sin(tw_ang)], axis=1)
    tw = np.repeat(tw, rc, axis=0).astype(np.float32)
    return split(fwd) + split(inv) + (jnp.asarray(tw),)


HY_CB = 32


def _fft_rows(n1, cb):
    return SUBLANES if n1 >= 16 else cb


def _dft_mm(x, hi_ref, lo_ref):
    xh = x.astype(BF16)
    acc = jnp.dot(xh, hi_ref[...], preferred_element_type=F32)
    if FFT_PASSES >= 2:
        acc = acc + jnp.dot(xh, lo_ref[...], preferred_element_type=F32)
    if FFT_PASSES >= 3:
        xl = (x - xh.astype(F32)).astype(BF16)
        acc = acc + jnp.dot(xl, hi_ref[...], preferred_element_type=F32)
    return acc


def _filter_hidden(feat_ref, w1_ref, b1_ref, w2_ref, b2_ref, fr_ref, h2h_s, h2l_s):
    hi = lax.Precision.HIGHEST

    @pl.when(pl.program_id(0) == 0)
    def _():
        h = jnp.dot(w1_ref[...], feat_ref[...], precision=hi, preferred_element_type=F32) + b1_ref[...]
        h = jnp.sin(fr_ref[:, 0:1] * h)
        h = jnp.dot(w2_ref[...], h, precision=hi, preferred_element_type=F32) + b2_ref[...]
        h2 = jnp.sin(fr_ref[:, 1:2] * h)
        h2h = h2.astype(BF16)
        h2h_s[...] = h2h
        h2l_s[...] = (h2 - h2h.astype(F32)).astype(BF16)


def _filter_block(w3_ref, fh_ref, fl_ref, tw_ref, kf_ref, h2h_s, h2l_s, k_s, s_s, t_s, *, seq, n1, cb, rc):
    n = 2 * seq
    lane = lax.broadcasted_iota(jnp.int32, (cb, n), 1)
    pos = jnp.where(lane < seq, lane, n - lane).astype(F32)
    t = pos / float(max(seq - 1, 1))
    d_idx = (lax.broadcasted_iota(jnp.int32, (cb, n), 0) + pl.program_id(0) * cb).astype(F32)
    min_decay = math.log(1e-2) / 1.5
    max_decay = math.log(1e-2) / 0.3
    delta = jnp.abs(min_decay + d_idx * ((max_decay - min_decay) / (D_HY - 1)))
    decay = jnp.exp(-t * delta)
    w3 = w3_ref[...].reshape(4 * cb, HY_FH)
    w3h = w3.astype(BF16)
    w3l = (w3 - w3h.astype(F32)).astype(BF16)
    hfb = (jnp.dot(w3h, h2h_s[...], preferred_element_type=F32) + jnp.dot(w3h, h2l_s[...], preferred_element_type=F32)
           + jnp.dot(w3l, h2h_s[...], preferred_element_type=F32))
    for o in range(2):
        hf = hfb[(2 * o) * cb:(2 * o + 1) * cb]
        hb = hfb[(2 * o + 1) * cb:(2 * o + 2) * cb]
        k_s[o] = jnp.where(lane < seq, hf, jnp.where(lane == seq, 0.0, hb)) * decay

    nrc = cb // rc
    for o in range(2):
        for i in range(nrc):
            r0 = i * rc
            base = i * (n1 * rc)

            def get(j, o=o, r0=r0):
                return (k_s[o, pl.ds(r0, rc), j * LANES:(j + 1) * LANES], None)

            def put(k1, z, base=base):
                re, im = z
                if k1 > 0:
                    twr = tw_ref[k1 * rc:(k1 + 1) * rc, 0:LANES]
                    twi = tw_ref[k1 * rc:(k1 + 1) * rc, LANES:2 * LANES]
                    re, im = (re * twr, re * twi) if im is None else (re * twr - im * twi, re * twi + im * twr)
                s_s[pl.ds(base + k1 * rc, rc), 0:LANES] = re
                s_s[pl.ds(base + k1 * rc, rc), LANES:2 * LANES] = im if im is not None else jnp.zeros_like(re)

            if n1 == 64:
                _fft64(get, put, -1, t_s, rc, n1, n1)
            else:
                zs = _cfft([get(j) for j in range(n1)], -1)
                for k1 in range(n1):
                    put(k1, zs[k1])
        x = s_s[...]
        xh = x.astype(BF16)
        xl = (x - xh.astype(F32)).astype(BF16)
        z = (jnp.dot(xh, fh_ref[...], preferred_element_type=F32) + jnp.dot(xh, fl_ref[...], preferred_element_type=F32)
             + jnp.dot(xl, fh_ref[...], preferred_element_type=F32))
        for i in range(nrc):
            kf_ref[o, i] = z[i * n1 * rc:(i + 1) * n1 * rc, :]


FRONT_TM = 512


def _front_kernel(x_ref, mod_ref, g_ref, wxy_ref, whyt_ref, feat_ref, w1_ref, b1_ref, w2_ref, b2_ref, fr_ref, w3_ref,
                  fh_ref, fl_ref, tw_ref, xy_ref, hyt_ref, kf_ref, h2h_s, h2l_s, k_s, s_s, t_s, *, seq, n1, cb, rc):
    _filter_hidden(feat_ref, w1_ref, b1_ref, w2_ref, b2_ref, fr_ref, h2h_s, h2l_s)
    for q in range(x_ref.shape[0]):
        xn = _rms(x_ref[q]) * g_ref[...]
        xn = (xn * (1.0 + mod_ref[0, 1:2, :]) + mod_ref[0, 0:1, :]).astype(BF16)
        xy_ref[q] = jnp.dot(xn, wxy_ref[...], preferred_element_type=F32)
        hyt_ref[q] = lax.dot_general(whyt_ref[...], xn, (((1,), (1,)), ((), ())), preferred_element_type=F32)
    _filter_block(w3_ref, fh_ref, fl_ref, tw_ref, kf_ref, h2h_s, h2l_s, k_s, s_s, t_s, seq=seq, n1=n1, cb=cb, rc=rc)


def _front(x, mod, g, wxy, whyt, w1, b1, w2, b2, w3, freq, tables):
    b, seq, _ = x.shape
    n1 = 2 * seq // DFT2
    n = 2 * seq
    rc = _fft_rows(n1, HY_CB)
    tmb = min(FRONT_TM, seq)
    nbt = FRONT_TM // tmb
    steps = b * seq // FRONT_TM
    cb = D_HY // steps
    assert cb % rc == 0 and cb * steps == D_HY
    tps = seq // tmb
    fh, fl, _, _, tw = tables
    pos = np.arange(n, dtype=np.float64)
    pos = np.where(pos < seq, pos, n - pos)
    tt = pos / max(seq - 1, 1)
    omega = 2.0 * math.pi * pos / seq
    bands = np.linspace(1e-4, HY_BANDS - 1, HY_BANDS)
    ang = omega[None, :] * bands[:, None]
    feats = np.concatenate([tt[None, :], np.cos(ang), np.sin(ang)], axis=0)
    nfeat = 5 * SUBLANES
    feats = np.pad(feats, ((0, nfeat - feats.shape[0]), (0, 0))).astype(np.float32)
    w1 = jnp.pad(w1, ((0, nfeat - w1.shape[0]), (0, 0)))
    w3t = w3.T.reshape(4, D_HY, HY_FH)
    kern = functools.partial(_front_kernel, seq=seq, n1=n1, cb=cb, rc=rc)
    const = lambda *shape: pl.BlockSpec(shape, lambda i: (0,) * len(shape), pipeline_mode=pl.Buffered(1))
    mod_map = (lambda i: (i // tps, 0, 0)) if mod.shape[0] > 1 else (lambda i: (0, 0, 0))
    xy, hyt, kf = pl.pallas_call(
        kern,
        grid=(steps,),
        in_specs=[pl.BlockSpec((nbt, tmb, D_MODEL), lambda i: (i, 0, 0)),
                  pl.BlockSpec((1, N_MOD, D_MODEL), mod_map),
                  const(1, D_MODEL), const(D_MODEL, 2 * D_RNN), const(3 * D_HY, D_MODEL),
                  const(nfeat, n), const(HY_FH, nfeat), const(HY_FH, 1), const(HY_FH, HY_FH), const(HY_FH, 1),
                  const(HY_FH, 2),
                  pl.BlockSpec((4, cb, HY_FH), lambda i: (0, i, 0)),
                  const(2 * DFT2, 2 * DFT2), const(2 * DFT2, 2 * DFT2), const(n1 * rc, 2 * DFT2)],
        out_specs=[pl.BlockSpec((nbt, tmb, 2 * D_RNN), lambda i: (i, 0, 0)),
                   pl.BlockSpec((nbt, 3 * D_HY, tmb), lambda i: (i // tps, 0, i % tps)),
                   pl.BlockSpec((2, cb // rc, n1 * rc, 2 * DFT2), lambda i: (0, i, 0, 0))],
        out_shape=[jax.ShapeDtypeStruct((b * seq // tmb, tmb, 2 * D_RNN), F32),
                   jax.ShapeDtypeStruct((b, 3 * D_HY, seq), F32),
                   jax.ShapeDtypeStruct((2, D_HY // rc, n1 * rc, 2 * DFT2), F32)],
        scratch_shapes=[pltpu.VMEM((HY_FH, n), BF16), pltpu.VMEM((HY_FH, n), BF16), pltpu.VMEM((2, cb, n), F32),
                        pltpu.VMEM((n1 * cb, 2 * DFT2), F32), pltpu.VMEM((n1 * rc, 2 * DFT2), F32)],
        compiler_params=_cparams(("arbitrary",)),
        name="front",
    )(x.reshape(b * seq // tmb, tmb, D_MODEL), mod, g, wxy, whyt,
      jnp.asarray(feats), w1.T, b1.reshape(HY_FH, 1), w2.T, b2.reshape(HY_FH, 1), freq.T, w3t, fh, fl, tw)
    return xy.reshape(b, seq, 2 * D_RNN), hyt, kf


HY_UNIT = 512


def _hyena_kernel(hy_ref, cw_ref, cb_ref, bias_ref, kf_ref, fh_ref, fl_ref, ih_ref, il_ref, tw_ref, o_ref,
                  buf, sa, sb, ta, tc, *, seq, n1, cb, nb, rc, pu):
    npairs = nb // 2
    nin = n1 // 2
    nrc = cb // rc
    upo = (npairs // pu) * nrc
    nitems = 2 * upo
    assert upo >= 3, "an item's second-order stage A must come after its first-order stage C"
    lane = lax.broadcasted_iota(jnp.int32, (rc, seq), 1)

    def conv3(h, part, r0):
        w0 = cw_ref[0, part, pl.ds(r0, rc), :]
        w1 = cw_ref[1, part, pl.ds(r0, rc), :]
        w2 = cw_ref[2, part, pl.ds(r0, rc), :]
        bb = cb_ref[part, pl.ds(r0, rc), :]
        hm = jnp.where(lane == 0, 0.0, pltpu.roll(h, 1, axis=1))
        hp = jnp.where(lane == seq - 1, 0.0, pltpu.roll(h, seq - 1, axis=1))
        return w0 * hm + w1 * h + w2 * hp + bb

    def prep(i, _):
        b = i // nrc
        r0 = pl.multiple_of((i % nrc) * rc, rc)
        buf[0, b, pl.ds(r0, rc), :] = conv3(hy_ref[b, 0, pl.ds(r0, rc), :], 0, r0)
        return 0
    lax.fori_loop(0, nb * nrc, prep, 0, unroll=2)

    def item(j):
        o = j // upo
        u = j % upo
        return o, u // nrc, u % nrc

    def stage_a(j, slot):
        o, pg, rg = item(j)
        r0 = pl.multiple_of(rg * rc, rc)
        for pp in range(pu):
            p = pg * pu + pp
            def get(jb, p=p):
                return (buf[o, 2 * p, pl.ds(r0, rc), jb * LANES:(jb + 1) * LANES],
                        buf[o, 2 * p + 1, pl.ds(r0, rc), jb * LANES:(jb + 1) * LANES])

            def put(k1, z, pp=pp):
                re, im = z
                if k1 > 0:
                    twr = tw_ref[k1 * rc:(k1 + 1) * rc, 0:LANES]
                    twi = tw_ref[k1 * rc:(k1 + 1) * rc, LANES:2 * LANES]
                    re, im = re * twr - im * twi, re * twi + im * twr
                rows = pl.ds((pp * n1 + k1) * rc, rc)
                sa[slot, rows, 0:LANES] = re
                sa[slot, rows, LANES:2 * LANES] = im

            if n1 == 64:
                _fft64(get, put, -1, ta, rc, nin, n1)
            else:
                zs = _cfft([get(jb) if jb < nin else None for jb in range(n1)], -1)
                for k1 in range(n1):
                    put(k1, zs[k1])

    def stage_b(j, slot):
        o, _, rg = item(j)
        z = _dft_mm(sa[slot], fh_ref, fl_ref)
        kk = kf_ref[o, rg]
        if pu > 1:
            kk = jnp.concatenate([kk] * pu, axis=0)
        zr, zi = z[:, :LANES], z[:, LANES:]
        kr, ki = kk[:, :LANES], kk[:, LANES:]
        w = jnp.concatenate([zr * kr - zi * ki, zr * ki + zi * kr], axis=1)
        sb[slot] = _dft_mm(w, ih_ref, il_ref)

    def stage_c(j, slot):
        o, pg, rg = item(j)
        r0 = pl.multiple_of(rg * rc, rc)
        bias = bias_ref[o, pl.ds(r0, rc), :]
        for pp in range(pu):
            p = pg * pu + pp
            xg = [conv3(hy_ref[2 * p + q, 1 + o, pl.ds(r0, rc), :], 1 + o, r0) for q in range(2)]

            def get(k1, pp=pp):
                re = sb[slot, pl.ds((pp * n1 + k1) * rc, rc), 0:LANES]
                im = sb[slot, pl.ds((pp * n1 + k1) * rc, rc), LANES:2 * LANES]
                if k1 > 0:
                    twr = tw_ref[k1 * rc:(k1 + 1) * rc, 0:LANES]
                    twi = tw_ref[k1 * rc:(k1 + 1) * rc, LANES:2 * LANES]
                    re, im = re * twr + im * twi, im * twr - re * twi
                return (re, im)

            def put(jb, y, p=p, xg=xg):
                lanes = slice(jb * LANES, (jb + 1) * LANES)
                for q in range(2):
                    u = buf[o, 2 * p + q, pl.ds(r0, rc), lanes]
                    buf[o + 1, 2 * p + q, pl.ds(r0, rc), lanes] = xg[q][:, lanes] * (y[q] + u * bias)

            if n1 == 64:
                _fft64(get, put, +1, tc, rc, n1, nin)
            else:
                ys = _cfft([get(k1) for k1 in range(n1)], +1)
                for jb in range(nin):
                    put(jb, ys[jb])

    stage_a(0, 0)
    stage_a(1, 1)
    stage_b(0, 0)

    assert nitems % 2 == 0

    def steady(i, _):
        t = 2 + 2 * i
        stage_b(t - 1, 1)
        stage_a(t, 0)
        stage_c(t - 2, 0)
        stage_b(t, 0)
        stage_a(t + 1, 1)
        stage_c(t - 1, 1)
        return 0
    lax.fori_loop(0, (nitems - 2) // 2, steady, 0)
    stage_b(nitems - 1, (nitems - 1) % 2)
    stage_c(nitems - 2, nitems % 2)
    stage_c(nitems - 1, (nitems - 1) % 2)
    for b in range(nb):
        o_ref[b] = buf[2, b]


def _hyena(hyt, cw, cbias, bias, kf, tables):
    nb, _, seq = hyt.shape
    n1 = 2 * seq // DFT2
    cb = HY_CB
    rc = _fft_rows(n1, cb)
    pu = HY_UNIT // (n1 * rc)
    nblk = D_HY // cb
    fh, fl, ih, il, tw = tables
    kern = functools.partial(_hyena_kernel, seq=seq, n1=n1, cb=cb, nb=nb, rc=rc, pu=pu)
    full = lambda *shape: pl.BlockSpec(shape, lambda i: (0,) * len(shape))
    return pl.pallas_call(
        kern,
        grid=(nblk,),
        in_specs=[pl.BlockSpec((nb, 3, cb, seq), lambda i: (0, 0, i, 0)),
                  pl.BlockSpec((3, 3, cb, 1), lambda i: (0, 0, i, 0)),
                  pl.BlockSpec((3, cb, 1), lambda i: (0, i, 0)),
                  pl.BlockSpec((2, cb, 1), lambda i: (0, i, 0)),
                  pl.BlockSpec((2, cb // rc, n1 * rc, 2 * DFT2), lambda i: (0, i, 0, 0)),
                  full(2 * DFT2, 2 * DFT2), full(2 * DFT2, 2 * DFT2),
                  full(2 * DFT2, 2 * DFT2), full(2 * DFT2, 2 * DFT2),
                  full(n1 * rc, 2 * DFT2)],
        out_specs=pl.BlockSpec((nb, cb, seq), lambda i: (0, i, 0)),
        out_shape=jax.ShapeDtypeStruct((nb, D_HY, seq), F32),
        scratch_shapes=[pltpu.VMEM((3, nb, cb, seq), F32),
                        pltpu.VMEM((2, HY_UNIT, 2 * DFT2), F32), pltpu.VMEM((2, HY_UNIT, 2 * DFT2), F32),
                        pltpu.VMEM((n1 * rc, 2 * DFT2), F32), pltpu.VMEM((n1 * rc, 2 * DFT2), F32)],
        compiler_params=_cparams(("parallel",)),
        name="hyena",
    )(hyt.reshape(nb, 3, D_HY, seq), cw, cbias, bias, kf, fh, fl, ih, il, tw)


def _out_proj_kernel(x_ref, or_ref, oht_ref, mod_ref, gr_ref, gh_ref, wr_ref, wh_ref, g2_ref, x1_ref, xn_ref):
    orn = (_rms(or_ref[0]) * gr_ref[...]).astype(BF16)
    oh = oht_ref[0]
    ohn = oh * lax.rsqrt(jnp.mean(oh * oh, axis=0, keepdims=True) + EPS) * gh_ref[...]
    o = jnp.dot(orn, wr_ref[...], preferred_element_type=F32)
    o = o + lax.dot_general(ohn.astype(BF16), wh_ref[...], (((0,), (0,)), ((), ())), preferred_element_type=F32)
    x1 = x_ref[0] + mod_ref[0, 2:3, :] * o
    x1_ref[0] = x1
    xn = _rms(x1) * g2_ref[...]
    xn_ref[0] = (xn * (1.0 + mod_ref[0, 4:5, :]) + mod_ref[0, 3:4, :]).astype(BF16)


def _out_proj(x, o_r, oht, mod, gr, gh, wr, wh, g2, tm):
    b, l, _ = x.shape
    mod_map = (lambda i, j: (i, 0, 0)) if mod.shape[0] > 1 else (lambda i, j: (0, 0, 0))
    return pl.pallas_call(
        _out_proj_kernel,
        grid=(b, l // tm),
        in_specs=[pl.BlockSpec((1, tm, D_MODEL), lambda i, j: (i, j, 0)),
                  pl.BlockSpec((1, tm, D_RNN), lambda i, j: (i, j, 0)),
                  pl.BlockSpec((1, D_HY, tm), lambda i, j: (i, 0, j)),
                  pl.BlockSpec((1, N_MOD, D_MODEL), mod_map),
                  pl.BlockSpec((1, D_RNN), lambda i, j: (0, 0)),
                  pl.BlockSpec((D_HY, 1), lambda i, j: (0, 0)),
                  pl.BlockSpec((D_RNN, D_MODEL), lambda i, j: (0, 0)),
                  pl.BlockSpec((D_HY, D_MODEL), lambda i, j: (0, 0)),
                  pl.BlockSpec((1, D_MODEL), lambda i, j: (0, 0))],
        out_specs=[pl.BlockSpec((1, tm, D_MODEL), lambda i, j: (i, j, 0)),
                   pl.BlockSpec((1, tm, D_MODEL), lambda i, j: (i, j, 0))],
        out_shape=[jax.ShapeDtypeStruct((b, l, D_MODEL), F32),
                   jax.ShapeDtypeStruct((b, l, D_MODEL), BF16)],
        compiler_params=_cparams(("parallel", "parallel")),
        name="out_proj",
    )(x, o_r, oht, mod, gr, gh, wr, wh, g2)


FF_TM = 512
FF_SUB = 4 * LANES


def _ffn_kernel(x1_ref, xn_ref, xp_ref, xq_ref, mod_ref, wu_ref, cw_ref, cb_ref, wd_ref, gf_ref, o_ref, xe_s, h_s,
                *, seg, halo, on_grid, tiles_per_seq, final_norm):
    t = pl.program_id(0)
    tm = x1_ref.shape[1]
    rows = tm + 2 * halo
    if halo:
        first = (t % tiles_per_seq) == 0
        last = (t % tiles_per_seq) == tiles_per_seq - 1
        xe_s[0:halo, :] = jnp.where(first, jnp.zeros_like(xp_ref[0]), xp_ref[0])
        xe_s[halo + tm:rows, :] = jnp.where(last, jnp.zeros_like(xq_ref[0]), xq_ref[0])
    xe_s[halo:halo + tm, :] = xn_ref[0]

    pos = lax.broadcasted_iota(jnp.int32, (rows, 1), 0) % seg
    for c0 in range(0, D_FF, FF_SUB):
        w = min(FF_SUB, D_FF - c0)
        g = jnp.dot(xe_s[...], wu_ref[:, D_FF + c0:D_FF + c0 + w], preferred_element_type=F32)
        a = jnp.dot(xe_s[halo:halo + tm, :], wu_ref[:, c0:c0 + w], preferred_element_type=F32)
        gls = pltpu.roll(jnp.where(pos == seg - 1, 0.0, g), 1, axis=0)
        grs = pltpu.roll(jnp.where(pos == 0, 0.0, g), rows - 1, axis=0)
        acc = cb_ref[:, c0:c0 + w]
        for dr in ((-1, 0, 1) if on_grid else (0,)):
            lo = halo + dr * seg
            acc = (acc + cw_ref[dr + 1, 0:1, c0:c0 + w] * gls[lo:lo + tm]
                   + cw_ref[dr + 1, 1:2, c0:c0 + w] * g[lo:lo + tm]
                   + cw_ref[dr + 1, 2:3, c0:c0 + w] * grs[lo:lo + tm])
        h_s[:, c0:c0 + w] = (jax.nn.gelu(acc) * a).astype(BF16)
    y = jnp.dot(h_s[...], wd_ref[...], preferred_element_type=F32)
    x2 = x1_ref[0] + mod_ref[0, 5:6, :] * y
    if final_norm:
        x2 = _rms(x2) * gf_ref[...]
    o_ref[0] = x2


def _ffn(x1, xn2, mod, wu, cw, cb, wd, gf, on_grid, final_norm):
    b, l, _ = x1.shape
    tm = FF_TM
    if on_grid:
        seg, halo = GRID_W, GRID_W
        tps = l // tm
    else:
        seg, halo = l, 0
        tps = 1
    nt = b * l // tm
    hb = GRID_W
    nhb = tm // hb
    last_hb = b * l // hb - 1
    mod_map = (lambda t: (t // tps, 0, 0)) if mod.shape[0] > 1 else (lambda t: (0, 0, 0))
    kern = functools.partial(_ffn_kernel, seg=seg, halo=halo, on_grid=on_grid, tiles_per_seq=tps, final_norm=final_norm)
    const = lambda *shape: pl.BlockSpec(shape, lambda t: (0,) * len(shape), pipeline_mode=pl.Buffered(1))
    xh = xn2.reshape(b * l // hb, hb, D_MODEL)
    out = pl.pallas_call(
        kern,
        grid=(nt,),
        in_specs=[pl.BlockSpec((1, tm, D_MODEL), lambda t: (t, 0, 0)),
                  pl.BlockSpec((1, tm, D_MODEL), lambda t: (t, 0, 0)),
                  pl.BlockSpec((1, hb, D_MODEL), lambda t: (jnp.maximum(t * nhb - 1, 0), 0, 0)),
                  pl.BlockSpec((1, hb, D_MODEL), lambda t: (jnp.minimum((t + 1) * nhb, last_hb), 0, 0)),
                  pl.BlockSpec((1, N_MOD, D_MODEL), mod_map),
                  const(D_MODEL, 2 * D_FF), const(3, 3, D_FF), const(1, D_FF), const(D_FF, D_MODEL), const(1, D_MODEL)],
        out_specs=pl.BlockSpec((1, tm, D_MODEL), lambda t: (t, 0, 0)),
        out_shape=jax.ShapeDtypeStruct((nt, tm, D_MODEL), F32),
        scratch_shapes=[pltpu.VMEM((tm + 2 * halo, D_MODEL), BF16), pltpu.VMEM((tm, D_FF), BF16)],
        compiler_params=_cparams(("parallel",)),
        name="ffn",
    )(x1.reshape(nt, tm, D_MODEL), xn2.reshape(nt, tm, D_MODEL), xh, xh, mod, wu, cw, cb, wd, gf)
    return out.reshape(b, l, D_MODEL)


def _trunk_layer(x, mod, h0, p, fargs, tables, on_grid, final_norm, g_final):
    b, l, _ = x.shape
    tm = min(512, l)
    xy, hyt, kf = _front(x, mod, p['g_norm1'], p['wxy'], p['whyt'], *fargs, tables)
    o_r, states = _rglru(xy, p['rg_conv_w'], p['rg_conv_b'], p['wg'], p['bg'], p['ap'], h0)
    oht = _hyena(hyt, p['hy_cw'], p['hy_cb'], p['hy_bias'], kf, tables)
    x1, xn2 = _out_proj(x, o_r, oht, mod, p['g_rnn_out'], p['g_hy_out'], p['w_out_r'], p['w_out_h'], p['g_norm2'], tm)
    x2 = _ffn(x1, xn2, mod, p['w_up'], p['ffn_conv_w'], p['ffn_conv_b'], p['w_down'], g_final, on_grid, final_norm)
    return x2, states


def kernel(x_prompt, x_sample, state_rglru, c, c_ctx, w_ada, b_ada, g_norm1, g_norm2, w_in, rg_conv_w, rg_conv_b, rg_gate_w, rg_gate_b, rg_a, hy_conv_w, hy_conv_b, hf_w1, hf_b1, hf_w2, hf_b2, hf_w3, hf_freq, hy_bias, g_rnn_out, g_hy_out, w_out, w_up, ffn_conv_w, ffn_conv_b, w_down, g_final):
    depth = w_in.shape[0]
    nb_ctx, l_ctx, _ = x_prompt.shape
    nb_lat, l_lat, _ = x_sample.shape

    cc = jnp.zeros((SUBLANES, D_MODEL), F32).at[0].set(c_ctx).at[1:1 + nb_lat].set(c)
    mods = _modulation(cc, w_ada, b_ada).reshape(depth, SUBLANES, N_MOD, D_MODEL)

    tab_ctx = _dft_tables(2 * l_ctx // DFT2, _fft_rows(2 * l_ctx // DFT2, HY_CB))
    tab_lat = _dft_tables(2 * l_lat // DFT2, _fft_rows(2 * l_lat // DFT2, HY_CB))
    gf = g_final.reshape(1, D_MODEL)
    zero_h = jnp.zeros((nb_ctx, 2, D_RNN), F32)

    xp, xs = x_prompt, x_sample
    new_states = []
    for l in range(depth):
        wg, bg, ap = _gate_weights(rg_gate_w[l], rg_gate_b[l], rg_a[l])
        p = {
            'g_norm1': g_norm1[l].reshape(1, D_MODEL), 'g_norm2': g_norm2[l].reshape(1, D_MODEL),
            'wxy': w_in[l, :, :2 * D_RNN].astype(BF16), 'whyt': w_in[l, :, 2 * D_RNN:].T.astype(BF16),
            'rg_conv_w': rg_conv_w[l], 'rg_conv_b': rg_conv_b[l].reshape(1, D_RNN),
            'wg': wg, 'bg': bg, 'ap': ap,
            'hy_cw': hy_conv_w[l].reshape(3, 3, D_HY, 1), 'hy_cb': hy_conv_b[l].reshape(3, D_HY, 1),
            'hy_bias': hy_bias[l].reshape(2, D_HY, 1),
            'g_rnn_out': g_rnn_out[l].reshape(1, D_RNN), 'g_hy_out': g_hy_out[l].reshape(D_HY, 1),
            'w_out_r': w_out[l, :D_RNN].astype(BF16), 'w_out_h': w_out[l, D_RNN:].astype(BF16),
            'w_up': w_up[l].astype(BF16), 'ffn_conv_w': ffn_conv_w[l], 'ffn_conv_b': ffn_conv_b[l].reshape(1, D_FF),
            'w_down': w_down[l].astype(BF16),
        }
        fargs = (hf_w1[l], hf_b1[l], hf_w2[l], hf_b2[l], hf_w3[l], hf_freq[l])
        final = l == depth - 1
        xp, st = _trunk_layer(xp, mods[l, 0:1], zero_h, p, fargs, tab_ctx, False, final, gf)
        new_states.append(st)
        xs, _ = _trunk_layer(xs, mods[l, 1:1 + nb_lat], state_rglru[:, l], p, fargs, tab_lat, True, final, gf)
    return (xp, xs, jnp.stack(new_states, axis=1))
```

```python
import functools
import math

import jax
import jax.numpy as jnp
import ml_dtypes
import numpy as np
from jax import lax
from jax.experimental import pallas as pl
from jax.experimental.pallas import tpu as pltpu

F32 = jnp.float32
BF16 = jnp.bfloat16

D_MODEL = 1024
D_RNN = 512
D_HY = 512
N_HEADS = 8
HEAD_DIM = D_RNN // N_HEADS
RG_C = 8.0
GRID_W = 64
HY_BANDS = 16
HY_FH = 64
D_FF = 2816
N_MOD = 6
EPS = 1e-6

SUBLANES = 8
LANES = 128
VMEM_LIMIT = 56 * 1024 * 1024

DFT2 = LANES
FFT_PASSES = 2


def _cparams(sem):
    return pltpu.CompilerParams(dimension_semantics=sem, vmem_limit_bytes=VMEM_LIMIT)


def _rms(x):
    return x * lax.rsqrt(jnp.mean(x * x, axis=-1, keepdims=True) + EPS)


def _mod_kernel(c_ref, w_ref, b_ref, o_ref):
    c = c_ref[...]
    s = c * jax.nn.sigmoid(c)
    o_ref[0] = jnp.dot(s, w_ref[0], precision=lax.Precision.HIGHEST, preferred_element_type=F32) + b_ref[0]


def _modulation(cc, w_ada, b_ada):
    depth, _, n = w_ada.shape
    tn = 1536
    return pl.pallas_call(
        _mod_kernel,
        grid=(depth, n // tn),
        in_specs=[pl.BlockSpec((SUBLANES, D_MODEL), lambda l, j: (0, 0)),
                  pl.BlockSpec((1, D_MODEL, tn), lambda l, j: (l, 0, j)),
                  pl.BlockSpec((1, 1, tn), lambda l, j: (l, 0, j))],
        out_specs=pl.BlockSpec((1, SUBLANES, tn), lambda l, j: (l, 0, j)),
        out_shape=jax.ShapeDtypeStruct((depth, SUBLANES, n), F32),
        compiler_params=_cparams(("parallel", "parallel")),
        name="adaln_mod",
    )(cc, w_ada, b_ada.reshape(depth, 1, n))


RG_HALF = D_RNN // 2
RG_TILES = RG_HALF // LANES


def _rglru_kernel(*refs, seq, tc, nbk):
    for bb in range(nbk):
        _rglru_sequence(bb, *refs, seq=seq, tc=tc)


def _rglru_sequence(bb, xr_ref, yr_ref, cw_ref, cb_ref, wg_ref, bg_ref, ap_ref, h0_ref, o_ref, st_ref,
                    ext, xc_s, hf, hb, a_s, b_s, *, seq, tc):
    nchunks = seq // tc
    nblk = tc // SUBLANES
    ext[0:SUBLANES, :] = jnp.zeros((SUBLANES, RG_HALF), F32)
    ext[SUBLANES:SUBLANES + seq, :] = xr_ref[bb]
    ext[SUBLANES + seq:2 * SUBLANES + seq, :] = jnp.zeros((SUBLANES, RG_HALF), F32)
    for c in range(nchunks):
        xc = cb_ref[...] + cw_ref[0:1, :] * ext[pl.ds(c * tc + SUBLANES - 2, tc), :]
        for k in range(1, 4):
            xc = xc + cw_ref[k:k + 1, :] * ext[pl.ds(c * tc + SUBLANES - 2 + k, tc), :]
        xc_s[pl.ds(c * tc, tc), :] = xc

    row = lax.broadcasted_iota(jnp.int32, (SUBLANES, LANES), 0)
    half_neg_c_sp = [(-0.5 * RG_C) * jax.nn.softplus(-ap_ref[d, 0]) for d in range(2)]

    def gates(c0, d):
        xc = xc_s[pl.ds(c0, tc), :]
        g = jnp.dot(xc.astype(BF16), wg_ref[d, 0], preferred_element_type=F32) + bg_ref[d, 0]
        i = 0.5 * jnp.tanh(g[:, RG_HALF:]) + 0.5
        log_a = half_neg_c_sp[d] * jnp.tanh(g[:, :RG_HALF]) + half_neg_c_sp[d]
        a = jnp.exp(log_a)
        y = jnp.tanh(-log_a) * (a * a + 1.0)
        mult = jnp.where(y > 0.0, y * lax.rsqrt(y), 0.0)
        a_s[d] = a
        b_s[d] = xc * i * mult

    def local_scan(a, b, reverse):
        for s in (1, 2, 4):
            if reverse:
                keep = row < SUBLANES - s
                shift = SUBLANES - s
            else:
                keep = row >= s
                shift = s
            a_sh = jnp.where(keep, pltpu.roll(a, shift, axis=0), 1.0)
            b_sh = jnp.where(keep, pltpu.roll(b, shift, axis=0), 0.0)
            b = a * b_sh + b
            a = a * a_sh
        return a, b

    def block_body(cf0, cb0):
        def body(j, carry):
            rf = pl.multiple_of(j * SUBLANES, SUBLANES)
            rb = pl.multiple_of(tc - SUBLANES - j * SUBLANES, SUBLANES)
            out = []
            for t in range(RG_TILES):
                lanes = slice(t * LANES, (t + 1) * LANES)
                pa, pb = local_scan(a_s[0, pl.ds(rf, SUBLANES), lanes], b_s[0, pl.ds(rf, SUBLANES), lanes], False)
                h = pa * carry[2 * t] + pb
                hf[pl.ds(cf0 + rf, SUBLANES), lanes] = h
                out.append(jnp.broadcast_to(h[SUBLANES - 1:SUBLANES, :], (SUBLANES, LANES)))
                pa, pb = local_scan(a_s[1, pl.ds(rb, SUBLANES), lanes], b_s[1, pl.ds(rb, SUBLANES), lanes], True)
                h = pa * carry[2 * t + 1] + pb
                hb[pl.ds(cb0 + rb, SUBLANES), lanes] = h
                out.append(jnp.broadcast_to(h[0:1, :], (SUBLANES, LANES)))
            return tuple(out)
        return body

    carry = []
    for t in range(RG_TILES):
        carry.append(jnp.broadcast_to(h0_ref[bb, 0:1, t * LANES:(t + 1) * LANES], (SUBLANES, LANES)))
        carry.append(jnp.broadcast_to(h0_ref[bb, 1:2, t * LANES:(t + 1) * LANES], (SUBLANES, LANES)))
    carry = tuple(carry)
    for c in range(nchunks):
        cf0 = c * tc
        cb0 = (nchunks - 1 - c) * tc
        gates(cf0, 0)
        gates(cb0, 1)
        carry = lax.fori_loop(0, nblk, block_body(cf0, cb0), carry, unroll=2)
    for t in range(RG_TILES):
        st_ref[bb, 0:1, t * LANES:(t + 1) * LANES] = carry[2 * t][0:1, :]
        st_ref[bb, 1:2, t * LANES:(t + 1) * LANES] = carry[2 * t + 1][0:1, :]
    for c in range(nchunks):
        rows = pl.ds(c * tc, tc)
        o_ref[bb, rows, :] = (hf[rows, :] + hb[rows, :]) * jax.nn.gelu(yr_ref[bb, rows, :])


def _rglru(xy, cw, cb, wg, bg, ap, h0):
    b, seq, _ = xy.shape
    tc = min(512, seq)
    nh = D_RNN // RG_HALF
    nbk = max(1, min(b, 1024 // seq))
    assert b % nbk == 0
    kern = functools.partial(_rglru_kernel, seq=seq, tc=tc, nbk=nbk)
    return pl.pallas_call(
        kern,
        grid=(b // nbk, nh),
        in_specs=[pl.BlockSpec((nbk, seq, RG_HALF), lambda i, h: (i, 0, h)),
                  pl.BlockSpec((nbk, seq, RG_HALF), lambda i, h: (i, 0, nh + h)),
                  pl.BlockSpec((4, RG_HALF), lambda i, h: (0, h)),
                  pl.BlockSpec((1, RG_HALF), lambda i, h: (0, h)),
                  pl.BlockSpec((2, 1, RG_HALF, 2 * RG_HALF), lambda i, h: (0, h, 0, 0)),
                  pl.BlockSpec((2, 1, 1, 2 * RG_HALF), lambda i, h: (0, h, 0, 0)),
                  pl.BlockSpec((2, 1, 1, RG_HALF), lambda i, h: (0, h, 0, 0)),
                  pl.BlockSpec((nbk, 2, RG_HALF), lambda i, h: (i, 0, h))],
        out_specs=[pl.BlockSpec((nbk, seq, RG_HALF), lambda i, h: (i, 0, h)),
                   pl.BlockSpec((nbk, 2, RG_HALF), lambda i, h: (i, 0, h))],
        out_shape=[jax.ShapeDtypeStruct((b, seq, D_RNN), F32),
                   jax.ShapeDtypeStruct((b, 2, D_RNN), F32)],
        scratch_shapes=[pltpu.VMEM((seq + 2 * SUBLANES, RG_HALF), F32),
                        pltpu.VMEM((seq, RG_HALF), F32),
                        pltpu.VMEM((seq, RG_HALF), F32),
                        pltpu.VMEM((seq, RG_HALF), F32),
                        pltpu.VMEM((2, tc, RG_HALF), F32),
                        pltpu.VMEM((2, tc, RG_HALF), F32)],
        compiler_params=_cparams(("parallel", "parallel")),
        name="rglru",
    )(xy, xy, cw, cb, wg, bg, ap, h0)


def _gate_weights(gate_w, gate_b, a_param):
    nh = D_RNN // RG_HALF
    hp = N_HEADS // nh
    w = gate_w.reshape(2, 2, nh, hp, HEAD_DIM, HEAD_DIM)
    eye = jnp.eye(hp, dtype=F32)
    dense = jnp.einsum('dghpio,pq->dhpigqo', w, eye)
    dense = (0.5 * dense).reshape(2, nh, RG_HALF, 2 * RG_HALF).astype(BF16)
    bias = 0.5 * gate_b.reshape(2, 2, nh, RG_HALF).transpose(0, 2, 1, 3).reshape(2, nh, 1, 2 * RG_HALF)
    ap = a_param.reshape(2, nh, 1, RG_HALF)
    return dense, bias, ap


def _add(a, b):
    if a is None:
        return b
    if b is None:
        return a
    return a + b


def _sub(a, b):
    if b is None:
        return a
    if a is None:
        return -b
    return a - b


def _scale(a, s):
    if a is None or s == 0.0:
        return None
    if s == 1.0:
        return a
    if s == -1.0:
        return -a
    return a * s


def _cmul_const(z, wr, wi):
    if z is None:
        return None
    re, im = z
    if abs(wr) < 1e-15:
        wr = 0.0
    if abs(wi) < 1e-15:
        wi = 0.0
    if wr != 0.0 and abs(abs(wr) - abs(wi)) < 1e-15 and re is not None and im is not None:
        sr, si = math.copysign(1.0, wr), math.copysign(1.0, wi)
        return (_scale(_sub(_scale(re, sr), _scale(im, si)), abs(wr)), _scale(_add(_scale(re, si), _scale(im, sr)), abs(wr)))
    return (_sub(_scale(re, wr), _scale(im, wi)), _add(_scale(re, wi), _scale(im, wr)))


def _cadd(a, b):
    if a is None:
        return b
    if b is None:
        return a
    return (_add(a[0], b[0]), _add(a[1], b[1]))


def _csub(a, b):
    if b is None:
        return a
    if a is None:
        return (_sub(None, b[0]), _sub(None, b[1]))
    return (_sub(a[0], b[0]), _sub(a[1], b[1]))


def _cfft(xs, sign):
    n = len(xs)
    if n == 1:
        return list(xs)
    even = _cfft(xs[0::2], sign)
    odd = _cfft(xs[1::2], sign)
    out = [None] * n
    for k in range(n // 2):
        ang = sign * 2.0 * math.pi * k / n
        t = _cmul_const(odd[k], math.cos(ang), math.sin(ang))
        out[k] = _cadd(even[k], t)
        out[k + n // 2] = _csub(even[k], t)
    return out


def _fft64(get, put, sign, t_ref, rc, nin, nout):
    r8 = 8
    for p in range(r8):
        if sign < 0:
            xs = [get(r8 * a + p) if r8 * a + p < nin else None for a in range(r8)]
        else:
            xs = [get(p + r8 * d) for d in range(r8)]
        ts = _cfft(xs, sign)
        for q in range(r8):
            ang = sign * 2.0 * math.pi * p * q / 64.0
            re, im = _cmul_const(ts[q], math.cos(ang), math.sin(ang))
            rows = pl.ds((p * r8 + q) * rc, rc)
            t_ref[rows, 0:LANES] = re if re is not None else jnp.zeros_like(im)
            t_ref[rows, LANES:2 * LANES] = im if im is not None else jnp.zeros_like(re)
    for q in range(r8):
        ys = [(t_ref[pl.ds((p * r8 + q) * rc, rc), 0:LANES], t_ref[pl.ds((p * r8 + q) * rc, rc), LANES:2 * LANES])
              for p in range(r8)]
        zs = _cfft(ys, sign)
        for m in range(r8):
            k = q + r8 * m if sign < 0 else r8 * m + q
            if k < nout:
                put(k, zs[m])


def _dft_tables(n1, rc):
    n = n1 * DFT2
    k = np.arange(DFT2, dtype=np.float64)
    ang = 2.0 * np.pi * np.outer(k, k) / DFT2
    c, s = np.cos(ang), np.sin(ang)
    fwd = np.block([[c, -s], [s, c]])
    inv = np.block([[c, s], [-s, c]]) / n

    def split(m):
        hi = m.astype(np.float32).astype(ml_dtypes.bfloat16)
        lo = (m - hi.astype(np.float64)).astype(ml_dtypes.bfloat16)
        return jnp.asarray(hi), jnp.asarray(lo)

    tw_ang = 2.0 * np.pi * np.outer(np.arange(n1, dtype=np.float64), k) / n
    tw = np.concatenate([np.cos(tw_ang), -np.sin(tw_ang)], axis=1)
    tw = np.repeat(tw, rc, axis=0).astype(np.float32)
    return split(fwd) + split(inv) + (jnp.asarray(tw),)


HY_CB = 32


def _fft_rows(n1, cb):
    return SUBLANES if n1 >= 16 else cb


def _dft_mm(x, hi_ref, lo_ref):
    xh = x.astype(BF16)
    acc = jnp.dot(xh, hi_ref[...], preferred_element_type=F32)
    if FFT_PASSES >= 2:
        acc = acc + jnp.dot(xh, lo_ref[...], preferred_element_type=F32)
    if FFT_PASSES >= 3:
        xl = (x - xh.astype(F32)).astype(BF16)
        acc = acc + jnp.dot(xl, hi_ref[...], preferred_element_type=F32)
    return acc


def _filter_hidden(feat_ref, w1_ref, b1_ref, w2_ref, b2_ref, fr_ref, h2h_s, h2l_s):
    hi = lax.Precision.HIGHEST

    @pl.when(pl.program_id(0) == 0)
    def _():
        h = jnp.dot(w1_ref[...], feat_ref[...], precision=hi, preferred_element_type=F32) + b1_ref[...]
        h = jnp.sin(fr_ref[:, 0:1] * h)
        h = jnp.dot(w2_ref[...], h, precision=hi, preferred_element_type=F32) + b2_ref[...]
        h2 = jnp.sin(fr_ref[:, 1:2] * h)
        h2h = h2.astype(BF16)
        h2h_s[...] = h2h
        h2l_s[...] = (h2 - h2h.astype(F32)).astype(BF16)


def _filter_block(w3_ref, fh_ref, fl_ref, tw_ref, kf_ref, h2h_s, h2l_s, k_s, s_s, t_s, *, seq, n1, cb, rc):
    n = 2 * seq
    lane = lax.broadcasted_iota(jnp.int32, (cb, n), 1)
    pos = jnp.where(lane < seq, lane, n - lane).astype(F32)
    t = pos / float(max(seq - 1, 1))
    d_idx = (lax.broadcasted_iota(jnp.int32, (cb, n), 0) + pl.program_id(0) * cb).astype(F32)
    min_decay = math.log(1e-2) / 1.5
    max_decay = math.log(1e-2) / 0.3
    delta = jnp.abs(min_decay + d_idx * ((max_decay - min_decay) / (D_HY - 1)))
    decay = jnp.exp(-t * delta)
    w3 = w3_ref[...].reshape(4 * cb, HY_FH)
    w3h = w3.astype(BF16)
    w3l = (w3 - w3h.astype(F32)).astype(BF16)
    hfb = (jnp.dot(w3h, h2h_s[...], preferred_element_type=F32) + jnp.dot(w3h, h2l_s[...], preferred_element_type=F32)
           + jnp.dot(w3l, h2h_s[...], preferred_element_type=F32))
    for o in range(2):
        hf = hfb[(2 * o) * cb:(2 * o + 1) * cb]
        hb = hfb[(2 * o + 1) * cb:(2 * o + 2) * cb]
        k_s[o] = jnp.where(lane < seq, hf, jnp.where(lane == seq, 0.0, hb)) * decay

    nrc = cb // rc
    for o in range(2):
        for i in range(nrc):
            r0 = i * rc
            base = i * (n1 * rc)

            def get(j, o=o, r0=r0):
                return (k_s[o, pl.ds(r0, rc), j * LANES:(j + 1) * LANES], None)

            def put(k1, z, base=base):
                re, im = z
                if k1 > 0:
                    twr = tw_ref[k1 * rc:(k1 + 1) * rc, 0:LANES]
                    twi = tw_ref[k1 * rc:(k1 + 1) * rc, LANES:2 * LANES]
                    re, im = (re * twr, re * twi) if im is None else (re * twr - im * twi, re * twi + im * twr)
                s_s[pl.ds(base + k1 * rc, rc), 0:LANES] = re
                s_s[pl.ds(base + k1 * rc, rc), LANES:2 * LANES] = im if im is not None else jnp.zeros_like(re)

            if n1 == 64:
                _fft64(get, put, -1, t_s, rc, n1, n1)
            else:
                zs = _cfft([get(j) for j in range(n1)], -1)
                for k1 in range(n1):
                    put(k1, zs[k1])
        z = _dft_mm(s_s[...], fh_ref, fl_ref)
        for i in range(nrc):
            kf_ref[o, i] = z[i * n1 * rc:(i + 1) * n1 * rc, :]


FRONT_TM = 512


def _front_kernel(x_ref, mod_ref, g_ref, wxy_ref, whyt_ref, feat_ref, w1_ref, b1_ref, w2_ref, b2_ref, fr_ref, w3_ref,
                  fh_ref, fl_ref, tw_ref, xy_ref, hyt_ref, kf_ref, h2h_s, h2l_s, k_s, s_s, t_s, *, seq, n1, cb, rc):
    _filter_hidden(feat_ref, w1_ref, b1_ref, w2_ref, b2_ref, fr_ref, h2h_s, h2l_s)
    for q in range(x_ref.shape[0]):
        xn = _rms(x_ref[q]) * g_ref[...]
        xn = (xn * (1.0 + mod_ref[0, 1:2, :]) + mod_ref[0, 0:1, :]).astype(BF16)
        xy_ref[q] = jnp.dot(xn, wxy_ref[...], preferred_element_type=F32)
        hyt_ref[q] = lax.dot_general(whyt_ref[...], xn, (((1,), (1,)), ((), ())), preferred_element_type=F32)
    _filter_block(w3_ref, fh_ref, fl_ref, tw_ref, kf_ref, h2h_s, h2l_s, k_s, s_s, t_s, seq=seq, n1=n1, cb=cb, rc=rc)


def _front(x, mod, g, wxy, whyt, w1, b1, w2, b2, w3, freq, tables):
    b, seq, _ = x.shape
    n1 = 2 * seq // DFT2
    n = 2 * seq
    rc = _fft_rows(n1, HY_CB)
    tmb = min(FRONT_TM, seq)
    nbt = FRONT_TM // tmb
    steps = b * seq // FRONT_TM
    cb = D_HY // steps
    assert cb % rc == 0 and cb * steps == D_HY
    tps = seq // tmb
    fh, fl, _, _, tw = tables
    pos = np.arange(n, dtype=np.float64)
    pos = np.where(pos < seq, pos, n - pos)
    tt = pos / max(seq - 1, 1)
    omega = 2.0 * math.pi * pos / seq
    bands = np.linspace(1e-4, HY_BANDS - 1, HY_BANDS)
    ang = omega[None, :] * bands[:, None]
    feats = np.concatenate([tt[None, :], np.cos(ang), np.sin(ang)], axis=0)
    nfeat = 5 * SUBLANES
    feats = np.pad(feats, ((0, nfeat - feats.shape[0]), (0, 0))).astype(np.float32)
    w1 = jnp.pad(w1, ((0, nfeat - w1.shape[0]), (0, 0)))
    w3t = w3.T.reshape(4, D_HY, HY_FH)
    kern = functools.partial(_front_kernel, seq=seq, n1=n1, cb=cb, rc=rc)
    const = lambda *shape: pl.BlockSpec(shape, lambda i: (0,) * len(shape), pipeline_mode=pl.Buffered(1))
    mod_map = (lambda i: (i // tps, 0, 0)) if mod.shape[0] > 1 else (lambda i: (0, 0, 0))
    xy, hyt, kf = pl.pallas_call(
        kern,
        grid=(steps,),
        in_specs=[pl.BlockSpec((nbt, tmb, D_MODEL), lambda i: (i, 0, 0)),
                  pl.BlockSpec((1, N_MOD, D_MODEL), mod_map),
                  const(1, D_MODEL), const(D_MODEL, 2 * D_RNN), const(3 * D_HY, D_MODEL),
                  const(nfeat, n), const(HY_FH, nfeat), const(HY_FH, 1), const(HY_FH, HY_FH), const(HY_FH, 1),
                  const(HY_FH, 2),
                  pl.BlockSpec((4, cb, HY_FH), lambda i: (0, i, 0)),
                  const(2 * DFT2, 2 * DFT2), const(2 * DFT2, 2 * DFT2), const(n1 * rc, 2 * DFT2)],
        out_specs=[pl.BlockSpec((nbt, tmb, 2 * D_RNN), lambda i: (i, 0, 0)),
                   pl.BlockSpec((nbt, 3 * D_HY, tmb), lambda i: (i // tps, 0, i % tps)),
                   pl.BlockSpec((2, cb // rc, n1 * rc, 2 * DFT2), lambda i: (0, i, 0, 0))],
        out_shape=[jax.ShapeDtypeStruct((b * seq // tmb, tmb, 2 * D_RNN), F32),
                   jax.ShapeDtypeStruct((b, 3 * D_HY, seq), F32),
                   jax.ShapeDtypeStruct((2, D_HY // rc, n1 * rc, 2 * DFT2), F32)],
        scratch_shapes=[pltpu.VMEM((HY_FH, n), BF16), pltpu.VMEM((HY_FH, n), BF16), pltpu.VMEM((2, cb, n), F32),
                        pltpu.VMEM((n1 * cb, 2 * DFT2), F32), pltpu.VMEM((n1 * rc, 2 * DFT2), F32)],
        compiler_params=_cparams(("arbitrary",)),
        name="front",
    )(x.reshape(b * seq // tmb, tmb, D_MODEL), mod, g, wxy, whyt,
      jnp.asarray(feats), w1.T, b1.reshape(HY_FH, 1), w2.T, b2.reshape(HY_FH, 1), freq.T, w3t, fh, fl, tw)
    return xy.reshape(b, seq, 2 * D_RNN), hyt, kf


HY_UNIT = 512


def _hyena_kernel(hy_ref, cw_ref, cb_ref, bias_ref, kf_ref, fh_ref, fl_ref, ih_ref, il_ref, tw_ref, o_ref,
                  buf, sa, sb, ta, tc, *, seq, n1, cb, nb, rc, pu):
    npairs = nb // 2
    nin = n1 // 2
    nrc = cb // rc
    upo = (npairs // pu) * nrc
    nitems = 2 * upo
    assert upo >= 3, "an item's second-order stage A must come after its first-order stage C"
    lane = lax.broadcasted_iota(jnp.int32, (rc, seq), 1)

    def conv3(h, part, r0):
        w0 = cw_ref[0, part, pl.ds(r0, rc), :]
        w1 = cw_ref[1, part, pl.ds(r0, rc), :]
        w2 = cw_ref[2, part, pl.ds(r0, rc), :]
        bb = cb_ref[part, pl.ds(r0, rc), :]
        hm = jnp.where(lane == 0, 0.0, pltpu.roll(h, 1, axis=1))
        hp = jnp.where(lane == seq - 1, 0.0, pltpu.roll(h, seq - 1, axis=1))
        return w0 * hm + w1 * h + w2 * hp + bb

    def prep(i, _):
        b = i // nrc
        r0 = pl.multiple_of((i % nrc) * rc, rc)
        buf[0, b, pl.ds(r0, rc), :] = conv3(hy_ref[b, 0, pl.ds(r0, rc), :], 0, r0)
        return 0
    lax.fori_loop(0, nb * nrc, prep, 0, unroll=2)

    def item(j):
        o = j // upo
        u = j % upo
        return o, u // nrc, u % nrc

    def stage_a(j, slot):
        o, pg, rg = item(j)
        r0 = pl.multiple_of(rg * rc, rc)
        for pp in range(pu):
            p = pg * pu + pp
            def get(jb, p=p):
                return (buf[o, 2 * p, pl.ds(r0, rc), jb * LANES:(jb + 1) * LANES],
                        buf[o, 2 * p + 1, pl.ds(r0, rc), jb * LANES:(jb + 1) * LANES])

            def put(k1, z, pp=pp):
                re, im = z
                if k1 > 0:
                    twr = tw_ref[k1 * rc:(k1 + 1) * rc, 0:LANES]
                    twi = tw_ref[k1 * rc:(k1 + 1) * rc, LANES:2 * LANES]
                    re, im = re * twr - im * twi, re * twi + im * twr
                rows = pl.ds((pp * n1 + k1) * rc, rc)
                sa[slot, rows, 0:LANES] = re
                sa[slot, rows, LANES:2 * LANES] = im

            if n1 == 64:
                _fft64(get, put, -1, ta, rc, nin, n1)
            else:
                zs = _cfft([get(jb) if jb < nin else None for jb in range(n1)], -1)
                for k1 in range(n1):
                    put(k1, zs[k1])

    def stage_b(j, slot):
        o, _, rg = item(j)
        z = _dft_mm(sa[slot], fh_ref, fl_ref)
        kk = kf_ref[o, rg]
        if pu > 1:
            kk = jnp.concatenate([kk] * pu, axis=0)
        zr, zi = z[:, :LANES], z[:, LANES:]
        kr, ki = kk[:, :LANES], kk[:, LANES:]
        w = jnp.concatenate([zr * kr - zi * ki, zr * ki + zi * kr], axis=1)
        sb[slot] = _dft_mm(w, ih_ref, il_ref)

    def stage_c(j, slot):
        o, pg, rg = item(j)
        r0 = pl.multiple_of(rg * rc, rc)
        bias = bias_ref[o, pl.ds(r0, rc), :]
        for pp in range(pu):
            p = pg * pu + pp
            xg = [conv3(hy_ref[2 * p + q, 1 + o, pl.ds(r0, rc), :], 1 + o, r0) for q in range(2)]

            def get(k1, pp=pp):
                re = sb[slot, pl.ds((pp * n1 + k1) * rc, rc), 0:LANES]
                im = sb[slot, pl.ds((pp * n1 + k1) * rc, rc), LANES:2 * LANES]
                if k1 > 0:
                    twr = tw_ref[k1 * rc:(k1 + 1) * rc, 0:LANES]
                    twi = tw_ref[k1 * rc:(k1 + 1) * rc, LANES:2 * LANES]
                    re, im = re * twr + im * twi, im * twr - re * twi
                return (re, im)

            def put(jb, y, p=p, xg=xg):
                lanes = slice(jb * LANES, (jb + 1) * LANES)
                for q in range(2):
                    u = buf[o, 2 * p + q, pl.ds(r0, rc), lanes]
                    buf[o + 1, 2 * p + q, pl.ds(r0, rc), lanes] = xg[q][:, lanes] * (y[q] + u * bias)

            if n1 == 64:
                _fft64(get, put, +1, tc, rc, n1, nin)
            else:
                ys = _cfft([get(k1) for k1 in range(n1)], +1)
                for jb in range(nin):
                    put(jb, ys[jb])

    stage_a(0, 0)
    stage_a(1, 1)
    stage_b(0, 0)

    assert nitems % 2 == 0

    def steady(i, _):
        t = 2 + 2 * i
        stage_b(t - 1, 1)
        stage_a(t, 0)
        stage_c(t - 2, 0)
        stage_b(t, 0)
        stage_a(t + 1, 1)
        stage_c(t - 1, 1)
        return 0
    lax.fori_loop(0, (nitems - 2) // 2, steady, 0)
    stage_b(nitems - 1, (nitems - 1) % 2)
    stage_c(nitems - 2, nitems % 2)
    stage_c(nitems - 1, (nitems - 1) % 2)
    for b in range(nb):
        o_ref[b] = buf[2, b]


def _hyena(hyt, cw, cbias, bias, kf, tables):
    nb, _, seq = hyt.shape
    n1 = 2 * seq // DFT2
    cb = HY_CB
    rc = _fft_rows(n1, cb)
    pu = HY_UNIT // (n1 * rc)
    nblk = D_HY // cb
    fh, fl, ih, il, tw = tables
    kern = functools.partial(_hyena_kernel, seq=seq, n1=n1, cb=cb, nb=nb, rc=rc, pu=pu)
    full = lambda *shape: pl.BlockSpec(shape, lambda i: (0,) * len(shape))
    return pl.pallas_call(
        kern,
        grid=(nblk,),
        in_specs=[pl.BlockSpec((nb, 3, cb, seq), lambda i: (0, 0, i, 0)),
                  pl.BlockSpec((3, 3, cb, 1), lambda i: (0, 0, i, 0)),
                  pl.BlockSpec((3, cb, 1), lambda i: (0, i, 0)),
                  pl.BlockSpec((2, cb, 1), lambda i: (0, i, 0)),
                  pl.BlockSpec((2, cb // rc, n1 * rc, 2 * DFT2), lambda i: (0, i, 0, 0)),
                  full(2 * DFT2, 2 * DFT2), full(2 * DFT2, 2 * DFT2),
                  full(2 * DFT2, 2 * DFT2), full(2 * DFT2, 2 * DFT2),
                  full(n1 * rc, 2 * DFT2)],
        out_specs=pl.BlockSpec((nb, cb, seq), lambda i: (0, i, 0)),
        out_shape=jax.ShapeDtypeStruct((nb, D_HY, seq), F32),
        scratch_shapes=[pltpu.VMEM((3, nb, cb, seq), F32),
                        pltpu.VMEM((2, HY_UNIT, 2 * DFT2), F32), pltpu.VMEM((2, HY_UNIT, 2 * DFT2), F32),
                        pltpu.VMEM((n1 * rc, 2 * DFT2), F32), pltpu.VMEM((n1 * rc, 2 * DFT2), F32)],
        compiler_params=_cparams(("parallel",)),
        name="hyena",
    )(hyt.reshape(nb, 3, D_HY, seq), cw, cbias, bias, kf, fh, fl, ih, il, tw)


def _out_proj_kernel(x_ref, or_ref, oht_ref, mod_ref, gr_ref, gh_ref, wr_ref, wh_ref, g2_ref, x1_ref, xn_ref):
    orn = (_rms(or_ref[0]) * gr_ref[...]).astype(BF16)
    oh = oht_ref[0]
    ohn = oh * lax.rsqrt(jnp.mean(oh * oh, axis=0, keepdims=True) + EPS) * gh_ref[...]
    o = jnp.dot(orn, wr_ref[...], preferred_element_type=F32)
    o = o + lax.dot_general(ohn.astype(BF16), wh_ref[...], (((0,), (0,)), ((), ())), preferred_element_type=F32)
    x1 = x_ref[0] + mod_ref[0, 2:3, :] * o
    x1_ref[0] = x1
    xn = _rms(x1) * g2_ref[...]
    xn_ref[0] = (xn * (1.0 + mod_ref[0, 4:5, :]) + mod_ref[0, 3:4, :]).astype(BF16)


def _out_proj(x, o_r, oht, mod, gr, gh, wr, wh, g2, tm):
    b, l, _ = x.shape
    mod_map = (lambda i, j: (i, 0, 0)) if mod.shape[0] > 1 else (lambda i, j: (0, 0, 0))
    return pl.pallas_call(
        _out_proj_kernel,
        grid=(b, l // tm),
        in_specs=[pl.BlockSpec((1, tm, D_MODEL), lambda i, j: (i, j, 0)),
                  pl.BlockSpec((1, tm, D_RNN), lambda i, j: (i, j, 0)),
                  pl.BlockSpec((1, D_HY, tm), lambda i, j: (i, 0, j)),
                  pl.BlockSpec((1, N_MOD, D_MODEL), mod_map),
                  pl.BlockSpec((1, D_RNN), lambda i, j: (0, 0)),
                  pl.BlockSpec((D_HY, 1), lambda i, j: (0, 0)),
                  pl.BlockSpec((D_RNN, D_MODEL), lambda i, j: (0, 0)),
                  pl.BlockSpec((D_HY, D_MODEL), lambda i, j: (0, 0)),
                  pl.BlockSpec((1, D_MODEL), lambda i, j: (0, 0))],
        out_specs=[pl.BlockSpec((1, tm, D_MODEL), lambda i, j: (i, j, 0)),
                   pl.BlockSpec((1, tm, D_MODEL), lambda i, j: (i, j, 0))],
        out_shape=[jax.ShapeDtypeStruct((b, l, D_MODEL), F32),
                   jax.ShapeDtypeStruct((b, l, D_MODEL), BF16)],
        compiler_params=_cparams(("parallel", "parallel")),
        name="out_proj",
    )(x, o_r, oht, mod, gr, gh, wr, wh, g2)


FF_TM = 512
FF_SUB = 2 * LANES


def _ffn_kernel(x1_ref, xn_ref, xp_ref, xq_ref, mod_ref, wu_ref, cw_ref, cb_ref, wd_ref, gf_ref, o_ref, xe_s, h_s,
                *, seg, halo, on_grid, tiles_per_seq, final_norm):
    t = pl.program_id(0)
    tm = x1_ref.shape[1]
    rows = tm + 2 * halo
    if halo:
        first = (t % tiles_per_seq) == 0
        last = (t % tiles_per_seq) == tiles_per_seq - 1
        xe_s[0:halo, :] = jnp.where(first, jnp.zeros_like(xp_ref[0]), xp_ref[0])
        xe_s[halo + tm:rows, :] = jnp.where(last, jnp.zeros_like(xq_ref[0]), xq_ref[0])
    xe_s[halo:halo + tm, :] = xn_ref[0]

    pos = lax.broadcasted_iota(jnp.int32, (rows, 1), 0) % seg
    for c0 in range(0, D_FF, FF_SUB):
        w = min(FF_SUB, D_FF - c0)
        g = jnp.dot(xe_s[...], wu_ref[:, D_FF + c0:D_FF + c0 + w], preferred_element_type=F32)
        a = jnp.dot(xe_s[halo:halo + tm, :], wu_ref[:, c0:c0 + w], preferred_element_type=F32)
        gls = pltpu.roll(jnp.where(pos == seg - 1, 0.0, g), 1, axis=0)
        grs = pltpu.roll(jnp.where(pos == 0, 0.0, g), rows - 1, axis=0)
        acc = cb_ref[:, c0:c0 + w]
        for dr in ((-1, 0, 1) if on_grid else (0,)):
            lo = halo + dr * seg
            acc = (acc + cw_ref[dr + 1, 0:1, c0:c0 + w] * gls[lo:lo + tm]
                   + cw_ref[dr + 1, 1:2, c0:c0 + w] * g[lo:lo + tm]
                   + cw_ref[dr + 1, 2:3, c0:c0 + w] * grs[lo:lo + tm])
        h_s[:, c0:c0 + w] = (jax.nn.gelu(acc) * a).astype(BF16)
    y = jnp.dot(h_s[...], wd_ref[...], preferred_element_type=F32)
    x2 = x1_ref[0] + mod_ref[0, 5:6, :] * y
    if final_norm:
        x2 = _rms(x2) * gf_ref[...]
    o_ref[0] = x2


def _ffn(x1, xn2, mod, wu, cw, cb, wd, gf, on_grid, final_norm):
    b, l, _ = x1.shape
    tm = FF_TM
    if on_grid:
        seg, halo = GRID_W, GRID_W
        tps = l // tm
    else:
        seg, halo = l, 0
        tps = 1
    nt = b * l // tm
    hb = GRID_W
    nhb = tm // hb
    last_hb = b * l // hb - 1
    mod_map = (lambda t: (t // tps, 0, 0)) if mod.shape[0] > 1 else (lambda t: (0, 0, 0))
    kern = functools.partial(_ffn_kernel, seg=seg, halo=halo, on_grid=on_grid, tiles_per_seq=tps, final_norm=final_norm)
    const = lambda *shape: pl.BlockSpec(shape, lambda t: (0,) * len(shape), pipeline_mode=pl.Buffered(1))
    xh = xn2.reshape(b * l // hb, hb, D_MODEL)
    out = pl.pallas_call(
        kern,
        grid=(nt,),
        in_specs=[pl.BlockSpec((1, tm, D_MODEL), lambda t: (t, 0, 0)),
                  pl.BlockSpec((1, tm, D_MODEL), lambda t: (t, 0, 0)),
                  pl.BlockSpec((1, hb, D_MODEL), lambda t: (jnp.maximum(t * nhb - 1, 0), 0, 0)),
                  pl.BlockSpec((1, hb, D_MODEL), lambda t: (jnp.minimum((t + 1) * nhb, last_hb), 0, 0)),
                  pl.BlockSpec((1, N_MOD, D_MODEL), mod_map),
                  const(D_MODEL, 2 * D_FF), const(3, 3, D_FF), const(1, D_FF), const(D_FF, D_MODEL), const(1, D_MODEL)],
        out_specs=pl.BlockSpec((1, tm, D_MODEL), lambda t: (t, 0, 0)),
        out_shape=jax.ShapeDtypeStruct((nt, tm, D_MODEL), F32),
        scratch_shapes=[pltpu.VMEM((tm + 2 * halo, D_MODEL), BF16), pltpu.VMEM((tm, D_FF), BF16)],
        compiler_params=_cparams(("parallel",)),
        name="ffn",
    )(x1.reshape(nt, tm, D_MODEL), xn2.reshape(nt, tm, D_MODEL), xh, xh, mod, wu, cw, cb, wd, gf)
    return out.reshape(b, l, D_MODEL)


def _trunk_layer(x, mod, h0, p, fargs, tables, on_grid, final_norm, g_final):
    b, l, _ = x.shape
    tm = min(512, l)
    xy, hyt, kf = _front(x, mod, p['g_norm1'], p['wxy'], p['whyt'], *fargs, tables)
    o_r, states = _rglru(xy, p['rg_conv_w'], p['rg_conv_b'], p['wg'], p['bg'], p['ap'], h0)
    oht = _hyena(hyt, p['hy_cw'], p['hy_cb'], p['hy_bias'], kf, tables)
    x1, xn2 = _out_proj(x, o_r, oht, mod, p['g_rnn_out'], p['g_hy_out'], p['w_out_r'], p['w_out_h'], p['g_norm2'], tm)
    x2 = _ffn(x1, xn2, mod, p['w_up'], p['ffn_conv_w'], p['ffn_conv_b'], p['w_down'], g_final, on_grid, final_norm)
    return x2, states


def kernel(x_prompt, x_sample, state_rglru, c, c_ctx, w_ada, b_ada, g_norm1, g_norm2, w_in, rg_conv_w, rg_conv_b, rg_gate_w, rg_gate_b, rg_a, hy_conv_w, hy_conv_b, hf_w1, hf_b1, hf_w2, hf_b2, hf_w3, hf_freq, hy_bias, g_rnn_out, g_hy_out, w_out, w_up, ffn_conv_w, ffn_conv_b, w_down, g_final):
    depth = w_in.shape[0]
    nb_ctx, l_ctx, _ = x_prompt.shape
    nb_lat, l_lat, _ = x_sample.shape

    cc = jnp.zeros((SUBLANES, D_MODEL), F32).at[0].set(c_ctx).at[1:1 + nb_lat].set(c)
    mods = _modulation(cc, w_ada, b_ada).reshape(depth, SUBLANES, N_MOD, D_MODEL)

    tab_ctx = _dft_tables(2 * l_ctx // DFT2, _fft_rows(2 * l_ctx // DFT2, HY_CB))
    tab_lat = _dft_tables(2 * l_lat // DFT2, _fft_rows(2 * l_lat // DFT2, HY_CB))
    gf = g_final.reshape(1, D_MODEL)
    zero_h = jnp.zeros((nb_ctx, 2, D_RNN), F32)

    xp, xs = x_prompt, x_sample
    new_states = []
    for l in range(depth):
        wg, bg, ap = _gate_weights(rg_gate_w[l], rg_gate_b[l], rg_a[l])
        p = {
            'g_norm1': g_norm1[l].reshape(1, D_MODEL), 'g_norm2': g_norm2[l].reshape(1, D_MODEL),
            'wxy': w_in[l, :, :2 * D_RNN].astype(BF16), 'whyt': w_in[l, :, 2 * D_RNN:].T.astype(BF16),
            'rg_conv_w': rg_conv_w[l], 'rg_conv_b': rg_conv_b[l].reshape(1, D_RNN),
            'wg': wg, 'bg': bg, 'ap': ap,
            'hy_cw': hy_conv_w[l].reshape(3, 3, D_HY, 1), 'hy_cb': hy_conv_b[l].reshape(3, D_HY, 1),
            'hy_bias': hy_bias[l].reshape(2, D_HY, 1),
            'g_rnn_out': g_rnn_out[l].reshape(1, D_RNN), 'g_hy_out': g_hy_out[l].reshape(D_HY, 1),
            'w_out_r': w_out[l, :D_RNN].astype(BF16), 'w_out_h': w_out[l, D_RNN:].astype(BF16),
            'w_up': w_up[l].astype(BF16), 'ffn_conv_w': ffn_conv_w[l], 'ffn_conv_b': ffn_conv_b[l].reshape(1, D_FF),
            'w_down': w_down[l].astype(BF16),
        }
        fargs = (hf_w1[l], hf_b1[l], hf_w2[l], hf_b2[l], hf_w3[l], hf_freq[l])
        final = l == depth - 1
        xp, st = _trunk_layer(xp, mods[l, 0:1], zero_h, p, fargs, tab_ctx, False, final, gf)
        new_states.append(st)
        xs, _ = _trunk_layer(xs, mods[l, 1:1 + nb_lat], state_rglru[:, l], p, fargs, tab_lat, True, final, gf)
    return (xp, xs, jnp.stack(new_states, axis=1))
```

```python
import functools
import math

import jax
import jax.numpy as jnp
import ml_dtypes
import numpy as np
from jax import lax
from jax.experimental import pallas as pl
from jax.experimental.pallas import tpu as pltpu

F32 = jnp.float32
BF16 = jnp.bfloat16

D_MODEL = 1024
D_RNN = 512
D_HY = 512
N_HEADS = 8
HEAD_DIM = D_RNN // N_HEADS
RG_C = 8.0
GRID_W = 64
HY_BANDS = 16
HY_FH = 64
D_FF = 2816
N_MOD = 6
EPS = 1e-6

SUBLANES = 8
LANES = 128
VMEM_LIMIT = 56 * 1024 * 1024

DFT2 = LANES
FFT_PASSES = 2


def _cparams(sem):
    return pltpu.CompilerParams(dimension_semantics=sem, vmem_limit_bytes=VMEM_LIMIT)


def _rms(x):
    return x * lax.rsqrt(jnp.mean(x * x, axis=-1, keepdims=True) + EPS)


def _mod_kernel(c_ref, w_ref, b_ref, o_ref):
    c = c_ref[...]
    s = c * jax.nn.sigmoid(c)
    w = w_ref[0]
    sh, wh = s.astype(BF16), w.astype(BF16)
    sl, wl = (s - sh.astype(F32)).astype(BF16), (w - wh.astype(F32)).astype(BF16)
    o_ref[0] = (jnp.dot(sh, wh, preferred_element_type=F32) + jnp.dot(sh, wl, preferred_element_type=F32)
                + jnp.dot(sl, wh, preferred_element_type=F32)) + b_ref[0]


def _modulation(cc, w_ada, b_ada):
    depth, _, n = w_ada.shape
    tn = 1536
    return pl.pallas_call(
        _mod_kernel,
        grid=(depth, n // tn),
        in_specs=[pl.BlockSpec((SUBLANES, D_MODEL), lambda l, j: (0, 0)),
                  pl.BlockSpec((1, D_MODEL, tn), lambda l, j: (l, 0, j)),
                  pl.BlockSpec((1, 1, tn), lambda l, j: (l, 0, j))],
        out_specs=pl.BlockSpec((1, SUBLANES, tn), lambda l, j: (l, 0, j)),
        out_shape=jax.ShapeDtypeStruct((depth, SUBLANES, n), F32),
        compiler_params=_cparams(("parallel", "parallel")),
        name="adaln_mod",
    )(cc, w_ada, b_ada.reshape(depth, 1, n))


RG_HALF = D_RNN // 2
RG_TILES = RG_HALF // LANES


def _rglru_kernel(*refs, seq, tc, nbk):
    for bb in range(nbk):
        _rglru_sequence(bb, *refs, seq=seq, tc=tc)


def _rglru_sequence(bb, xr_ref, yr_ref, cw_ref, cb_ref, wg_ref, bg_ref, ap_ref, h0_ref, o_ref, st_ref,
                    ext, xc_s, hf, hb, a_s, b_s, *, seq, tc):
    nchunks = seq // tc
    nblk = tc // SUBLANES
    ext[0:SUBLANES, :] = jnp.zeros((SUBLANES, RG_HALF), F32)
    ext[SUBLANES:SUBLANES + seq, :] = xr_ref[bb]
    ext[SUBLANES + seq:2 * SUBLANES + seq, :] = jnp.zeros((SUBLANES, RG_HALF), F32)
    for c in range(nchunks):
        xc = cb_ref[...] + cw_ref[0:1, :] * ext[pl.ds(c * tc + SUBLANES - 2, tc), :]
        for k in range(1, 4):
            xc = xc + cw_ref[k:k + 1, :] * ext[pl.ds(c * tc + SUBLANES - 2 + k, tc), :]
        xc_s[pl.ds(c * tc, tc), :] = xc

    row = lax.broadcasted_iota(jnp.int32, (SUBLANES, LANES), 0)
    half_neg_c_sp = [(-0.5 * RG_C) * jax.nn.softplus(-ap_ref[d, 0]) for d in range(2)]

    def gates(c0, d):
        xc = xc_s[pl.ds(c0, tc), :]
        g = jnp.dot(xc.astype(BF16), wg_ref[d, 0], preferred_element_type=F32) + bg_ref[d, 0]
        i = 0.5 * jnp.tanh(g[:, RG_HALF:]) + 0.5
        log_a = half_neg_c_sp[d] * jnp.tanh(g[:, :RG_HALF]) + half_neg_c_sp[d]
        a = jnp.exp(log_a)
        y = jnp.tanh(-log_a) * (a * a + 1.0)
        mult = jnp.where(y > 0.0, y * lax.rsqrt(y), 0.0)
        a_s[d] = a
        b_s[d] = xc * i * mult

    def local_scan(a, b, reverse):
        for s in (1, 2, 4):
            if reverse:
                keep = row < SUBLANES - s
                shift = SUBLANES - s
            else:
                keep = row >= s
                shift = s
            a_sh = jnp.where(keep, pltpu.roll(a, shift, axis=0), 1.0)
            b_sh = jnp.where(keep, pltpu.roll(b, shift, axis=0), 0.0)
            b = a * b_sh + b
            a = a * a_sh
        return a, b

    def block_body(cf0, cb0):
        def body(j, carry):
            rf = pl.multiple_of(j * SUBLANES, SUBLANES)
            rb = pl.multiple_of(tc - SUBLANES - j * SUBLANES, SUBLANES)
            out = []
            for t in range(RG_TILES):
                lanes = slice(t * LANES, (t + 1) * LANES)
                pa, pb = local_scan(a_s[0, pl.ds(rf, SUBLANES), lanes], b_s[0, pl.ds(rf, SUBLANES), lanes], False)
                h = pa * carry[2 * t] + pb
                hf[pl.ds(cf0 + rf, SUBLANES), lanes] = h
                out.append(jnp.broadcast_to(h[SUBLANES - 1:SUBLANES, :], (SUBLANES, LANES)))
                pa, pb = local_scan(a_s[1, pl.ds(rb, SUBLANES), lanes], b_s[1, pl.ds(rb, SUBLANES), lanes], True)
                h = pa * carry[2 * t + 1] + pb
                hb[pl.ds(cb0 + rb, SUBLANES), lanes] = h
                out.append(jnp.broadcast_to(h[0:1, :], (SUBLANES, LANES)))
            return tuple(out)
        return body

    carry = []
    for t in range(RG_TILES):
        carry.append(jnp.broadcast_to(h0_ref[bb, 0:1, t * LANES:(t + 1) * LANES], (SUBLANES, LANES)))
        carry.append(jnp.broadcast_to(h0_ref[bb, 1:2, t * LANES:(t + 1) * LANES], (SUBLANES, LANES)))
    carry = tuple(carry)
    for c in range(nchunks):
        cf0 = c * tc
        cb0 = (nchunks - 1 - c) * tc
        gates(cf0, 0)
        gates(cb0, 1)
        carry = lax.fori_loop(0, nblk, block_body(cf0, cb0), carry, unroll=2)
    for t in range(RG_TILES):
        st_ref[bb, 0:1, t * LANES:(t + 1) * LANES] = carry[2 * t][0:1, :]
        st_ref[bb, 1:2, t * LANES:(t + 1) * LANES] = carry[2 * t + 1][0:1, :]
    for c in range(nchunks):
        rows = pl.ds(c * tc, tc)
        o_ref[bb, rows, :] = (hf[rows, :] + hb[rows, :]) * jax.nn.gelu(yr_ref[bb, rows, :])


def _rglru(xy, cw, cb, wg, bg, ap, h0):
    b, seq, _ = xy.shape
    tc = min(512, seq)
    nh = D_RNN // RG_HALF
    nbk = max(1, min(b, 1024 // seq))
    assert b % nbk == 0
    kern = functools.partial(_rglru_kernel, seq=seq, tc=tc, nbk=nbk)
    return pl.pallas_call(
        kern,
        grid=(b // nbk, nh),
        in_specs=[pl.BlockSpec((nbk, seq, RG_HALF), lambda i, h: (i, 0, h)),
                  pl.BlockSpec((nbk, seq, RG_HALF), lambda i, h: (i, 0, nh + h)),
                  pl.BlockSpec((4, RG_HALF), lambda i, h: (0, h)),
                  pl.BlockSpec((1, RG_HALF), lambda i, h: (0, h)),
                  pl.BlockSpec((2, 1, RG_HALF, 2 * RG_HALF), lambda i, h: (0, h, 0, 0)),
                  pl.BlockSpec((2, 1, 1, 2 * RG_HALF), lambda i, h: (0, h, 0, 0)),
                  pl.BlockSpec((2, 1, 1, RG_HALF), lambda i, h: (0, h, 0, 0)),
                  pl.BlockSpec((nbk, 2, RG_HALF), lambda i, h: (i, 0, h))],
        out_specs=[pl.BlockSpec((nbk, seq, RG_HALF), lambda i, h: (i, 0, h)),
                   pl.BlockSpec((nbk, 2, RG_HALF), lambda i, h: (i, 0, h))],
        out_shape=[jax.ShapeDtypeStruct((b, seq, D_RNN), F32),
                   jax.ShapeDtypeStruct((b, 2, D_RNN), F32)],
        scratch_shapes=[pltpu.VMEM((seq + 2 * SUBLANES, RG_HALF), F32),
                        pltpu.VMEM((seq, RG_HALF), F32),
                        pltpu.VMEM((seq, RG_HALF), F32),
                        pltpu.VMEM((seq, RG_HALF), F32),
                        pltpu.VMEM((2, tc, RG_HALF), F32),
                        pltpu.VMEM((2, tc, RG_HALF), F32)],
        compiler_params=_cparams(("parallel", "parallel")),
        name="rglru",
    )(xy, xy, cw, cb, wg, bg, ap, h0)


def _gate_weights(gate_w, gate_b, a_param):
    nh = D_RNN // RG_HALF
    hp = N_HEADS // nh
    w = gate_w.reshape(2, 2, nh, hp, HEAD_DIM, HEAD_DIM)
    eye = jnp.eye(hp, dtype=F32)
    dense = jnp.einsum('dghpio,pq->dhpigqo', w, eye)
    dense = (0.5 * dense).reshape(2, nh, RG_HALF, 2 * RG_HALF).astype(BF16)
    bias = 0.5 * gate_b.reshape(2, 2, nh, RG_HALF).transpose(0, 2, 1, 3).reshape(2, nh, 1, 2 * RG_HALF)
    ap = a_param.reshape(2, nh, 1, RG_HALF)
    return dense, bias, ap


def _add(a, b):
    if a is None:
        return b
    if b is None:
        return a
    return a + b


def _sub(a, b):
    if b is None:
        return a
    if a is None:
        return -b
    return a - b


def _scale(a, s):
    if a is None or s == 0.0:
        return None
    if s == 1.0:
        return a
    if s == -1.0:
        return -a
    return a * s


def _cmul_const(z, wr, wi):
    if z is None:
        return None
    re, im = z
    if abs(wr) < 1e-15:
        wr = 0.0
    if abs(wi) < 1e-15:
        wi = 0.0
    if wr != 0.0 and abs(abs(wr) - abs(wi)) < 1e-15 and re is not None and im is not None:
        sr, si = math.copysign(1.0, wr), math.copysign(1.0, wi)
        return (_scale(_sub(_scale(re, sr), _scale(im, si)), abs(wr)), _scale(_add(_scale(re, si), _scale(im, sr)), abs(wr)))
    return (_sub(_scale(re, wr), _scale(im, wi)), _add(_scale(re, wi), _scale(im, wr)))


def _cadd(a, b):
    if a is None:
        return b
    if b is None:
        return a
    return (_add(a[0], b[0]), _add(a[1], b[1]))


def _csub(a, b):
    if b is None:
        return a
    if a is None:
        return (_sub(None, b[0]), _sub(None, b[1]))
    return (_sub(a[0], b[0]), _sub(a[1], b[1]))


def _cfft(xs, sign):
    n = len(xs)
    if n == 1:
        return list(xs)
    even = _cfft(xs[0::2], sign)
    odd = _cfft(xs[1::2], sign)
    out = [None] * n
    for k in range(n // 2):
        ang = sign * 2.0 * math.pi * k / n
        t = _cmul_const(odd[k], math.cos(ang), math.sin(ang))
        out[k] = _cadd(even[k], t)
        out[k + n // 2] = _csub(even[k], t)
    return out


def _fft64(get, put, sign, t_ref, rc, nin, nout):
    r8 = 8
    for p in range(r8):
        if sign < 0:
            xs = [get(r8 * a + p) if r8 * a + p < nin else None for a in range(r8)]
        else:
            xs = [get(p + r8 * d) for d in range(r8)]
        ts = _cfft(xs, sign)
        for q in range(r8):
            ang = sign * 2.0 * math.pi * p * q / 64.0
            re, im = _cmul_const(ts[q], math.cos(ang), math.sin(ang))
            rows = pl.ds((p * r8 + q) * rc, rc)
            t_ref[rows, 0:LANES] = re if re is not None else jnp.zeros_like(im)
            t_ref[rows, LANES:2 * LANES] = im if im is not None else jnp.zeros_like(re)
    for q in range(r8):
        ys = [(t_ref[pl.ds((p * r8 + q) * rc, rc), 0:LANES], t_ref[pl.ds((p * r8 + q) * rc, rc), LANES:2 * LANES])
              for p in range(r8)]
        zs = _cfft(ys, sign)
        for m in range(r8):
            k = q + r8 * m if sign < 0 else r8 * m + q
            if k < nout:
                put(k, zs[m])


def _dft_tables(n1, rc):
    n = n1 * DFT2
    k = np.arange(DFT2, dtype=np.float64)
    ang = 2.0 * np.pi * np.outer(k, k) / DFT2
    c, s = np.cos(ang), np.sin(ang)
    fwd = np.block([[c, -s], [s, c]])
    inv = np.block([[c, s], [-s, c]]) / n

    def split(m):
        hi = m.astype(np.float32).astype(ml_dtypes.bfloat16)
        lo = (m - hi.astype(np.float64)).astype(ml_dtypes.bfloat16)
        return jnp.asarray(hi), jnp.asarray(lo)

    tw_ang = 2.0 * np.pi * np.outer(np.arange(n1, dtype=np.float64), k) / n
    tw = np.concatenate([np.cos(tw_ang), -np.sin(tw_ang)], axis=1)
    tw = np.repeat(tw, rc, axis=0).astype(np.float32)
    return split(fwd) + split(inv) + (jnp.asarray(tw),)


HY_CB = 32


def _fft_rows(n1, cb):
    return SUBLANES if n1 >= 16 else cb


def _dft_mm(x, hi_ref, lo_ref):
    xh = x.astype(BF16)
    acc = jnp.dot(xh, hi_ref[...], preferred_element_type=F32)
    if FFT_PASSES >= 2:
        acc = acc + jnp.dot(xh, lo_ref[...], preferred_element_type=F32)
    if FFT_PASSES >= 3:
        xl = (x - xh.astype(F32)).astype(BF16)
        acc = acc + jnp.dot(xl, hi_ref[...], preferred_element_type=F32)
    return acc


def _filter_hidden(feat_ref, w1_ref, b1_ref, w2_ref, b2_ref, fr_ref, h2h_s, h2l_s):
    hi = lax.Precision.HIGHEST

    @pl.when(pl.program_id(0) == 0)
    def _():
        h = jnp.dot(w1_ref[...], feat_ref[...], precision=hi, preferred_element_type=F32) + b1_ref[...]
        h = jnp.sin(fr_ref[:, 0:1] * h)
        h = jnp.dot(w2_ref[...], h, precision=hi, preferred_element_type=F32) + b2_ref[...]
        h2 = jnp.sin(fr_ref[:, 1:2] * h)
        h2h = h2.astype(BF16)
        h2h_s[...] = h2h
        h2l_s[...] = (h2 - h2h.astype(F32)).astype(BF16)


def _filter_block(w3_ref, fh_ref, fl_ref, tw_ref, kf_ref, h2h_s, h2l_s, k_s, s_s, t_s, *, seq, n1, cb, rc):
    n = 2 * seq
    lane = lax.broadcasted_iota(jnp.int32, (cb, n), 1)
    pos = jnp.where(lane < seq, lane, n - lane).astype(F32)
    t = pos / float(max(seq - 1, 1))
    d_idx = (lax.broadcasted_iota(jnp.int32, (cb, n), 0) + pl.program_id(0) * cb).astype(F32)
    min_decay = math.log(1e-2) / 1.5
    max_decay = math.log(1e-2) / 0.3
    delta = jnp.abs(min_decay + d_idx * ((max_decay - min_decay) / (D_HY - 1)))
    decay = jnp.exp(-t * delta)
    w3 = w3_ref[...].reshape(4 * cb, HY_FH)
    w3h = w3.astype(BF16)
    w3l = (w3 - w3h.astype(F32)).astype(BF16)
    hfb = (jnp.dot(w3h, h2h_s[...], preferred_element_type=F32) + jnp.dot(w3h, h2l_s[...], preferred_element_type=F32)
           + jnp.dot(w3l, h2h_s[...], preferred_element_type=F32))
    for o in range(2):
        hf = hfb[(2 * o) * cb:(2 * o + 1) * cb]
        hb = hfb[(2 * o + 1) * cb:(2 * o + 2) * cb]
        k_s[o] = jnp.where(lane < seq, hf, jnp.where(lane == seq, 0.0, hb)) * decay

    nrc = cb // rc
    for o in range(2):
        for i in range(nrc):
            r0 = i * rc
            base = i * (n1 * rc)

            def get(j, o=o, r0=r0):
                return (k_s[o, pl.ds(r0, rc), j * LANES:(j + 1) * LANES], None)

            def put(k1, z, base=base):
                re, im = z
                if k1 > 0:
                    twr = tw_ref[k1 * rc:(k1 + 1) * rc, 0:LANES]
                    twi = tw_ref[k1 * rc:(k1 + 1) * rc, LANES:2 * LANES]
                    re, im = (re * twr, re * twi) if im is None else (re * twr - im * twi, re * twi + im * twr)
                s_s[pl.ds(base + k1 * rc, rc), 0:LANES] = re
                s_s[pl.ds(base + k1 * rc, rc), LANES:2 * LANES] = im if im is not None else jnp.zeros_like(re)

            if n1 == 64:
                _fft64(get, put, -1, t_s, rc, n1, n1)
            else:
                zs = _cfft([get(j) for j in range(n1)], -1)
                for k1 in range(n1):
                    put(k1, zs[k1])
        z = _dft_mm(s_s[...], fh_ref, fl_ref)
        for i in range(nrc):
            kf_ref[o, i] = z[i * n1 * rc:(i + 1) * n1 * rc, :]


FRONT_TM = 512


def _front_kernel(x_ref, mod_ref, g_ref, wxy_ref, whyt_ref, feat_ref, w1_ref, b1_ref, w2_ref, b2_ref, fr_ref, w3_ref,
                  fh_ref, fl_ref, tw_ref, xy_ref, hyt_ref, kf_ref, h2h_s, h2l_s, k_s, s_s, t_s, *, seq, n1, cb, rc):
    _filter_hidden(feat_ref, w1_ref, b1_ref, w2_ref, b2_ref, fr_ref, h2h_s, h2l_s)
    for q in range(x_ref.shape[0]):
        xn = _rms(x_ref[q]) * g_ref[...]
        xn = (xn * (1.0 + mod_ref[0, 1:2, :]) + mod_ref[0, 0:1, :]).astype(BF16)
        xy_ref[q] = jnp.dot(xn, wxy_ref[...], preferred_element_type=F32)
        hyt_ref[q] = lax.dot_general(whyt_ref[...], xn, (((1,), (1,)), ((), ())), preferred_element_type=F32)
    _filter_block(w3_ref, fh_ref, fl_ref, tw_ref, kf_ref, h2h_s, h2l_s, k_s, s_s, t_s, seq=seq, n1=n1, cb=cb, rc=rc)


def _front(x, mod, g, wxy, whyt, w1, b1, w2, b2, w3, freq, tables):
    b, seq, _ = x.shape
    n1 = 2 * seq // DFT2
    n = 2 * seq
    rc = _fft_rows(n1, HY_CB)
    tmb = min(FRONT_TM, seq)
    nbt = FRONT_TM // tmb
    steps = b * seq // FRONT_TM
    cb = D_HY // steps
    assert cb % rc == 0 and cb * steps == D_HY
    tps = seq // tmb
    fh, fl, _, _, tw = tables
    pos = np.arange(n, dtype=np.float64)
    pos = np.where(pos < seq, pos, n - pos)
    tt = pos / max(seq - 1, 1)
    omega = 2.0 * math.pi * pos / seq
    bands = np.linspace(1e-4, HY_BANDS - 1, HY_BANDS)
    ang = omega[None, :] * bands[:, None]
    feats = np.concatenate([tt[None, :], np.cos(ang), np.sin(ang)], axis=0)
    nfeat = 5 * SUBLANES
    feats = np.pad(feats, ((0, nfeat - feats.shape[0]), (0, 0))).astype(np.float32)
    w1 = jnp.pad(w1, ((0, nfeat - w1.shape[0]), (0, 0)))
    w3t = w3.T.reshape(4, D_HY, HY_FH)
    kern = functools.partial(_front_kernel, seq=seq, n1=n1, cb=cb, rc=rc)
    const = lambda *shape: pl.BlockSpec(shape, lambda i: (0,) * len(shape), pipeline_mode=pl.Buffered(1))
    mod_map = (lambda i: (i // tps, 0, 0)) if mod.shape[0] > 1 else (lambda i: (0, 0, 0))
    xy, hyt, kf = pl.pallas_call(
        kern,
        grid=(steps,),
        in_specs=[pl.BlockSpec((nbt, tmb, D_MODEL), lambda i: (i, 0, 0)),
                  pl.BlockSpec((1, N_MOD, D_MODEL), mod_map),
                  const(1, D_MODEL), const(D_MODEL, 2 * D_RNN), const(3 * D_HY, D_MODEL),
                  const(nfeat, n), const(HY_FH, nfeat), const(HY_FH, 1), const(HY_FH, HY_FH), const(HY_FH, 1),
                  const(HY_FH, 2),
                  pl.BlockSpec((4, cb, HY_FH), lambda i: (0, i, 0)),
                  const(2 * DFT2, 2 * DFT2), const(2 * DFT2, 2 * DFT2), const(n1 * rc, 2 * DFT2)],
        out_specs=[pl.BlockSpec((nbt, tmb, 2 * D_RNN), lambda i: (i, 0, 0)),
                   pl.BlockSpec((nbt, 3 * D_HY, tmb), lambda i: (i // tps, 0, i % tps)),
                   pl.BlockSpec((2, cb // rc, n1 * rc, 2 * DFT2), lambda i: (0, i, 0, 0))],
        out_shape=[jax.ShapeDtypeStruct((b * seq // tmb, tmb, 2 * D_RNN), F32),
                   jax.ShapeDtypeStruct((b, 3 * D_HY, seq), F32),
                   jax.ShapeDtypeStruct((2, D_HY // rc, n1 * rc, 2 * DFT2), F32)],
        scratch_shapes=[pltpu.VMEM((HY_FH, n), BF16), pltpu.VMEM((HY_FH, n), BF16), pltpu.VMEM((2, cb, n), F32),
                        pltpu.VMEM((n1 * cb, 2 * DFT2), F32), pltpu.VMEM((n1 * rc, 2 * DFT2), F32)],
        compiler_params=_cparams(("arbitrary",)),
        name="front",
    )(x.reshape(b * seq // tmb, tmb, D_MODEL), mod, g, wxy, whyt,
      jnp.asarray(feats), w1.T, b1.reshape(HY_FH, 1), w2.T, b2.reshape(HY_FH, 1), freq.T, w3t, fh, fl, tw)
    return xy.reshape(b, seq, 2 * D_RNN), hyt, kf


HY_UNIT = 512


def _hyena_kernel(hy_ref, cw_ref, cb_ref, bias_ref, kf_ref, fh_ref, fl_ref, ih_ref, il_ref, tw_ref, o_ref,
                  buf, sa, sb, ta, tc, *, seq, n1, cb, nb, rc, pu):
    npairs = nb // 2
    nin = n1 // 2
    nrc = cb // rc
    upo = (npairs // pu) * nrc
    nitems = 2 * upo
    assert upo >= 3, "an item's second-order stage A must come after its first-order stage C"
    lane = lax.broadcasted_iota(jnp.int32, (rc, seq), 1)

    def conv3(h, part, r0):
        w0 = cw_ref[0, part, pl.ds(r0, rc), :]
        w1 = cw_ref[1, part, pl.ds(r0, rc), :]
        w2 = cw_ref[2, part, pl.ds(r0, rc), :]
        bb = cb_ref[part, pl.ds(r0, rc), :]
        hm = jnp.where(lane == 0, 0.0, pltpu.roll(h, 1, axis=1))
        hp = jnp.where(lane == seq - 1, 0.0, pltpu.roll(h, seq - 1, axis=1))
        return w0 * hm + w1 * h + w2 * hp + bb

    def prep(i, _):
        b = i // nrc
        r0 = pl.multiple_of((i % nrc) * rc, rc)
        buf[0, b, pl.ds(r0, rc), :] = conv3(hy_ref[b, 0, pl.ds(r0, rc), :], 0, r0)
        return 0
    lax.fori_loop(0, nb * nrc, prep, 0, unroll=2)

    def item(j):
        o = j // upo
        u = j % upo
        return o, u // nrc, u % nrc

    def stage_a(j, slot):
        o, pg, rg = item(j)
        r0 = pl.multiple_of(rg * rc, rc)
        for pp in range(pu):
            p = pg * pu + pp
            def get(jb, p=p):
                return (buf[o, 2 * p, pl.ds(r0, rc), jb * LANES:(jb + 1) * LANES],
                        buf[o, 2 * p + 1, pl.ds(r0, rc), jb * LANES:(jb + 1) * LANES])

            def put(k1, z, pp=pp):
                re, im = z
                if k1 > 0:
                    twr = tw_ref[k1 * rc:(k1 + 1) * rc, 0:LANES]
                    twi = tw_ref[k1 * rc:(k1 + 1) * rc, LANES:2 * LANES]
                    re, im = re * twr - im * twi, re * twi + im * twr
                rows = pl.ds((pp * n1 + k1) * rc, rc)
                sa[slot, rows, 0:LANES] = re
                sa[slot, rows, LANES:2 * LANES] = im

            if n1 == 64:
                _fft64(get, put, -1, ta, rc, nin, n1)
            else:
                zs = _cfft([get(jb) if jb < nin else None for jb in range(n1)], -1)
                for k1 in range(n1):
                    put(k1, zs[k1])

    def stage_b(j, slot):
        o, _, rg = item(j)
        z = _dft_mm(sa[slot], fh_ref, fl_ref)
        kk = kf_ref[o, rg]
        if pu > 1:
            kk = jnp.concatenate([kk] * pu, axis=0)
        zr, zi = z[:, :LANES], z[:, LANES:]
        kr, ki = kk[:, :LANES], kk[:, LANES:]
        w = jnp.concatenate([zr * kr - zi * ki, zr * ki + zi * kr], axis=1)
        sb[slot] = _dft_mm(w, ih_ref, il_ref)

    def stage_c(j, slot):
        o, pg, rg = item(j)
        r0 = pl.multiple_of(rg * rc, rc)
        bias = bias_ref[o, pl.ds(r0, rc), :]
        for pp in range(pu):
            p = pg * pu + pp
            xg = [conv3(hy_ref[2 * p + q, 1 + o, pl.ds(r0, rc), :], 1 + o, r0) for q in range(2)]

            def get(k1, pp=pp):
                re = sb[slot, pl.ds((pp * n1 + k1) * rc, rc), 0:LANES]
                im = sb[slot, pl.ds((pp * n1 + k1) * rc, rc), LANES:2 * LANES]
                if k1 > 0:
                    twr = tw_ref[k1 * rc:(k1 + 1) * rc, 0:LANES]
                    twi = tw_ref[k1 * rc:(k1 + 1) * rc, LANES:2 * LANES]
                    re, im = re * twr + im * twi, im * twr - re * twi
                return (re, im)

            def put(jb, y, p=p, xg=xg):
                lanes = slice(jb * LANES, (jb + 1) * LANES)
                for q in range(2):
                    u = buf[o, 2 * p + q, pl.ds(r0, rc), lanes]
                    buf[o + 1, 2 * p + q, pl.ds(r0, rc), lanes] = xg[q][:, lanes] * (y[q] + u * bias)

            if n1 == 64:
                _fft64(get, put, +1, tc, rc, n1, nin)
            else:
                ys = _cfft([get(k1) for k1 in range(n1)], +1)
                for jb in range(nin):
                    put(jb, ys[jb])

    stage_a(0, 0)
    stage_a(1, 1)
    stage_b(0, 0)

    assert nitems % 2 == 0

    def steady(i, _):
        t = 2 + 2 * i
        stage_b(t - 1, 1)
        stage_a(t, 0)
        stage_c(t - 2, 0)
        stage_b(t, 0)
        stage_a(t + 1, 1)
        stage_c(t - 1, 1)
        return 0
    lax.fori_loop(0, (nitems - 2) // 2, steady, 0)
    stage_b(nitems - 1, (nitems - 1) % 2)
    stage_c(nitems - 2, nitems % 2)
    stage_c(nitems - 1, (nitems - 1) % 2)
    for b in range(nb):
        o_ref[b] = buf[2, b]


def _hyena(hyt, cw, cbias, bias, kf, tables):
    nb, _, seq = hyt.shape
    n1 = 2 * seq // DFT2
    cb = HY_CB
    rc = _fft_rows(n1, cb)
    pu = HY_UNIT // (n1 * rc)
    nblk = D_HY // cb
    fh, fl, ih, il, tw = tables
    kern = functools.partial(_hyena_kernel, seq=seq, n1=n1, cb=cb, nb=nb, rc=rc, pu=pu)
    full = lambda *shape: pl.BlockSpec(shape, lambda i: (0,) * len(shape))
    return pl.pallas_call(
        kern,
        grid=(nblk,),
        in_specs=[pl.BlockSpec((nb, 3, cb, seq), lambda i: (0, 0, i, 0)),
                  pl.BlockSpec((3, 3, cb, 1), lambda i: (0, 0, i, 0)),
                  pl.BlockSpec((3, cb, 1), lambda i: (0, i, 0)),
                  pl.BlockSpec((2, cb, 1), lambda i: (0, i, 0)),
                  pl.BlockSpec((2, cb // rc, n1 * rc, 2 * DFT2), lambda i: (0, i, 0, 0)),
                  full(2 * DFT2, 2 * DFT2), full(2 * DFT2, 2 * DFT2),
                  full(2 * DFT2, 2 * DFT2), full(2 * DFT2, 2 * DFT2),
                  full(n1 * rc, 2 * DFT2)],
        out_specs=pl.BlockSpec((nb, cb, seq), lambda i: (0, i, 0)),
        out_shape=jax.ShapeDtypeStruct((nb, D_HY, seq), F32),
        scratch_shapes=[pltpu.VMEM((3, nb, cb, seq), F32),
                        pltpu.VMEM((2, HY_UNIT, 2 * DFT2), F32), pltpu.VMEM((2, HY_UNIT, 2 * DFT2), F32),
                        pltpu.VMEM((n1 * rc, 2 * DFT2), F32), pltpu.VMEM((n1 * rc, 2 * DFT2), F32)],
        compiler_params=_cparams(("parallel",)),
        name="hyena",
    )(hyt.reshape(nb, 3, D_HY, seq), cw, cbias, bias, kf, fh, fl, ih, il, tw)


def _out_proj_kernel(x_ref, or_ref, oht_ref, mod_ref, gr_ref, gh_ref, wr_ref, wh_ref, g2_ref, x1_ref, xn_ref):
    orn = (_rms(or_ref[0]) * gr_ref[...]).astype(BF16)
    oh = oht_ref[0]
    ohn = oh * lax.rsqrt(jnp.mean(oh * oh, axis=0, keepdims=True) + EPS) * gh_ref[...]
    o = jnp.dot(orn, wr_ref[...], preferred_element_type=F32)
    o = o + lax.dot_general(ohn.astype(BF16), wh_ref[...], (((0,), (0,)), ((), ())), preferred_element_type=F32)
    x1 = x_ref[0] + mod_ref[0, 2:3, :] * o
    x1_ref[0] = x1
    xn = _rms(x1) * g2_ref[...]
    xn_ref[0] = (xn * (1.0 + mod_ref[0, 4:5, :]) + mod_ref[0, 3:4, :]).astype(BF16)


def _out_proj(x, o_r, oht, mod, gr, gh, wr, wh, g2, tm):
    b, l, _ = x.shape
    mod_map = (lambda i, j: (i, 0, 0)) if mod.shape[0] > 1 else (lambda i, j: (0, 0, 0))
    return pl.pallas_call(
        _out_proj_kernel,
        grid=(b, l // tm),
        in_specs=[pl.BlockSpec((1, tm, D_MODEL), lambda i, j: (i, j, 0)),
                  pl.BlockSpec((1, tm, D_RNN), lambda i, j: (i, j, 0)),
                  pl.BlockSpec((1, D_HY, tm), lambda i, j: (i, 0, j)),
                  pl.BlockSpec((1, N_MOD, D_MODEL), mod_map),
                  pl.BlockSpec((1, D_RNN), lambda i, j: (0, 0)),
                  pl.BlockSpec((D_HY, 1), lambda i, j: (0, 0)),
                  pl.BlockSpec((D_RNN, D_MODEL), lambda i, j: (0, 0)),
                  pl.BlockSpec((D_HY, D_MODEL), lambda i, j: (0, 0)),
                  pl.BlockSpec((1, D_MODEL), lambda i, j: (0, 0))],
        out_specs=[pl.BlockSpec((1, tm, D_MODEL), lambda i, j: (i, j, 0)),
                   pl.BlockSpec((1, tm, D_MODEL), lambda i, j: (i, j, 0))],
        out_shape=[jax.ShapeDtypeStruct((b, l, D_MODEL), F32),
                   jax.ShapeDtypeStruct((b, l, D_MODEL), BF16)],
        compiler_params=_cparams(("parallel", "parallel")),
        name="out_proj",
    )(x, o_r, oht, mod, gr, gh, wr, wh, g2)


FF_TM = 512
FF_SUB = 4 * LANES


def _ffn_kernel(x1_ref, xn_ref, xp_ref, xq_ref, mod_ref, wu_ref, cw_ref, cb_ref, wd_ref, gf_ref, o_ref, xe_s, h_s,
                *, seg, halo, on_grid, tiles_per_seq, final_norm):
    t = pl.program_id(0)
    tm = x1_ref.shape[1]
    rows = tm + 2 * halo
    if halo:
        first = (t % tiles_per_seq) == 0
        last = (t % tiles_per_seq) == tiles_per_seq - 1
        xe_s[0:halo, :] = jnp.where(first, jnp.zeros_like(xp_ref[0]), xp_ref[0])
        xe_s[halo + tm:rows, :] = jnp.where(last, jnp.zeros_like(xq_ref[0]), xq_ref[0])
    xe_s[halo:halo + tm, :] = xn_ref[0]

    pos = lax.broadcasted_iota(jnp.int32, (rows, 1), 0) % seg
    for c0 in range(0, D_FF, FF_SUB):
        w = min(FF_SUB, D_FF - c0)
        g = jnp.dot(xe_s[...], wu_ref[:, D_FF + c0:D_FF + c0 + w], preferred_element_type=F32)
        a = jnp.dot(xe_s[halo:halo + tm, :], wu_ref[:, c0:c0 + w], preferred_element_type=F32)
        gls = pltpu.roll(jnp.where(pos == seg - 1, 0.0, g), 1, axis=0)
        grs = pltpu.roll(jnp.where(pos == 0, 0.0, g), rows - 1, axis=0)
        acc = cb_ref[:, c0:c0 + w]
        for dr in ((-1, 0, 1) if on_grid else (0,)):
            lo = halo + dr * seg
            acc = (acc + cw_ref[dr + 1, 0:1, c0:c0 + w] * gls[lo:lo + tm]
                   + cw_ref[dr + 1, 1:2, c0:c0 + w] * g[lo:lo + tm]
                   + cw_ref[dr + 1, 2:3, c0:c0 + w] * grs[lo:lo + tm])
        h_s[:, c0:c0 + w] = (jax.nn.gelu(acc) * a).astype(BF16)
    y = jnp.dot(h_s[...], wd_ref[...], preferred_element_type=F32)
    x2 = x1_ref[0] + mod_ref[0, 5:6, :] * y
    if final_norm:
        x2 = _rms(x2) * gf_ref[...]
    o_ref[0] = x2


def _ffn(x1, xn2, mod, wu, cw, cb, wd, gf, on_grid, final_norm):
    b, l, _ = x1.shape
    tm = FF_TM
    if on_grid:
        seg, halo = GRID_W, GRID_W
        tps = l // tm
    else:
        seg, halo = l, 0
        tps = 1
    nt = b * l // tm
    hb = GRID_W
    nhb = tm // hb
    last_hb = b * l // hb - 1
    mod_map = (lambda t: (t // tps, 0, 0)) if mod.shape[0] > 1 else (lambda t: (0, 0, 0))
    kern = functools.partial(_ffn_kernel, seg=seg, halo=halo, on_grid=on_grid, tiles_per_seq=tps, final_norm=final_norm)
    const = lambda *shape: pl.BlockSpec(shape, lambda t: (0,) * len(shape), pipeline_mode=pl.Buffered(1))
    xh = xn2.reshape(b * l // hb, hb, D_MODEL)
    out = pl.pallas_call(
        kern,
        grid=(nt,),
        in_specs=[pl.BlockSpec((1, tm, D_MODEL), lambda t: (t, 0, 0)),
                  pl.BlockSpec((1, tm, D_MODEL), lambda t: (t, 0, 0)),
                  pl.BlockSpec((1, hb, D_MODEL), lambda t: (jnp.maximum(t * nhb - 1, 0), 0, 0)),
                  pl.BlockSpec((1, hb, D_MODEL), lambda t: (jnp.minimum((t + 1) * nhb, last_hb), 0, 0)),
                  pl.BlockSpec((1, N_MOD, D_MODEL), mod_map),
                  const(D_MODEL, 2 * D_FF), const(3, 3, D_FF), const(1, D_FF), const(D_FF, D_MODEL), const(1, D_MODEL)],
        out_specs=pl.BlockSpec((1, tm, D_MODEL), lambda t: (t, 0, 0)),
        out_shape=jax.ShapeDtypeStruct((nt, tm, D_MODEL), F32),
        scratch_shapes=[pltpu.VMEM((tm + 2 * halo, D_MODEL), BF16), pltpu.VMEM((tm, D_FF), BF16)],
        compiler_params=_cparams(("parallel",)),
        name="ffn",
    )(x1.reshape(nt, tm, D_MODEL), xn2.reshape(nt, tm, D_MODEL), xh, xh, mod, wu, cw, cb, wd, gf)
    return out.reshape(b, l, D_MODEL)


def _trunk_layer(x, mod, h0, p, fargs, tables, on_grid, final_norm, g_final):
    b, l, _ = x.shape
    tm = min(512, l)
    xy, hyt, kf = _front(x, mod, p['g_norm1'], p['wxy'], p['whyt'], *fargs, tables)
    o_r, states = _rglru(xy, p['rg_conv_w'], p['rg_conv_b'], p['wg'], p['bg'], p['ap'], h0)
    oht = _hyena(hyt, p['hy_cw'], p['hy_cb'], p['hy_bias'], kf, tables)
    x1, xn2 = _out_proj(x, o_r, oht, mod, p['g_rnn_out'], p['g_hy_out'], p['w_out_r'], p['w_out_h'], p['g_norm2'], tm)
    x2 = _ffn(x1, xn2, mod, p['w_up'], p['ffn_conv_w'], p['ffn_conv_b'], p['w_down'], g_final, on_grid, final_norm)
    return x2, states


def kernel(x_prompt, x_sample, state_rglru, c, c_ctx, w_ada, b_ada, g_norm1, g_norm2, w_in, rg_conv_w, rg_conv_b, rg_gate_w, rg_gate_b, rg_a, hy_conv_w, hy_conv_b, hf_w1, hf_b1, hf_w2, hf_b2, hf_w3, hf_freq, hy_bias, g_rnn_out, g_hy_out, w_out, w_up, ffn_conv_w, ffn_conv_b, w_down, g_final):
    depth = w_in.shape[0]
    nb_ctx, l_ctx, _ = x_prompt.shape
    nb_lat, l_lat, _ = x_sample.shape

    cc = jnp.zeros((SUBLANES, D_MODEL), F32).at[0].set(c_ctx).at[1:1 + nb_lat].set(c)
    mods = _modulation(cc, w_ada, b_ada).reshape(depth, SUBLANES, N_MOD, D_MODEL)

    tab_ctx = _dft_tables(2 * l_ctx // DFT2, _fft_rows(2 * l_ctx // DFT2, HY_CB))
    tab_lat = _dft_tables(2 * l_lat // DFT2, _fft_rows(2 * l_lat // DFT2, HY_CB))
    gf = g_final.reshape(1, D_MODEL)
    zero_h = jnp.zeros((nb_ctx, 2, D_RNN), F32)

    xp, xs = x_prompt, x_sample
    new_states = []
    for l in range(depth):
        wg, bg, ap = _gate_weights(rg_gate_w[l], rg_gate_b[l], rg_a[l])
        p = {
            'g_norm1': g_norm1[l].reshape(1, D_MODEL), 'g_norm2': g_norm2[l].reshape(1, D_MODEL),
            'wxy': w_in[l, :, :2 * D_RNN].astype(BF16), 'whyt': w_in[l, :, 2 * D_RNN:].T.astype(BF16),
            'rg_conv_w': rg_conv_w[l], 'rg_conv_b': rg_conv_b[l].reshape(1, D_RNN),
            'wg': wg, 'bg': bg, 'ap': ap,
            'hy_cw': hy_conv_w[l].reshape(3, 3, D_HY, 1), 'hy_cb': hy_conv_b[l].reshape(3, D_HY, 1),
            'hy_bias': hy_bias[l].reshape(2, D_HY, 1),
            'g_rnn_out': g_rnn_out[l].reshape(1, D_RNN), 'g_hy_out': g_hy_out[l].reshape(D_HY, 1),
            'w_out_r': w_out[l, :D_RNN].astype(BF16), 'w_out_h': w_out[l, D_RNN:].astype(BF16),
            'w_up': w_up[l].astype(BF16), 'ffn_conv_w': ffn_conv_w[l], 'ffn_conv_b': ffn_conv_b[l].reshape(1, D_FF),
            'w_down': w_down[l].astype(BF16),
        }
        fargs = (hf_w1[l], hf_b1[l], hf_w2[l], hf_b2[l], hf_w3[l], hf_freq[l])
        final = l == depth - 1
        xp, st = _trunk_layer(xp, mods[l, 0:1], zero_h, p, fargs, tab_ctx, False, final, gf)
        new_states.append(st)
        xs, _ = _trunk_layer(xs, mods[l, 1:1 + nb_lat], state_rglru[:, l], p, fargs, tab_lat, True, final, gf)
    return (xp, xs, jnp.stack(new_states, axis=1))
```
